```python
import math
import jax
import jax.numpy as jnp
from jax import lax
import numpy as np

D_MODEL = 1024
BATCH = 1
SEQ = 16384
DEPTH = 2
DEC_BATCH = 8
DEC_SEQ = 16
PAST_LEN = 4096

CHUNK = 64
ATT_HEADS = 8
ATT_HEAD_DIM = 64
ATT_V_DIM = 2 * ATT_HEAD_DIM
ATT_WIDTH = ATT_HEADS * ATT_V_DIM
Q_DIM = ATT_HEADS * 2 * ATT_HEAD_DIM
ATT_Q_BLOCK = 128
ATT_SUBLN_EPS = 1e-5
SSM_EXPAND = 2
D_INNER = SSM_EXPAND * D_MODEL
SSM_HEAD_DIM = 64
SSM_HEADS = D_INNER // SSM_HEAD_DIM
SSM_GROUPS = 8
SSM_HEADS_PER_GROUP = SSM_HEADS // SSM_GROUPS
SSM_STATE = 128
CONV_WIDTH = 4
CONV_DIM = D_INNER + 2 * SSM_GROUPS * SSM_STATE
SSM_NORM_EPS = 1e-5
D_FF = -(-8 * D_MODEL // (3 * 256)) * 256
RMS_EPS = 1e-6
IN_SIZES = (Q_DIM, Q_DIM, ATT_WIDTH, D_INNER, CONV_DIM, SSM_HEADS, 2 * D_MODEL)
IN_DIM = sum(IN_SIZES)
SPLIT_POINTS = tuple(sum(IN_SIZES[:i + 1]) for i in range(len(IN_SIZES) - 1))

kernel_name = "hybrid_diffattn_mamba2_streaming_step"


def rmsnorm(x, w, eps=RMS_EPS):
    xf = x.astype(jnp.float32)
    y = xf * lax.rsqrt(jnp.mean(xf * xf, axis=-1, keepdims=True) + eps)
    return (y * w.astype(jnp.float32)).astype(x.dtype)


def diff_attention(q, k, v, lam, q_pos0):
    b, lq = q.shape[0], q.shape[1]
    lk = k.shape[1]
    blk = min(ATT_Q_BLOCK, lq)
    nblk = lq // blk
    scale = ATT_HEAD_DIM ** -0.5
    k_chunk = jnp.arange(lk) // CHUNK
    qb = q.reshape(b, nblk, blk, ATT_HEADS, 2, ATT_HEAD_DIM).swapaxes(0, 1)

    def one_block(args):
        q_blk, i = args
        s = jnp.einsum("bqhmd,bkhmd->bhmqk", q_blk, k, preferred_element_type=jnp.float32) * scale
        q_chunk = (q_pos0 + i * blk + jnp.arange(blk)) // CHUNK
        mask = k_chunk[None, :] <= q_chunk[:, None]
        s = jnp.where(mask, s, -jnp.inf)
        p = jax.nn.softmax(s, axis=-1)
        a = p[:, :, 0] - lam * p[:, :, 1]
        return jnp.einsum("bhqk,bkhe->bqhe", a.astype(v.dtype), v)

    out = lax.map(one_block, (qb, jnp.arange(nblk)))
    return out.swapaxes(0, 1).reshape(b, lq, ATT_HEADS, ATT_V_DIM)


def ssd_scan(x, a, bm, cm, init_state):
    b, L = x.shape[0], x.shape[1]
    cl = min(CHUNK, L)
    nc = L // cl
    f32 = jnp.float32
    x = x.astype(f32).reshape(b, nc, cl, SSM_GROUPS, SSM_HEADS_PER_GROUP, SSM_HEAD_DIM)
    a = a.astype(f32).reshape(b, nc, cl, SSM_GROUPS, SSM_HEADS_PER_GROUP)
    bm = bm.astype(f32).reshape(b, nc, cl, SSM_GROUPS, SSM_STATE)
    cm = cm.astype(f32).reshape(b, nc, cl, SSM_GROUPS, SSM_STATE)
    a_cs = jnp.cumsum(a, axis=2)
    seg = a_cs[:, :, :, None] - a_cs[:, :, None, :]
    causal = jnp.tril(jnp.ones((cl, cl), dtype=bool))[None, None, :, :, None, None]
    decay = jnp.exp(jnp.where(causal, seg, -jnp.inf))
    cb = jnp.einsum("bclgn,bcsgn->bclsg", cm, bm)
    y_diag = jnp.einsum("bclsg,bclsgr,bcsgrp->bclgrp", cb, decay, x)
    decay_to_end = jnp.exp(a_cs[:, :, -1:] - a_cs)
    chunk_states = jnp.einsum("bclgn,bclgr,bclgrp->bcgrpn", bm, decay_to_end, x)
    chunk_decay = jnp.exp(a_cs[:, :, -1])

    def step(state, inp):
        cs, dec = inp
        return dec[..., None, None] * state + cs, state

    final, prev = lax.scan(step, init_state.astype(f32),
                           (chunk_states.swapaxes(0, 1), chunk_decay.swapaxes(0, 1)))
    prev = prev.swapaxes(0, 1)
    y_off = jnp.einsum("bclgn,bcgrpn,bclgr->bclgrp", cm, prev, jnp.exp(a_cs))
    y = (y_diag + y_off).reshape(b, L, SSM_GROUPS, SSM_HEADS_PER_GROUP, SSM_HEAD_DIM)
    return y, final


def ssd_branch(z, xbc, dt_raw, conv_buf, ssm_state, p):
    b, L = xbc.shape[0], xbc.shape[1]
    xpad = jnp.concatenate([conv_buf.astype(xbc.dtype), xbc], axis=1)
    new_conv = xpad[:, -(CONV_WIDTH - 1):]
    conv = p["conv_b"] + sum(p["conv_w"][j] * xpad[:, j:j + L] for j in range(CONV_WIDTH))
    xbc_act = jax.nn.silu(conv)
    xs, bm, cm = jnp.split(xbc_act, (D_INNER, D_INNER + SSM_GROUPS * SSM_STATE), axis=-1)
    xs = xs.reshape(b, L, SSM_GROUPS, SSM_HEADS_PER_GROUP, SSM_HEAD_DIM).astype(jnp.float32)
    bm = bm.reshape(b, L, SSM_GROUPS, SSM_STATE)
    cm = cm.reshape(b, L, SSM_GROUPS, SSM_STATE)
    dt = jax.nn.softplus(dt_raw.astype(jnp.float32) + p["dt_bias"].astype(jnp.float32))
    dt = dt.reshape(b, L, SSM_GROUPS, SSM_HEADS_PER_GROUP)
    a_neg = -jnp.exp(p["a_log"].astype(jnp.float32)).reshape(SSM_GROUPS, SSM_HEADS_PER_GROUP)
    y, final = ssd_scan(xs * dt[..., None], dt * a_neg, bm, cm, ssm_state)
    y = y + p["d_skip"].astype(jnp.float32).reshape(SSM_GROUPS, SSM_HEADS_PER_GROUP)[:, :, None] * xs
    y = y.reshape(b, L, D_INNER) * jax.nn.silu(z.astype(jnp.float32))
    yg = y.reshape(b, L, SSM_GROUPS, D_INNER // SSM_GROUPS)
    yg = yg * lax.rsqrt(jnp.mean(yg * yg, axis=-1, keepdims=True) + SSM_NORM_EPS)
    y = yg.reshape(b, L, D_INNER) * p["ssm_norm"].astype(jnp.float32)
    return y.astype(xbc.dtype), new_conv, final


def trunk_layer(x, pos0, past_k, past_v, conv_buf, ssm_state, lam_init, p):
    b, L, _ = x.shape
    h = rmsnorm(x, p["norm_mix"])
    proj = h @ p["w_in"]
    q, k, v, z, xbc, dt_raw, gate_logits = jnp.split(proj, SPLIT_POINTS, axis=-1)
    q = q.reshape(b, L, ATT_HEADS, 2, ATT_HEAD_DIM)
    k = k.reshape(b, L, ATT_HEADS, 2, ATT_HEAD_DIM)
    v = v.reshape(b, L, ATT_HEADS, ATT_V_DIM)
    if past_k is None:
        k_all, v_all = k, v
    else:
        k_all = jnp.concatenate([past_k.astype(k.dtype), k], axis=1)
        v_all = jnp.concatenate([past_v.astype(v.dtype), v], axis=1)
    lam = (jnp.exp(jnp.sum(p["lambda_q1"] * p["lambda_k1"]).astype(jnp.float32))
           - jnp.exp(jnp.sum(p["lambda_q2"] * p["lambda_k2"]).astype(jnp.float32)) + lam_init)
    att = diff_attention(q, k_all, v_all, lam, pos0)
    att = rmsnorm(att, p["attn_subln"], eps=ATT_SUBLN_EPS) * (1.0 - lam_init)
    att = att.reshape(b, L, ATT_WIDTH)
    y_ssd, new_conv, new_ssm = ssd_branch(z, xbc, dt_raw, conv_buf, ssm_state, p)
    g_att, g_ssd = jnp.split(jax.nn.sigmoid(gate_logits), 2, axis=-1)
    merged = g_att * (att @ p["w_branch_att"]) + g_ssd * (y_ssd @ p["w_branch_ssd"])
    x = x + merged @ p["w_out"]
    h2 = rmsnorm(x, p["norm_ffn"])
    gt, up = jnp.split(h2 @ p["w_gate_up"], 2, axis=-1)
    x = x + (jax.nn.silu(gt) * up) @ p["w_down"]
    return x, k, v, new_conv, new_ssm


def setup_inputs(seed: int = 0) -> dict:
    key = jax.random.key(seed)
    ks = jax.random.split(key, 32)
    f32 = jnp.float32
    nrm = lambda k, shape, s: jax.random.normal(k, shape, f32) * s
    dt0 = jnp.exp(jax.random.uniform(ks[14], (DEPTH, SSM_HEADS), f32,
                                     minval=math.log(1e-3), maxval=math.log(1e-1)))
    return {
        "x_prompt": nrm(ks[0], (BATCH, SEQ, D_MODEL), 1.0),
        "x_sample": nrm(ks[1], (DEC_BATCH, DEC_SEQ, D_MODEL), 1.0),
        "cache_k": nrm(ks[2], (DEPTH, DEC_BATCH, PAST_LEN, ATT_HEADS, 2, ATT_HEAD_DIM), 1.0),
        "cache_v": nrm(ks[3], (DEPTH, DEC_BATCH, PAST_LEN, ATT_HEADS, ATT_V_DIM), 1.0),
        "state_conv": nrm(ks[4], (DEPTH, DEC_BATCH, CONV_WIDTH - 1, CONV_DIM), 1.0),
        "state_ssm": nrm(ks[5], (DEPTH, DEC_BATCH, SSM_GROUPS, SSM_HEADS_PER_GROUP,
                                 SSM_HEAD_DIM, SSM_STATE), 0.1),
        "norm_mix": 1.0 + nrm(ks[6], (DEPTH, D_MODEL), 0.01),
        "w_in": nrm(ks[7], (DEPTH, D_MODEL, IN_DIM), D_MODEL ** -0.5),
        "lambda_q1": nrm(ks[8], (DEPTH, ATT_HEAD_DIM), 0.1),
        "lambda_k1": nrm(ks[9], (DEPTH, ATT_HEAD_DIM), 0.1),
        "lambda_q2": nrm(ks[10], (DEPTH, ATT_HEAD_DIM), 0.1),
        "lambda_k2": nrm(ks[11], (DEPTH, ATT_HEAD_DIM), 0.1),
        "attn_subln": 1.0 + nrm(ks[12], (DEPTH, ATT_V_DIM), 0.01),
        "conv_w": nrm(ks[13], (DEPTH, CONV_WIDTH, CONV_DIM), CONV_WIDTH ** -0.5),
        "conv_b": nrm(ks[15], (DEPTH, CONV_DIM), 0.01),
        "dt_bias": dt0 + jnp.log(-jnp.expm1(-dt0)),
        "a_log": jnp.log(jax.random.uniform(ks[16], (DEPTH, SSM_HEADS), f32, minval=1.0, maxval=16.0)),
        "d_skip": 1.0 + nrm(ks[17], (DEPTH, SSM_HEADS), 0.01),
        "ssm_norm": 1.0 + nrm(ks[18], (DEPTH, D_INNER), 0.01),
        "w_branch_att": nrm(ks[19], (DEPTH, ATT_WIDTH, D_MODEL), ATT_WIDTH ** -0.5),
        "w_branch_ssd": nrm(ks[20], (DEPTH, D_INNER, D_MODEL), D_INNER ** -0.5),
        "w_out": nrm(ks[21], (DEPTH, D_MODEL, D_MODEL), D_MODEL ** -0.5),
        "norm_ffn": 1.0 + nrm(ks[22], (DEPTH, D_MODEL), 0.01),
        "w_gate_up": nrm(ks[23], (DEPTH, D_MODEL, 2 * D_FF), D_MODEL ** -0.5),
        "w_down": nrm(ks[24], (DEPTH, D_FF, D_MODEL), D_FF ** -0.5),
        "norm_final": 1.0 + nrm(ks[25], (D_MODEL,), 0.01),
    }


def reference(x_prompt, x_sample, cache_k, cache_v, state_conv, state_ssm,
              norm_mix, w_in, lambda_q1, lambda_k1, lambda_q2, lambda_k2, attn_subln,
              conv_w, conv_b, dt_bias, a_log, d_skip, ssm_norm,
              w_branch_att, w_branch_ssd, w_out, norm_ffn, w_gate_up, w_down, norm_final):
    xp, xs = x_prompt, x_sample
    kp, vp, cp, sp = [], [], [], []
    kd, vd, cd, sd = [], [], [], []
    conv0 = jnp.zeros((xp.shape[0], CONV_WIDTH - 1, CONV_DIM), xp.dtype)
    ssm0 = jnp.zeros((xp.shape[0], SSM_GROUPS, SSM_HEADS_PER_GROUP, SSM_HEAD_DIM, SSM_STATE), jnp.float32)
    for l in range(DEPTH):
        p = {
            "norm_mix": norm_mix[l], "w_in": w_in[l],
            "lambda_q1": lambda_q1[l], "lambda_k1": lambda_k1[l],
            "lambda_q2": lambda_q2[l], "lambda_k2": lambda_k2[l],
            "attn_subln": attn_subln[l], "conv_w": conv_w[l], "conv_b": conv_b[l],
            "dt_bias": dt_bias[l], "a_log": a_log[l], "d_skip": d_skip[l], "ssm_norm": ssm_norm[l],
            "w_branch_att": w_branch_att[l], "w_branch_ssd": w_branch_ssd[l], "w_out": w_out[l],
            "norm_ffn": norm_ffn[l], "w_gate_up": w_gate_up[l], "w_down": w_down[l],
        }
        lam_init = 0.8 - 0.6 * math.exp(-0.3 * l)
        xp, k_new, v_new, c_new, s_new = trunk_layer(xp, 0, None, None, conv0, ssm0, lam_init, p)
        kp.append(k_new); vp.append(v_new); cp.append(c_new); sp.append(s_new)
        xs, k_new, v_new, c_new, s_new = trunk_layer(xs, PAST_LEN, cache_k[l], cache_v[l],
                                                     state_conv[l], state_ssm[l], lam_init, p)
        kd.append(k_new); vd.append(v_new); cd.append(c_new); sd.append(s_new)
    y_prompt = rmsnorm(xp, norm_final)
    y_sample = rmsnorm(xs, norm_final)
    return (y_prompt, y_sample,
            jnp.stack(kp), jnp.stack(vp), jnp.stack(cp), jnp.stack(sp),
            jnp.stack(kd), jnp.stack(vd), jnp.stack(cd), jnp.stack(sd))
```

```python
import functools
import math

import jax
import jax.numpy as jnp
from jax import lax
from jax.experimental import pallas as pl
from jax.experimental.pallas import tpu as pltpu

F32 = jnp.float32
BF16 = jnp.bfloat16

D_MODEL = 1024
DEPTH = 2
CHUNK = 64
ATT_HEADS = 8
ATT_HEAD_DIM = 64
ATT_V_DIM = 128
ATT_WIDTH = 1024
Q_DIM = 1024
ATT_SUBLN_EPS = 1e-5
D_INNER = 2048
SSM_HEAD_DIM = 64
SSM_HEADS = 32
SSM_GROUPS = 8
SSM_HEADS_PER_GROUP = 4
SSM_STATE = 128
CONV_WIDTH = 4
CONV_DIM = 4096
SSM_NORM_EPS = 1e-5
D_FF = 2816
RMS_EPS = 1e-6
GROUP_WIDTH = SSM_HEADS_PER_GROUP * SSM_HEAD_DIM

LANES = 128
SUBLANES = 8
VMEM_LIMIT_BYTES = 56 * 1024 * 1024

PROJ_TN = 512
SSD_CHUNK = 128
CONV_PAD = 8


def _cparams(*sem):
    return pltpu.CompilerParams(dimension_semantics=sem, vmem_limit_bytes=VMEM_LIMIT_BYTES)


def _const_spec(shape):
    return pl.BlockSpec(shape, lambda *_: (0,) * len(shape), pipeline_mode=pl.Buffered(1))


def _rms(x, w, eps):
    return x * lax.rsqrt(jnp.mean(x * x, axis=-1, keepdims=True) + eps) * w


def _split3(x):
    hi = x.astype(BF16)
    r1 = x - hi.astype(F32)
    mid = r1.astype(BF16)
    lo = (r1 - mid.astype(F32)).astype(BF16)
    return hi, mid, lo


def _dot(a, b):
    return jnp.dot(a, b, preferred_element_type=F32)


def _dot_nt(a, b):
    return lax.dot_general(a, b, (((1,), (1,)), ((), ())), preferred_element_type=F32)


_NQ = Q_DIM // PROJ_TN
_NK = Q_DIM // PROJ_TN
_NV = ATT_WIDTH // PROJ_TN
_NZ = D_INNER // PROJ_TN
_NX = CONV_DIM // PROJ_TN
_NG = 2 * D_MODEL // PROJ_TN
_OFF_K = _NQ
_OFF_V = _OFF_K + _NK
_OFF_Z = _OFF_V + _NV
_OFF_X = _OFF_Z + _NZ
_OFF_G = _OFF_X + _NX
_N_COL_TILES = _OFF_G + _NG


def _in_proj_kernel(x_ref, nw_ref, w_ref, wdt_ref,
                    q_ref, k_ref, kb_ref, v_ref, vb_ref, z_ref, xbc_ref, g_ref, dt_ref, h_sc):
    j = pl.program_id(1)

    @pl.when(j == 0)
    def _():
        hb = _rms(x_ref[...], nw_ref[...], RMS_EPS).astype(BF16)
        h_sc[...] = hb
        dt_ref[...] = _dot(hb, wdt_ref[...])

    res = _dot(h_sc[...], w_ref[...])

    @pl.when(j < _OFF_K)
    def _():
        q_ref[...] = res.astype(BF16)

    @pl.when((j >= _OFF_K) & (j < _OFF_V))
    def _():
        k_ref[...] = res
        kb_ref[...] = res.astype(BF16)

    @pl.when((j >= _OFF_V) & (j < _OFF_Z))
    def _():
        v_ref[...] = res
        vb_ref[...] = res.astype(BF16)

    @pl.when((j >= _OFF_Z) & (j < _OFF_X))
    def _():
        z_ref[...] = res

    @pl.when((j >= _OFF_X) & (j < _OFF_G))
    def _():
        xbc_ref[...] = res

    @pl.when(j >= _OFF_G)
    def _():
        g_ref[...] = res


def _in_proj(x, norm_w, w_main, w_dt, tm):
    t = x.shape[0]

    def out_spec(off, n):
        return pl.BlockSpec((tm, PROJ_TN), lambda i, j: (i, jnp.clip(j - off, 0, n - 1)))

    out_shape = (
        jax.ShapeDtypeStruct((t, Q_DIM), BF16),
        jax.ShapeDtypeStruct((t, Q_DIM), F32),
        jax.ShapeDtypeStruct((t, Q_DIM), BF16),
        jax.ShapeDtypeStruct((t, ATT_WIDTH), F32),
        jax.ShapeDtypeStruct((t, ATT_WIDTH), BF16),
        jax.ShapeDtypeStruct((t, D_INNER), F32),
        jax.ShapeDtypeStruct((t, CONV_DIM), F32),
        jax.ShapeDtypeStruct((t, 2 * D_MODEL), F32),
        jax.ShapeDtypeStruct((t, LANES), F32),
    )
    out_specs = (
        out_spec(0, _NQ), out_spec(_OFF_K, _NK), out_spec(_OFF_K, _NK),
        out_spec(_OFF_V, _NV), out_spec(_OFF_V, _NV), out_spec(_OFF_Z, _NZ),
        out_spec(_OFF_X, _NX), out_spec(_OFF_G, _NG),
        pl.BlockSpec((tm, LANES), lambda i, j: (i, 0)),
    )
    return pl.pallas_call(
        _in_proj_kernel,
        grid=(t // tm, _N_COL_TILES),
        in_specs=[
            pl.BlockSpec((tm, D_MODEL), lambda i, j: (i, 0)),
            pl.BlockSpec((1, D_MODEL), lambda i, j: (0, 0)),
            pl.BlockSpec((D_MODEL, PROJ_TN), lambda i, j: (0, j)),
            pl.BlockSpec((D_MODEL, LANES), lambda i, j: (0, 0)),
        ],
        out_specs=out_specs,
        out_shape=out_shape,
        scratch_shapes=[pltpu.VMEM((tm, D_MODEL), BF16)],
        compiler_params=_cparams("arbitrary", "arbitrary"),
        name="in_proj",
    )(x, norm_w, w_main, w_dt)


def _stack_q(q):
    lane = lax.broadcasted_iota(jnp.int32, q.shape, 1)
    zero = jnp.zeros_like(q)
    return jnp.concatenate([jnp.where(lane < ATT_HEAD_DIM, q, zero),
                            jnp.where(lane >= ATT_HEAD_DIM, q, zero)], axis=0)


def _diff_epilogue(o, n, lamp_ref, subln_ref, lam_init):
    lp = lamp_ref[...]
    lam = (jnp.exp(jnp.sum(lp[0:1] * lp[1:2], axis=-1, keepdims=True))
           - jnp.exp(jnp.sum(lp[2:3] * lp[3:4], axis=-1, keepdims=True)) + lam_init)
    a = o[:n] - lam * o[n:]
    return _rms(a, subln_ref[...], ATT_SUBLN_EPS) * (1.0 - lam_init)


def _attn_prompt_kernel(lamp_ref, subln_ref, q_ref, k_ref, v_ref, o_ref, m_sc, l_sc, acc_sc, *, tq, lam_init):
    qi = pl.program_id(1)
    qs = _stack_q(q_ref[...])
    m_sc[...] = jnp.full(m_sc.shape, -jnp.inf, F32)
    l_sc[...] = jnp.zeros(l_sc.shape, F32)
    acc_sc[...] = jnp.zeros(acc_sc.shape, F32)

    def block(start, masked):
        kb = k_ref[pl.ds(start, tq), :]
        vb = v_ref[pl.ds(start, tq), :]
        s = _dot_nt(qs, kb)
        if masked:
            r = lax.broadcasted_iota(jnp.int32, s.shape, 0)
            c = lax.broadcasted_iota(jnp.int32, s.shape, 1)
            rq = jnp.where(r >= tq, r - tq, r)
            s = jnp.where((c // CHUNK) <= (rq // CHUNK), s, -jnp.inf)
        m_prev = m_sc[...]
        m_new = jnp.maximum(m_prev, jnp.max(s, axis=-1, keepdims=True))
        alpha = jnp.exp(m_prev - m_new)
        p = jnp.exp(s - m_new)
        l_sc[...] = alpha * l_sc[...] + jnp.sum(p, axis=-1, keepdims=True)
        acc_sc[...] = alpha * acc_sc[...] + _dot(p.astype(BF16), vb)
        m_sc[...] = m_new

    def body(ki, carry):
        block(pl.multiple_of(ki * tq, tq), False)
        return carry

    lax.fori_loop(0, qi, body, 0)
    block(pl.multiple_of(qi * tq, tq), True)

    o = acc_sc[...] * (1.0 / l_sc[...])
    o_ref[...] = _diff_epilogue(o, tq, lamp_ref, subln_ref, lam_init).astype(o_ref.dtype)


def _attn_prompt(q, kb, vb, lamp, subln, lam_init, tq):
    t = q.shape[0]
    kern = functools.partial(_attn_prompt_kernel, tq=tq, lam_init=lam_init)
    return pl.pallas_call(
        kern,
        grid=(ATT_HEADS, t // tq),
        in_specs=[
            pl.BlockSpec((4, ATT_HEAD_DIM), lambda h, i: (0, 0)),
            pl.BlockSpec((1, ATT_V_DIM), lambda h, i: (0, 0)),
            pl.BlockSpec((tq, LANES), lambda h, i: (i, h)),
            pl.BlockSpec((t, LANES), lambda h, i: (0, h)),
            pl.BlockSpec((t, LANES), lambda h, i: (0, h)),
        ],
        out_specs=pl.BlockSpec((tq, LANES), lambda h, i: (i, h)),
        out_shape=jax.ShapeDtypeStruct((t, ATT_WIDTH), BF16),
        scratch_shapes=[pltpu.VMEM((2 * tq, 1), F32), pltpu.VMEM((2 * tq, 1), F32),
                        pltpu.VMEM((2 * tq, ATT_V_DIM), F32)],
        compiler_params=_cparams("arbitrary", "arbitrary"),
        name="attn_prompt",
    )(lamp, subln, q, kb, vb)


def _attn_sample_kernel(lamp_ref, subln_ref, q_ref, kc_ref, vc_ref, kn_ref, vn_ref, o_ref, *, n, lam_init):
    qs = _stack_q(q_ref[0])
    kc = kc_ref[0, 0].astype(BF16)
    vc = vc_ref[0, 0].astype(BF16)
    s_past = _dot_nt(qs, kc)
    s_new = _dot_nt(qs, kn_ref[0])
    m = jnp.maximum(jnp.max(s_past, axis=-1, keepdims=True), jnp.max(s_new, axis=-1, keepdims=True))
    p_past = jnp.exp(s_past - m)
    p_new = jnp.exp(s_new - m)
    l = jnp.sum(p_past, axis=-1, keepdims=True) + jnp.sum(p_new, axis=-1, keepdims=True)
    acc = _dot(p_past.astype(BF16), vc) + _dot(p_new.astype(BF16), vn_ref[0])
    o = acc * (1.0 / l)
    o_ref[0] = _diff_epilogue(o, n, lamp_ref, subln_ref, lam_init).astype(o_ref.dtype)


def _attn_sample(q, cache_k, cache_v, kb, vb, lamp, subln, lam_init, layer):
    b, n, _ = q.shape
    past = cache_k.shape[2]
    assert past % CHUNK == 0 and n <= CHUNK
    kern = functools.partial(_attn_sample_kernel, n=n, lam_init=lam_init)
    return pl.pallas_call(
        kern,
        grid=(b, ATT_HEADS),
        in_specs=[
            pl.BlockSpec((4, ATT_HEAD_DIM), lambda bi, h: (0, 0)),
            pl.BlockSpec((1, ATT_V_DIM), lambda bi, h: (0, 0)),
            pl.BlockSpec((1, n, LANES), lambda bi, h: (bi, 0, h)),
            pl.BlockSpec((1, 1, past, LANES), lambda bi, h: (layer, bi, 0, h)),
            pl.BlockSpec((1, 1, past, LANES), lambda bi, h: (layer, bi, 0, h)),
            pl.BlockSpec((1, n, LANES), lambda bi, h: (bi, 0, h)),
            pl.BlockSpec((1, n, LANES), lambda bi, h: (bi, 0, h)),
        ],
        out_specs=pl.BlockSpec((1, n, LANES), lambda bi, h: (bi, 0, h)),
        out_shape=jax.ShapeDtypeStruct((b, n, ATT_WIDTH), BF16),
        compiler_params=_cparams("arbitrary", "arbitrary"),
        name="attn_sample",
    )(lamp, subln, q, cache_k, cache_v, kb, vb)


def _ssd_kernel(xbc_ref, z_ref, dt_ref, cbuf_ref, st0_ref, cw_ref, cb_ref, dtb_ref, alog_ref, dsk_ref,
                nrm_ref, expand_ref, y_ref, cout_ref, sout_ref, xpad_sc, st_sc, *, lc, valid_last):
    c = pl.program_id(1)
    nchunks = pl.num_programs(1)

    @pl.when(c == 0)
    def _():
        xpad_sc[0:CONV_PAD, :] = jnp.zeros((CONV_PAD, CONV_DIM), F32)
        xpad_sc[CONV_PAD - (CONV_WIDTH - 1):CONV_PAD, :] = cbuf_ref[0]
        for g in range(SSM_GROUPS):
            st_sc[g] = st0_ref[0, g].T

    xpad_sc[CONV_PAD:CONV_PAD + lc, :] = xbc_ref[0]
    conv = cb_ref[...]
    for j in range(CONV_WIDTH):
        off = CONV_PAD - (CONV_WIDTH - 1) + j
        conv = conv + cw_ref[j:j + 1, :] * xpad_sc[off:off + lc, :]
    act = conv * jax.nn.sigmoid(conv)

    @pl.when(c == nchunks - 1)
    def _():
        end = CONV_PAD + valid_last
        cout_ref[0] = xpad_sc[end - (CONV_WIDTH - 1):end, :]

    xpad_sc[0:CONV_PAD, :] = xpad_sc[lc:lc + CONV_PAD, :]

    dtr = dt_ref[0] + dtb_ref[...]
    dt = jnp.maximum(dtr, 0.0) + jnp.log1p(jnp.exp(-jnp.abs(dtr)))
    if valid_last < lc:
        row = lax.broadcasted_iota(jnp.int32, dt.shape, 0)
        dt = jnp.where(row < valid_last, dt, 0.0)
    a = dt * (-jnp.exp(alog_ref[...]))
    rr = lax.broadcasted_iota(jnp.int32, (lc, lc), 0)
    cc = lax.broadcasted_iota(jnp.int32, (lc, lc), 1)
    causal = rr >= cc
    tri = jnp.where(causal, 1.0, 0.0).astype(BF16)
    a_cs = sum(_dot(tri, part) for part in _split3(a))
    a_last = a_cs[lc - 1:lc, :]
    a_cs_t = a_cs.T

    stack = jnp.concatenate([dt, dt * jnp.exp(a_last - a_cs), jnp.exp(a_cs),
                             jnp.broadcast_to(jnp.exp(a_last), (SUBLANES, LANES))], axis=0)
    expand = expand_ref[...]
    ex = sum(_dot(part, expand) for part in _split3(stack))
    w_dt = ex[0:lc]
    w_state = ex[lc:2 * lc]
    w_off = ex[2 * lc:3 * lc]
    w_carry = ex[3 * lc:3 * lc + 1]

    xs = act[:, :D_INNER]
    xdt = xs * w_dt
    xst = xs * w_state
    lane = lax.broadcasted_iota(jnp.int32, (lc, LANES), 1)
    lo_half = lane < SSM_HEAD_DIM

    for g in range(SSM_GROUPS):
        gs = slice(g * GROUP_WIDTH, (g + 1) * GROUP_WIDTH)
        bg = act[:, D_INNER + g * SSM_STATE:D_INNER + (g + 1) * SSM_STATE]
        cg = act[:, D_INNER + SSM_GROUPS * SSM_STATE + g * SSM_STATE:
                 D_INNER + SSM_GROUPS * SSM_STATE + (g + 1) * SSM_STATE]
        bgb = bg.astype(BF16)
        cgb = cg.astype(BF16)
        cb = _dot_nt(cgb, bgb)
        st_prev = st_sc[g]
        y_off = _dot(cgb, st_prev.astype(BF16)) * w_off[:, gs]
        st_sc[g] = w_carry[:, gs] * st_prev + _dot(bg.T.astype(BF16), xst[:, gs].astype(BF16))

        pairs = []
        for pr in range(SSM_HEADS_PER_GROUP // 2):
            x_pair = xdt[:, g * GROUP_WIDTH + pr * LANES:g * GROUP_WIDTH + (pr + 1) * LANES]
            y_pair = None
            for half in range(2):
                h = g * SSM_HEADS_PER_GROUP + pr * 2 + half
                seg = a_cs[:, h:h + 1] - a_cs_t[h:h + 1, :]
                decay = jnp.exp(jnp.where(causal, seg, -jnp.inf))
                mix = (cb * decay).astype(BF16)
                keep = lo_half if half == 0 else jnp.logical_not(lo_half)
                contrib = _dot(mix, jnp.where(keep, x_pair, 0.0).astype(BF16))
                y_pair = contrib if y_pair is None else y_pair + contrib
            pairs.append(y_pair)
        y_g = jnp.concatenate(pairs, axis=1) + y_off + dsk_ref[:, gs] * xs[:, gs]

        zg = z_ref[0, :, gs]
        yz = y_g * (zg * jax.nn.sigmoid(zg))
        y_ref[0, :, gs] = _rms(yz, nrm_ref[:, gs], SSM_NORM_EPS).astype(y_ref.dtype)

    @pl.when(c == nchunks - 1)
    def _():
        for g in range(SSM_GROUPS):
            sout_ref[0, g] = st_sc[g].T


def _ssd(xbc, z, dt, conv_buf, state0, p, valid_last):
    b, l, _ = xbc.shape
    lc = SSD_CHUNK
    assert l % lc == 0 and CONV_WIDTH - 1 <= valid_last <= lc
    assert valid_last == lc or l == lc
    kern = functools.partial(_ssd_kernel, lc=lc, valid_last=valid_last)
    return pl.pallas_call(
        kern,
        grid=(b, l // lc),
        in_specs=[
            pl.BlockSpec((1, lc, CONV_DIM), lambda bi, c: (bi, c, 0)),
            pl.BlockSpec((1, lc, D_INNER), lambda bi, c: (bi, c, 0)),
            pl.BlockSpec((1, lc, LANES), lambda bi, c: (bi, c, 0)),
            pl.BlockSpec((1, CONV_WIDTH - 1, CONV_DIM), lambda bi, c: (bi, 0, 0)),
            pl.BlockSpec((1, SSM_GROUPS, GROUP_WIDTH, SSM_STATE), lambda bi, c: (bi, 0, 0, 0)),
            pl.BlockSpec((CONV_WIDTH, CONV_DIM), lambda bi, c: (0, 0)),
            pl.BlockSpec((1, CONV_DIM), lambda bi, c: (0, 0)),
            pl.BlockSpec((1, LANES), lambda bi, c: (0, 0)),
            pl.BlockSpec((1, LANES), lambda bi, c: (0, 0)),
            pl.BlockSpec((1, D_INNER), lambda bi, c: (0, 0)),
            pl.BlockSpec((1, D_INNER), lambda bi, c: (0, 0)),
            pl.BlockSpec((LANES, D_INNER), lambda bi, c: (0, 0)),
        ],
        out_specs=(
            pl.BlockSpec((1, lc, D_INNER), lambda bi, c: (bi, c, 0)),
            pl.BlockSpec((1, CONV_WIDTH - 1, CONV_DIM), lambda bi, c: (bi, 0, 0)),
            pl.BlockSpec((1, SSM_GROUPS, GROUP_WIDTH, SSM_STATE), lambda bi, c: (bi, 0, 0, 0)),
        ),
        out_shape=(
            jax.ShapeDtypeStruct((b, l, D_INNER), BF16),
            jax.ShapeDtypeStruct((b, CONV_WIDTH - 1, CONV_DIM), F32),
            jax.ShapeDtypeStruct((b, SSM_GROUPS, GROUP_WIDTH, SSM_STATE), F32),
        ),
        scratch_shapes=[pltpu.VMEM((lc + CONV_PAD, CONV_DIM), F32),
                        pltpu.VMEM((SSM_GROUPS, SSM_STATE, GROUP_WIDTH), F32)],
        compiler_params=_cparams("arbitrary", "arbitrary"),
        name="ssd",
    )(xbc, z, dt, conv_buf, state0, p["conv_w"], p["conv_b"], p["dt_bias"], p["a_log"], p["d_skip"],
      p["ssm_norm"], p["expand"])


def _merge_kernel(x_ref, att_ref, y_ref, g_ref, wa_ref, ws_ref, wo_ref, o_ref):
    ba = _dot(att_ref[...], wa_ref[...])
    bs = _dot(y_ref[...], ws_ref[...])
    g = jax.nn.sigmoid(g_ref[...])
    merged = g[:, :D_MODEL] * ba + g[:, D_MODEL:] * bs
    o_ref[...] = x_ref[...] + _dot(merged.astype(BF16), wo_ref[...])


def _merge(x, att, y, gates, wa, ws, wo, tm):
    t = x.shape[0]
    return pl.pallas_call(
        _merge_kernel,
        grid=(t // tm,),
        in_specs=[
            pl.BlockSpec((tm, D_MODEL), lambda i: (i, 0)),
            pl.BlockSpec((tm, ATT_WIDTH), lambda i: (i, 0)),
            pl.BlockSpec((tm, D_INNER), lambda i: (i, 0)),
            pl.BlockSpec((tm, 2 * D_MODEL), lambda i: (i, 0)),
            _const_spec((ATT_WIDTH, D_MODEL)),
            _const_spec((D_INNER, D_MODEL)),
            _const_spec((D_MODEL, D_MODEL)),
        ],
        out_specs=pl.BlockSpec((tm, D_MODEL), lambda i: (i, 0)),
        out_shape=jax.ShapeDtypeStruct((t, D_MODEL), F32),
        compiler_params=_cparams("arbitrary"),
        name="merge",
    )(x, att, y, gates, wa, ws, wo)


FFN_SPLIT = 2
FFN_TF = D_FF // FFN_SPLIT


def _ffn_kernel(x_ref, nw_ref, wgu_ref, wd_ref, nf_ref, *out_refs, final_norm):
    x = x_ref[...]
    hb = _rms(x, nw_ref[...], RMS_EPS).astype(BF16)
    acc = x
    for c in range(FFN_SPLIT):
        gt = _dot(hb, wgu_ref[:, c * FFN_TF:(c + 1) * FFN_TF])
        up = _dot(hb, wgu_ref[:, D_FF + c * FFN_TF:D_FF + (c + 1) * FFN_TF])
        act = (gt * jax.nn.sigmoid(gt) * up).astype(BF16)
        acc = acc + _dot(act, wd_ref[c * FFN_TF:(c + 1) * FFN_TF, :])
    out_refs[0][...] = acc
    if final_norm:
        out_refs[1][...] = _rms(acc, nf_ref[...], RMS_EPS)


def _ffn(x, norm_w, wgu, wd, norm_final, final_norm, tm):
    t = x.shape[0]
    row_spec = pl.BlockSpec((tm, D_MODEL), lambda i: (i, 0))
    n_out = 2 if final_norm else 1
    return pl.pallas_call(
        functools.partial(_ffn_kernel, final_norm=final_norm),
        grid=(t // tm,),
        in_specs=[
            row_spec,
            pl.BlockSpec((1, D_MODEL), lambda i: (0, 0)),
            _const_spec((D_MODEL, 2 * D_FF)),
            _const_spec((D_FF, D_MODEL)),
            pl.BlockSpec((1, D_MODEL), lambda i: (0, 0)),
        ],
        out_specs=(row_spec,) * n_out,
        out_shape=(jax.ShapeDtypeStruct((t, D_MODEL), F32),) * n_out,
        compiler_params=_cparams("arbitrary"),
        name="ffn",
    )(x, norm_w, wgu, wd, norm_final)


def _layer_params(l, norm_mix, w_in, lambda_q1, lambda_k1, lambda_q2, lambda_k2, attn_subln, conv_w, conv_b,
                  dt_bias, a_log, d_skip, ssm_norm, w_branch_att, w_branch_ssd, w_out, norm_ffn, w_gate_up,
                  w_down):
    w = w_in[l]
    n_main = Q_DIM + Q_DIM + ATT_WIDTH + D_INNER + CONV_DIM
    scale = ATT_HEAD_DIM ** -0.5
    w_main = jnp.concatenate([(w[:, :Q_DIM] * scale).astype(BF16), w[:, Q_DIM:n_main].astype(BF16),
                              w[:, n_main + SSM_HEADS:].astype(BF16)], axis=1)
    w_dt = jnp.pad(w[:, n_main:n_main + SSM_HEADS], ((0, 0), (0, LANES - SSM_HEADS))).astype(BF16)
    head_of_channel = jnp.arange(D_INNER) // SSM_HEAD_DIM
    return {
        "norm_mix": norm_mix[l][None], "w_main": w_main, "w_dt": w_dt,
        "lamp": jnp.stack([lambda_q1[l], lambda_k1[l], lambda_q2[l], lambda_k2[l]]),
        "subln": attn_subln[l][None],
        "conv_w": conv_w[l], "conv_b": conv_b[l][None],
        "dt_bias": jnp.pad(dt_bias[l], (0, LANES - SSM_HEADS))[None],
        "a_log": jnp.pad(a_log[l], (0, LANES - SSM_HEADS))[None],
        "d_skip": jnp.repeat(d_skip[l], SSM_HEAD_DIM)[None],
        "ssm_norm": ssm_norm[l][None],
        "expand": (jnp.arange(LANES)[:, None] == head_of_channel[None, :]).astype(BF16),
        "wa": w_branch_att[l].astype(BF16), "ws": w_branch_ssd[l].astype(BF16), "wo": w_out[l].astype(BF16),
        "norm_ffn": norm_ffn[l][None], "wgu": w_gate_up[l].astype(BF16), "wd": w_down[l].astype(BF16),
    }


def _pad_time(x, b, n, lc):
    return jnp.pad(x.reshape(b, n, x.shape[-1]), ((0, 0), (0, lc - n), (0, 0)))


def kernel(x_prompt, x_sample, cache_k, cache_v, state_conv, state_ssm, norm_mix, w_in, lambda_q1, lambda_k1,
           lambda_q2, lambda_k2, attn_subln, conv_w, conv_b, dt_bias, a_log, d_skip, ssm_norm, w_branch_att,
           w_branch_ssd, w_out, norm_ffn, w_gate_up, w_down, norm_final):
    bp, lp, _ = x_prompt.shape
    bs, ls, _ = x_sample.shape
    assert bp == 1
    past = cache_k.shape[2]
    tp, ts = bp * lp, bs * ls
    xp = x_prompt.reshape(tp, D_MODEL)
    xs = x_sample.reshape(ts, D_MODEL)
    cache_k = cache_k.reshape(DEPTH, bs, past, Q_DIM)
    cache_v = cache_v.reshape(DEPTH, bs, past, ATT_WIDTH)
    state_ssm = state_ssm.reshape(DEPTH, bs, SSM_GROUPS, GROUP_WIDTH, SSM_STATE)
    conv0 = jnp.zeros((bp, CONV_WIDTH - 1, CONV_DIM), F32)
    ssm0 = jnp.zeros((bp, SSM_GROUPS, GROUP_WIDTH, SSM_STATE), F32)
    nf = norm_final[None]

    tm_p = 1024 if tp % 1024 == 0 else 128
    tm_row = 512 if tp % 512 == 0 else 128
    tq = 512 if lp % 512 == 0 else CHUNK

    outs_p = [[] for _ in range(4)]
    outs_s = [[] for _ in range(4)]
    yp = ys = None
    for l in range(DEPTH):
        p = _layer_params(l, norm_mix, w_in, lambda_q1, lambda_k1, lambda_q2, lambda_k2, attn_subln, conv_w,
                          conv_b, dt_bias, a_log, d_skip, ssm_norm, w_branch_att, w_branch_ssd, w_out,
                          norm_ffn, w_gate_up, w_down)
        lam_init = 0.8 - 0.6 * math.exp(-0.3 * l)
        last = l == DEPTH - 1

        q, k, kb, v, vb, z, xbc, gates, dt = _in_proj(xp, p["norm_mix"], p["w_main"], p["w_dt"], tm_p)
        att = _attn_prompt(q, kb, vb, p["lamp"], p["subln"], lam_init, tq)
        y, c_new, s_new = _ssd(xbc.reshape(bp, lp, CONV_DIM), z.reshape(bp, lp, D_INNER),
                               dt.reshape(bp, lp, LANES), conv0, ssm0, p, SSD_CHUNK)
        x1 = _merge(xp, att, y.reshape(tp, D_INNER), gates, p["wa"], p["ws"], p["wo"], tm_row)
        res = _ffn(x1, p["norm_ffn"], p["wgu"], p["wd"], nf, last, tm_row)
        xp = res[0]
        if last:
            yp = res[1]
        outs_p[0].append(k.reshape(bp, lp, ATT_HEADS, 2, ATT_HEAD_DIM))
        outs_p[1].append(v.reshape(bp, lp, ATT_HEADS, ATT_V_DIM))
        outs_p[2].append(c_new)
        outs_p[3].append(s_new.reshape(bp, SSM_GROUPS, SSM_HEADS_PER_GROUP, SSM_HEAD_DIM, SSM_STATE))

        q, k, kb, v, vb, z, xbc, gates, dt = _in_proj(xs, p["norm_mix"], p["w_main"], p["w_dt"], ts)
        att = _attn_sample(q.reshape(bs, ls, Q_DIM), cache_k, cache_v, kb.reshape(bs, ls, Q_DIM),
                           vb.reshape(bs, ls, ATT_WIDTH), p["lamp"], p["subln"], lam_init, l)
        y, c_new, s_new = _ssd(_pad_time(xbc, bs, ls, SSD_CHUNK), _pad_time(z, bs, ls, SSD_CHUNK),
                               _pad_time(dt, bs, ls, SSD_CHUNK), state_conv[l], state_ssm[l], p, ls)
        x1 = _merge(xs, att.reshape(ts, ATT_WIDTH), y[:, :ls].reshape(ts, D_INNER), gates,
                    p["wa"], p["ws"], p["wo"], ts)
        res = _ffn(x1, p["norm_ffn"], p["wgu"], p["wd"], nf, last, ts)
        xs = res[0]
        if last:
            ys = res[1]
        outs_s[0].append(k.reshape(bs, ls, ATT_HEADS, 2, ATT_HEAD_DIM))
        outs_s[1].append(v.reshape(bs, ls, ATT_HEADS, ATT_V_DIM))
        outs_s[2].append(c_new)
        outs_s[3].append(s_new.reshape(bs, SSM_GROUPS, SSM_HEADS_PER_GROUP, SSM_HEAD_DIM, SSM_STATE))

    return (yp.reshape(bp, lp, D_MODEL), ys.reshape(bs, ls, D_MODEL),
            *(jnp.stack(o) for o in outs_p), *(jnp.stack(o) for o in outs_s))
```

```python
import functools
import math

import jax
import jax.numpy as jnp
from jax import lax
from jax.experimental import pallas as pl
from jax.experimental.pallas import tpu as pltpu

F32 = jnp.float32
BF16 = jnp.bfloat16

D_MODEL = 1024
DEPTH = 2
CHUNK = 64
ATT_HEADS = 8
ATT_HEAD_DIM = 64
ATT_V_DIM = 128
ATT_WIDTH = 1024
Q_DIM = 1024
ATT_SUBLN_EPS = 1e-5
D_INNER = 2048
SSM_HEAD_DIM = 64
SSM_HEADS = 32
SSM_GROUPS = 8
SSM_HEADS_PER_GROUP = 4
SSM_STATE = 128
CONV_WIDTH = 4
CONV_DIM = 4096
SSM_NORM_EPS = 1e-5
D_FF = 2816
RMS_EPS = 1e-6
GROUP_WIDTH = SSM_HEADS_PER_GROUP * SSM_HEAD_DIM

LANES = 128
SUBLANES = 8
VMEM_LIMIT_BYTES = 56 * 1024 * 1024

PROJ_TN = 512
SSD_CHUNK = 128
CONV_PAD = 8


def _cparams(*sem):
    return pltpu.CompilerParams(dimension_semantics=sem, vmem_limit_bytes=VMEM_LIMIT_BYTES)


def _const_spec(shape):
    return pl.BlockSpec(shape, lambda *_: (0,) * len(shape), pipeline_mode=pl.Buffered(1))


def _rms(x, w, eps):
    return x * lax.rsqrt(jnp.mean(x * x, axis=-1, keepdims=True) + eps) * w


def _split3(x):
    hi = x.astype(BF16)
    r1 = x - hi.astype(F32)
    mid = r1.astype(BF16)
    lo = (r1 - mid.astype(F32)).astype(BF16)
    return hi, mid, lo


def _dot(a, b):
    return jnp.dot(a, b, preferred_element_type=F32)


def _dot_nt(a, b):
    return lax.dot_general(a, b, (((1,), (1,)), ((), ())), preferred_element_type=F32)


_NQ = Q_DIM // PROJ_TN
_NK = Q_DIM // PROJ_TN
_NV = ATT_WIDTH // PROJ_TN
_NZ = D_INNER // PROJ_TN
_NX = CONV_DIM // PROJ_TN
_NG = 2 * D_MODEL // PROJ_TN
_OFF_K = _NQ
_OFF_V = _OFF_K + _NK
_OFF_Z = _OFF_V + _NV
_OFF_X = _OFF_Z + _NZ
_OFF_G = _OFF_X + _NX
_N_COL_TILES = _OFF_G + _NG


def _in_proj_kernel(x_ref, nw_ref, w_ref, wdt_ref,
                    q_ref, k_ref, kb_ref, v_ref, vb_ref, z_ref, xbc_ref, g_ref, dt_ref, h_sc):
    j = pl.program_id(1)

    @pl.when(j == 0)
    def _():
        hb = _rms(x_ref[...], nw_ref[...], RMS_EPS).astype(BF16)
        h_sc[...] = hb
        dt_ref[...] = _dot(hb, wdt_ref[...])

    res = _dot(h_sc[...], w_ref[...])

    @pl.when(j < _OFF_K)
    def _():
        q_ref[...] = res.astype(BF16)

    @pl.when((j >= _OFF_K) & (j < _OFF_V))
    def _():
        k_ref[...] = res
        kb_ref[...] = res.astype(BF16)

    @pl.when((j >= _OFF_V) & (j < _OFF_Z))
    def _():
        v_ref[...] = res
        vb_ref[...] = res.astype(BF16)

    @pl.when((j >= _OFF_Z) & (j < _OFF_X))
    def _():
        z_ref[...] = res

    @pl.when((j >= _OFF_X) & (j < _OFF_G))
    def _():
        xbc_ref[...] = res

    @pl.when(j >= _OFF_G)
    def _():
        g_ref[...] = res


def _in_proj(x, norm_w, w_main, w_dt, tm):
    t = x.shape[0]

    def out_spec(off, n):
        return pl.BlockSpec((tm, PROJ_TN), lambda i, j: (i, jnp.clip(j - off, 0, n - 1)))

    out_shape = (
        jax.ShapeDtypeStruct((t, Q_DIM), BF16),
        jax.ShapeDtypeStruct((t, Q_DIM), F32),
        jax.ShapeDtypeStruct((t, Q_DIM), BF16),
        jax.ShapeDtypeStruct((t, ATT_WIDTH), F32),
        jax.ShapeDtypeStruct((t, ATT_WIDTH), BF16),
        jax.ShapeDtypeStruct((t, D_INNER), F32),
        jax.ShapeDtypeStruct((t, CONV_DIM), F32),
        jax.ShapeDtypeStruct((t, 2 * D_MODEL), F32),
        jax.ShapeDtypeStruct((t, LANES), F32),
    )
    out_specs = (
        out_spec(0, _NQ), out_spec(_OFF_K, _NK), out_spec(_OFF_K, _NK),
        out_spec(_OFF_V, _NV), out_spec(_OFF_V, _NV), out_spec(_OFF_Z, _NZ),
        out_spec(_OFF_X, _NX), out_spec(_OFF_G, _NG),
        pl.BlockSpec((tm, LANES), lambda i, j: (i, 0)),
    )
    return pl.pallas_call(
        _in_proj_kernel,
        grid=(t // tm, _N_COL_TILES),
        in_specs=[
            pl.BlockSpec((tm, D_MODEL), lambda i, j: (i, 0)),
            pl.BlockSpec((1, D_MODEL), lambda i, j: (0, 0)),
            pl.BlockSpec((D_MODEL, PROJ_TN), lambda i, j: (0, j)),
            pl.BlockSpec((D_MODEL, LANES), lambda i, j: (0, 0)),
        ],
        out_specs=out_specs,
        out_shape=out_shape,
        scratch_shapes=[pltpu.VMEM((tm, D_MODEL), BF16)],
        compiler_params=_cparams("arbitrary", "arbitrary"),
        name="in_proj",
    )(x, norm_w, w_main, w_dt)


def _stack_q(q):
    lane = lax.broadcasted_iota(jnp.int32, q.shape, 1)
    zero = jnp.zeros_like(q)
    return jnp.concatenate([jnp.where(lane < ATT_HEAD_DIM, q, zero),
                            jnp.where(lane >= ATT_HEAD_DIM, q, zero)], axis=0)


def _diff_epilogue(o, n, lamp_ref, subln_ref, lam_init):
    lp = lamp_ref[...]
    lam = (jnp.exp(jnp.sum(lp[0:1] * lp[1:2], axis=-1, keepdims=True))
           - jnp.exp(jnp.sum(lp[2:3] * lp[3:4], axis=-1, keepdims=True)) + lam_init)
    a = o[:n] - lam * o[n:]
    return _rms(a, subln_ref[...], ATT_SUBLN_EPS) * (1.0 - lam_init)


def _attn_prompt_kernel(lamp_ref, subln_ref, q_ref, k_ref, v_ref, o_ref, m_sc, l_sc, acc_sc, *, tq, lam_init):
    qi = pl.program_id(1)
    qs = _stack_q(q_ref[...])
    m_sc[...] = jnp.full(m_sc.shape, -jnp.inf, F32)
    l_sc[...] = jnp.zeros(l_sc.shape, F32)
    acc_sc[...] = jnp.zeros(acc_sc.shape, F32)
    n_lane_tiles = tq // LANES

    def block(start, masked, state):
        m_prev, l_prev, acc_prev = state
        kb = k_ref[pl.ds(start, tq), :]
        vb = v_ref[pl.ds(start, tq), :]
        s = _dot_nt(qs, kb)
        if masked:
            r = lax.broadcasted_iota(jnp.int32, s.shape, 0)
            c = lax.broadcasted_iota(jnp.int32, s.shape, 1)
            rq = jnp.where(r >= tq, r - tq, r)
            s = jnp.where((c // CHUNK) <= (rq // CHUNK), s, -jnp.inf)
        m_new = jnp.maximum(m_prev, jnp.max(s, axis=-1, keepdims=True))
        alpha = jnp.exp(m_prev - m_new)
        p = jnp.exp(s - pltpu.repeat(m_new, n_lane_tiles, axis=1))
        p_lanes = p[:, 0:LANES]
        for t in range(1, n_lane_tiles):
            p_lanes = p_lanes + p[:, t * LANES:(t + 1) * LANES]
        return m_new, alpha * l_prev + p_lanes, alpha * acc_prev + _dot(p.astype(BF16), vb)

    def blocks(first, count, masked_last):
        state = (m_sc[...], l_sc[...], acc_sc[...])
        for n in range(count):
            start = pl.multiple_of((first + n) * tq, tq)
            state = block(start, masked_last and n == count - 1, state)
        m_sc[...], l_sc[...], acc_sc[...] = state

    def body(kp, carry):
        blocks(2 * kp, 2, False)
        return carry

    lax.fori_loop(0, qi // 2, body, 0)

    @pl.when(qi % 2 == 1)
    def _():
        blocks(qi - 1, 2, True)

    @pl.when(qi % 2 == 0)
    def _():
        blocks(qi, 1, True)

    l = jnp.sum(l_sc[...], axis=-1, keepdims=True)
    o = acc_sc[...] * (1.0 / l)
    o_ref[...] = _diff_epilogue(o, tq, lamp_ref, subln_ref, lam_init).astype(o_ref.dtype)


def _attn_prompt(q, kb, vb, lamp, subln, lam_init, tq):
    t = q.shape[0]
    kern = functools.partial(_attn_prompt_kernel, tq=tq, lam_init=lam_init)
    return pl.pallas_call(
        kern,
        grid=(ATT_HEADS, t // tq),
        in_specs=[
            pl.BlockSpec((4, ATT_HEAD_DIM), lambda h, i: (0, 0)),
            pl.BlockSpec((1, ATT_V_DIM), lambda h, i: (0, 0)),
            pl.BlockSpec((tq, LANES), lambda h, i: (i, h)),
            pl.BlockSpec((t, LANES), lambda h, i: (0, h)),
            pl.BlockSpec((t, LANES), lambda h, i: (0, h)),
        ],
        out_specs=pl.BlockSpec((tq, LANES), lambda h, i: (i, h)),
        out_shape=jax.ShapeDtypeStruct((t, ATT_WIDTH), BF16),
        scratch_shapes=[pltpu.VMEM((2 * tq, LANES), F32), pltpu.VMEM((2 * tq, LANES), F32),
                        pltpu.VMEM((2 * tq, ATT_V_DIM), F32)],
        compiler_params=_cparams("arbitrary", "arbitrary"),
        name="attn_prompt",
    )(lamp, subln, q, kb, vb)


def _attn_sample_kernel(lamp_ref, subln_ref, q_ref, kc_ref, vc_ref, kn_ref, vn_ref, o_ref, *, n, lam_init):
    qs = _stack_q(q_ref[0])
    kc = kc_ref[0, 0].astype(BF16)
    vc = vc_ref[0, 0].astype(BF16)
    s_past = _dot_nt(qs, kc)
    s_new = _dot_nt(qs, kn_ref[0])
    m = jnp.maximum(jnp.max(s_past, axis=-1, keepdims=True), jnp.max(s_new, axis=-1, keepdims=True))
    p_past = jnp.exp(s_past - m)
    p_new = jnp.exp(s_new - m)
    l = jnp.sum(p_past, axis=-1, keepdims=True) + jnp.sum(p_new, axis=-1, keepdims=True)
    acc = _dot(p_past.astype(BF16), vc) + _dot(p_new.astype(BF16), vn_ref[0])
    o = acc * (1.0 / l)
    o_ref[0] = _diff_epilogue(o, n, lamp_ref, subln_ref, lam_init).astype(o_ref.dtype)


def _attn_sample(q, cache_k, cache_v, kb, vb, lamp, subln, lam_init, layer):
    b, n, _ = q.shape
    past = cache_k.shape[2]
    assert past % CHUNK == 0 and n <= CHUNK
    kern = functools.partial(_attn_sample_kernel, n=n, lam_init=lam_init)
    return pl.pallas_call(
        kern,
        grid=(b, ATT_HEADS),
        in_specs=[
            pl.BlockSpec((4, ATT_HEAD_DIM), lambda bi, h: (0, 0)),
            pl.BlockSpec((1, ATT_V_DIM), lambda bi, h: (0, 0)),
            pl.BlockSpec((1, n, LANES), lambda bi, h: (bi, 0, h)),
            pl.BlockSpec((1, 1, past, LANES), lambda bi, h: (layer, bi, 0, h)),
            pl.BlockSpec((1, 1, past, LANES), lambda bi, h: (layer, bi, 0, h)),
            pl.BlockSpec((1, n, LANES), lambda bi, h: (bi, 0, h)),
            pl.BlockSpec((1, n, LANES), lambda bi, h: (bi, 0, h)),
        ],
        out_specs=pl.BlockSpec((1, n, LANES), lambda bi, h: (bi, 0, h)),
        out_shape=jax.ShapeDtypeStruct((b, n, ATT_WIDTH), BF16),
        compiler_params=_cparams("arbitrary", "arbitrary"),
        name="attn_sample",
    )(lamp, subln, q, cache_k, cache_v, kb, vb)


def _ssd_kernel(xbc_ref, z_ref, dt_ref, cbuf_ref, st0_ref, cw_ref, cb_ref, dtb_ref, alog_ref, dsk_ref,
                nrm_ref, expand_ref, y_ref, cout_ref, sout_ref, xpad_sc, st_sc, *, lc, valid_last):
    c = pl.program_id(1)
    nchunks = pl.num_programs(1)

    @pl.when(c == 0)
    def _():
        xpad_sc[0:CONV_PAD, :] = jnp.zeros((CONV_PAD, CONV_DIM), F32)
        xpad_sc[CONV_PAD - (CONV_WIDTH - 1):CONV_PAD, :] = cbuf_ref[0]
        for g in range(SSM_GROUPS):
            st_sc[g] = st0_ref[0, g].T

    xpad_sc[CONV_PAD:CONV_PAD + lc, :] = xbc_ref[0]
    conv = cb_ref[...]
    for j in range(CONV_WIDTH):
        off = CONV_PAD - (CONV_WIDTH - 1) + j
        conv = conv + cw_ref[j:j + 1, :] * xpad_sc[off:off + lc, :]
    act = conv * jax.nn.sigmoid(conv)

    @pl.when(c == nchunks - 1)
    def _():
        end = CONV_PAD + valid_last
        cout_ref[0] = xpad_sc[end - (CONV_WIDTH - 1):end, :]

    xpad_sc[0:CONV_PAD, :] = xpad_sc[lc:lc + CONV_PAD, :]

    dtr = dt_ref[0] + dtb_ref[...]
    dt = jnp.maximum(dtr, 0.0) + jnp.log1p(jnp.exp(-jnp.abs(dtr)))
    if valid_last < lc:
        row = lax.broadcasted_iota(jnp.int32, dt.shape, 0)
        dt = jnp.where(row < valid_last, dt, 0.0)
    a = dt * (-jnp.exp(alog_ref[...]))
    rr = lax.broadcasted_iota(jnp.int32, (lc, lc), 0)
    cc = lax.broadcasted_iota(jnp.int32, (lc, lc), 1)
    causal = rr >= cc
    tri = jnp.where(causal, 1.0, 0.0).astype(BF16)
    a_cs = sum(_dot(tri, part) for part in _split3(a))
    a_last = a_cs[lc - 1:lc, :]
    a_cs_t = a_cs.T

    stack = jnp.concatenate([dt, dt * jnp.exp(a_last - a_cs), jnp.exp(a_cs),
                             jnp.broadcast_to(jnp.exp(a_last), (SUBLANES, LANES))], axis=0)
    expand = expand_ref[...]
    ex = sum(_dot(part, expand) for part in _split3(stack))
    w_dt = ex[0:lc]
    w_state = ex[lc:2 * lc]
    w_off = ex[2 * lc:3 * lc]
    w_carry = ex[3 * lc:3 * lc + 1]

    xs = act[:, :D_INNER]
    xdt = xs * w_dt
    xst = xs * w_state
    lane = lax.broadcasted_iota(jnp.int32, (lc, LANES), 1)
    lo_half = lane < SSM_HEAD_DIM

    for g in range(SSM_GROUPS):
        gs = slice(g * GROUP_WIDTH, (g + 1) * GROUP_WIDTH)
        bg = act[:, D_INNER + g * SSM_STATE:D_INNER + (g + 1) * SSM_STATE]
        cg = act[:, D_INNER + SSM_GROUPS * SSM_STATE + g * SSM_STATE:
                 D_INNER + SSM_GROUPS * SSM_STATE + (g + 1) * SSM_STATE]
        bgb = bg.astype(BF16)
        cgb = cg.astype(BF16)
        cb = _dot_nt(cgb, bgb)
        st_prev = st_sc[g]
        y_off = _dot(cgb, st_prev.astype(BF16)) * w_off[:, gs]
        st_sc[g] = w_carry[:, gs] * st_prev + _dot(bg.T.astype(BF16), xst[:, gs].astype(BF16))

        pairs = []
        for pr in range(SSM_HEADS_PER_GROUP // 2):
            x_pair = xdt[:, g * GROUP_WIDTH + pr * LANES:g * GROUP_WIDTH + (pr + 1) * LANES]
            y_pair = None
            for half in range(2):
                h = g * SSM_HEADS_PER_GROUP + pr * 2 + half
                seg = a_cs[:, h:h + 1] - a_cs_t[h:h + 1, :]
                decay = jnp.exp(jnp.where(causal, seg, -jnp.inf))
                mix = (cb * decay).astype(BF16)
                keep = lo_half if half == 0 else jnp.logical_not(lo_half)
                contrib = _dot(mix, jnp.where(keep, x_pair, 0.0).astype(BF16))
                y_pair = contrib if y_pair is None else y_pair + contrib
            pairs.append(y_pair)
        y_g = jnp.concatenate(pairs, axis=1) + y_off + dsk_ref[:, gs] * xs[:, gs]

        zg = z_ref[0, :, gs]
        yz = y_g * (zg * jax.nn.sigmoid(zg))
        y_ref[0, :, gs] = _rms(yz, nrm_ref[:, gs], SSM_NORM_EPS).astype(y_ref.dtype)

    @pl.when(c == nchunks - 1)
    def _():
        for g in range(SSM_GROUPS):
            sout_ref[0, g] = st_sc[g].T


def _ssd(xbc, z, dt, conv_buf, state0, p, valid_last):
    b, l, _ = xbc.shape
    lc = SSD_CHUNK
    assert l % lc == 0 and CONV_WIDTH - 1 <= valid_last <= lc
    assert valid_last == lc or l == lc
    kern = functools.partial(_ssd_kernel, lc=lc, valid_last=valid_last)
    return pl.pallas_call(
        kern,
        grid=(b, l // lc),
        in_specs=[
            pl.BlockSpec((1, lc, CONV_DIM), lambda bi, c: (bi, c, 0)),
            pl.BlockSpec((1, lc, D_INNER), lambda bi, c: (bi, c, 0)),
            pl.BlockSpec((1, lc, LANES), lambda bi, c: (bi, c, 0)),
            pl.BlockSpec((1, CONV_WIDTH - 1, CONV_DIM), lambda bi, c: (bi, 0, 0)),
            pl.BlockSpec((1, SSM_GROUPS, GROUP_WIDTH, SSM_STATE), lambda bi, c: (bi, 0, 0, 0)),
            pl.BlockSpec((CONV_WIDTH, CONV_DIM), lambda bi, c: (0, 0)),
            pl.BlockSpec((1, CONV_DIM), lambda bi, c: (0, 0)),
            pl.BlockSpec((1, LANES), lambda bi, c: (0, 0)),
            pl.BlockSpec((1, LANES), lambda bi, c: (0, 0)),
            pl.BlockSpec((1, D_INNER), lambda bi, c: (0, 0)),
            pl.BlockSpec((1, D_INNER), lambda bi, c: (0, 0)),
            pl.BlockSpec((LANES, D_INNER), lambda bi, c: (0, 0)),
        ],
        out_specs=(
            pl.BlockSpec((1, lc, D_INNER), lambda bi, c: (bi, c, 0)),
            pl.BlockSpec((1, CONV_WIDTH - 1, CONV_DIM), lambda bi, c: (bi, 0, 0)),
            pl.BlockSpec((1, SSM_GROUPS, GROUP_WIDTH, SSM_STATE), lambda bi, c: (bi, 0, 0, 0)),
        ),
        out_shape=(
            jax.ShapeDtypeStruct((b, l, D_INNER), BF16),
            jax.ShapeDtypeStruct((b, CONV_WIDTH - 1, CONV_DIM), F32),
            jax.ShapeDtypeStruct((b, SSM_GROUPS, GROUP_WIDTH, SSM_STATE), F32),
        ),
        scratch_shapes=[pltpu.VMEM((lc + CONV_PAD, CONV_DIM), F32),
                        pltpu.VMEM((SSM_GROUPS, SSM_STATE, GROUP_WIDTH), F32)],
        compiler_params=_cparams("arbitrary", "arbitrary"),
        name="ssd",
    )(xbc, z, dt, conv_buf, state0, p["conv_w"], p["conv_b"], p["dt_bias"], p["a_log"], p["d_skip"],
      p["ssm_norm"], p["expand"])


def _merge_kernel(x_ref, att_ref, y_ref, g_ref, wa_ref, ws_ref, wo_ref, o_ref):
    ba = _dot(att_ref[...], wa_ref[...])
    bs = _dot(y_ref[...], ws_ref[...])
    g = jax.nn.sigmoid(g_ref[...])
    merged = g[:, :D_MODEL] * ba + g[:, D_MODEL:] * bs
    o_ref[...] = x_ref[...] + _dot(merged.astype(BF16), wo_ref[...])


def _merge(x, att, y, gates, wa, ws, wo, tm):
    t = x.shape[0]
    return pl.pallas_call(
        _merge_kernel,
        grid=(t // tm,),
        in_specs=[
            pl.BlockSpec((tm, D_MODEL), lambda i: (i, 0)),
            pl.BlockSpec((tm, ATT_WIDTH), lambda i: (i, 0)),
            pl.BlockSpec((tm, D_INNER), lambda i: (i, 0)),
            pl.BlockSpec((tm, 2 * D_MODEL), lambda i: (i, 0)),
            _const_spec((ATT_WIDTH, D_MODEL)),
            _const_spec((D_INNER, D_MODEL)),
            _const_spec((D_MODEL, D_MODEL)),
        ],
        out_specs=pl.BlockSpec((tm, D_MODEL), lambda i: (i, 0)),
        out_shape=jax.ShapeDtypeStruct((t, D_MODEL), F32),
        compiler_params=_cparams("arbitrary"),
        name="merge",
    )(x, att, y, gates, wa, ws, wo)


FFN_SPLIT = 2
FFN_TF = D_FF // FFN_SPLIT


def _ffn_kernel(x_ref, nw_ref, wgu_ref, wd_ref, nf_ref, *out_refs, final_norm):
    x = x_ref[...]
    hb = _rms(x, nw_ref[...], RMS_EPS).astype(BF16)
    acc = x
    for c in range(FFN_SPLIT):
        gt = _dot(hb, wgu_ref[:, c * FFN_TF:(c + 1) * FFN_TF])
        up = _dot(hb, wgu_ref[:, D_FF + c * FFN_TF:D_FF + (c + 1) * FFN_TF])
        act = (gt * jax.nn.sigmoid(gt) * up).astype(BF16)
        acc = acc + _dot(act, wd_ref[c * FFN_TF:(c + 1) * FFN_TF, :])
    out_refs[0][...] = acc
    if final_norm:
        out_refs[1][...] = _rms(acc, nf_ref[...], RMS_EPS)


def _ffn(x, norm_w, wgu, wd, norm_final, final_norm, tm):
    t = x.shape[0]
    row_spec = pl.BlockSpec((tm, D_MODEL), lambda i: (i, 0))
    n_out = 2 if final_norm else 1
    return pl.pallas_call(
        functools.partial(_ffn_kernel, final_norm=final_norm),
        grid=(t // tm,),
        in_specs=[
            row_spec,
            pl.BlockSpec((1, D_MODEL), lambda i: (0, 0)),
            _const_spec((D_MODEL, 2 * D_FF)),
            _const_spec((D_FF, D_MODEL)),
            pl.BlockSpec((1, D_MODEL), lambda i: (0, 0)),
        ],
        out_specs=(row_spec,) * n_out,
        out_shape=(jax.ShapeDtypeStruct((t, D_MODEL), F32),) * n_out,
        compiler_params=_cparams("arbitrary"),
        name="ffn",
    )(x, norm_w, wgu, wd, norm_final)


def _layer_params(l, norm_mix, w_in, lambda_q1, lambda_k1, lambda_q2, lambda_k2, attn_subln, conv_w, conv_b,
                  dt_bias, a_log, d_skip, ssm_norm, w_branch_att, w_branch_ssd, w_out, norm_ffn, w_gate_up,
                  w_down):
    w = w_in[l]
    n_main = Q_DIM + Q_DIM + ATT_WIDTH + D_INNER + CONV_DIM
    scale = ATT_HEAD_DIM ** -0.5
    w_main = jnp.concatenate([(w[:, :Q_DIM] * scale).astype(BF16), w[:, Q_DIM:n_main].astype(BF16),
                              w[:, n_main + SSM_HEADS:].astype(BF16)], axis=1)
    w_dt = jnp.pad(w[:, n_main:n_main + SSM_HEADS], ((0, 0), (0, LANES - SSM_HEADS))).astype(BF16)
    head_of_channel = jnp.arange(D_INNER) // SSM_HEAD_DIM
    return {
        "norm_mix": norm_mix[l][None], "w_main": w_main, "w_dt": w_dt,
        "lamp": jnp.stack([lambda_q1[l], lambda_k1[l], lambda_q2[l], lambda_k2[l]]),
        "subln": attn_subln[l][None],
        "conv_w": conv_w[l], "conv_b": conv_b[l][None],
        "dt_bias": jnp.pad(dt_bias[l], (0, LANES - SSM_HEADS))[None],
        "a_log": jnp.pad(a_log[l], (0, LANES - SSM_HEADS))[None],
        "d_skip": jnp.repeat(d_skip[l], SSM_HEAD_DIM)[None],
        "ssm_norm": ssm_norm[l][None],
        "expand": (jnp.arange(LANES)[:, None] == head_of_channel[None, :]).astype(BF16),
        "wa": w_branch_att[l].astype(BF16), "ws": w_branch_ssd[l].astype(BF16), "wo": w_out[l].astype(BF16),
        "norm_ffn": norm_ffn[l][None], "wgu": w_gate_up[l].astype(BF16), "wd": w_down[l].astype(BF16),
    }


def _pad_time(x, b, n, lc):
    return jnp.pad(x.reshape(b, n, x.shape[-1]), ((0, 0), (0, lc - n), (0, 0)))


def kernel(x_prompt, x_sample, cache_k, cache_v, state_conv, state_ssm, norm_mix, w_in, lambda_q1, lambda_k1,
           lambda_q2, lambda_k2, attn_subln, conv_w, conv_b, dt_bias, a_log, d_skip, ssm_norm, w_branch_att,
           w_branch_ssd, w_out, norm_ffn, w_gate_up, w_down, norm_final):
    bp, lp, _ = x_prompt.shape
    bs, ls, _ = x_sample.shape
    assert bp == 1
    past = cache_k.shape[2]
    tp, ts = bp * lp, bs * ls
    xp = x_prompt.reshape(tp, D_MODEL)
    xs = x_sample.reshape(ts, D_MODEL)
    cache_k = cache_k.reshape(DEPTH, bs, past, Q_DIM)
    cache_v = cache_v.reshape(DEPTH, bs, past, ATT_WIDTH)
    state_ssm = state_ssm.reshape(DEPTH, bs, SSM_GROUPS, GROUP_WIDTH, SSM_STATE)
    conv0 = jnp.zeros((bp, CONV_WIDTH - 1, CONV_DIM), F32)
    ssm0 = jnp.zeros((bp, SSM_GROUPS, GROUP_WIDTH, SSM_STATE), F32)
    nf = norm_final[None]

    tm_p = 1024 if tp % 1024 == 0 else 128
    tm_row = 512 if tp % 512 == 0 else 128
    tq = 512 if lp % 512 == 0 else CHUNK

    outs_p = [[] for _ in range(4)]
    outs_s = [[] for _ in range(4)]
    yp = ys = None
    for l in range(DEPTH):
        p = _layer_params(l, norm_mix, w_in, lambda_q1, lambda_k1, lambda_q2, lambda_k2, attn_subln, conv_w,
                          conv_b, dt_bias, a_log, d_skip, ssm_norm, w_branch_att, w_branch_ssd, w_out,
                          norm_ffn, w_gate_up, w_down)
        lam_init = 0.8 - 0.6 * math.exp(-0.3 * l)
        last = l == DEPTH - 1

        q, k, kb, v, vb, z, xbc, gates, dt = _in_proj(xp, p["norm_mix"], p["w_main"], p["w_dt"], tm_p)
        att = _attn_prompt(q, kb, vb, p["lamp"], p["subln"], lam_init, tq)
        y, c_new, s_new = _ssd(xbc.reshape(bp, lp, CONV_DIM), z.reshape(bp, lp, D_INNER),
                               dt.reshape(bp, lp, LANES), conv0, ssm0, p, SSD_CHUNK)
        x1 = _merge(xp, att, y.reshape(tp, D_INNER), gates, p["wa"], p["ws"], p["wo"], tm_row)
        res = _ffn(x1, p["norm_ffn"], p["wgu"], p["wd"], nf, last, tm_row)
        xp = res[0]
        if last:
            yp = res[1]
        outs_p[0].append(k.reshape(bp, lp, ATT_HEADS, 2, ATT_HEAD_DIM))
        outs_p[1].append(v.reshape(bp, lp, ATT_HEADS, ATT_V_DIM))
        outs_p[2].append(c_new)
        outs_p[3].append(s_new.reshape(bp, SSM_GROUPS, SSM_HEADS_PER_GROUP, SSM_HEAD_DIM, SSM_STATE))

        q, k, kb, v, vb, z, xbc, gates, dt = _in_proj(xs, p["norm_mix"], p["w_main"], p["w_dt"], ts)
        att = _attn_sample(q.reshape(bs, ls, Q_DIM), cache_k, cache_v, kb.reshape(bs, ls, Q_DIM),
                           vb.reshape(bs, ls, ATT_WIDTH), p["lamp"], p["subln"], lam_init, l)
        y, c_new, s_new = _ssd(_pad_time(xbc, bs, ls, SSD_CHUNK), _pad_time(z, bs, ls, SSD_CHUNK),
                               _pad_time(dt, bs, ls, SSD_CHUNK), state_conv[l], state_ssm[l], p, ls)
        x1 = _merge(xs, att.reshape(ts, ATT_WIDTH), y[:, :ls].reshape(ts, D_INNER), gates,
                    p["wa"], p["ws"], p["wo"], ts)
        res = _ffn(x1, p["norm_ffn"], p["wgu"], p["wd"], nf, last, ts)
        xs = res[0]
        if last:
            ys = res[1]
        outs_s[0].append(k.reshape(bs, ls, ATT_HEADS, 2, ATT_HEAD_DIM))
        outs_s[1].append(v.reshape(bs, ls, ATT_HEADS, ATT_V_DIM))
        outs_s[2].append(c_new)
        outs_s[3].append(s_new.reshape(bs, SSM_GROUPS, SSM_HEADS_PER_GROUP, SSM_HEAD_DIM, SSM_STATE))

    return (yp.reshape(bp, lp, D_MODEL), ys.reshape(bs, ls, D_MODEL),
            *(jnp.stack(o) for o in outs_p), *(jnp.stack(o) for o in outs_s))
```

```python
import functools
import math

import jax
import jax.numpy as jnp
from jax import lax
from jax.experimental import pallas as pl
from jax.experimental.pallas import tpu as pltpu

F32 = jnp.float32
BF16 = jnp.bfloat16

D_MODEL = 1024
DEPTH = 2
CHUNK = 64
ATT_HEADS = 8
ATT_HEAD_DIM = 64
ATT_V_DIM = 128
ATT_WIDTH = 1024
Q_DIM = 1024
ATT_SUBLN_EPS = 1e-5
D_INNER = 2048
SSM_HEAD_DIM = 64
SSM_HEADS = 32
SSM_GROUPS = 8
SSM_HEADS_PER_GROUP = 4
SSM_STATE = 128
CONV_WIDTH = 4
CONV_DIM = 4096
SSM_NORM_EPS = 1e-5
D_FF = 2816
RMS_EPS = 1e-6
GROUP_WIDTH = SSM_HEADS_PER_GROUP * SSM_HEAD_DIM

LANES = 128
SUBLANES = 8
VMEM_LIMIT_BYTES = 56 * 1024 * 1024

PROJ_TN = 512
SSD_CHUNK = 256
SSD_PAD_ROWS = 128
CONV_PAD = 8
ATTN_UNROLL = 4


def _cparams(*sem):
    return pltpu.CompilerParams(dimension_semantics=sem, vmem_limit_bytes=VMEM_LIMIT_BYTES)


def _const_spec(shape):
    return pl.BlockSpec(shape, lambda *_: (0,) * len(shape), pipeline_mode=pl.Buffered(1))


def _rms(x, w, eps):
    return x * lax.rsqrt(jnp.mean(x * x, axis=-1, keepdims=True) + eps) * w


def _split3(x):
    hi = x.astype(BF16)
    r1 = x - hi.astype(F32)
    mid = r1.astype(BF16)
    lo = (r1 - mid.astype(F32)).astype(BF16)
    return hi, mid, lo


def _dot(a, b):
    return jnp.dot(a, b, preferred_element_type=F32)


def _dot_nt(a, b):
    return lax.dot_general(a, b, (((1,), (1,)), ((), ())), preferred_element_type=F32)


_NQ = Q_DIM // PROJ_TN
_NK = Q_DIM // PROJ_TN
_NV = ATT_WIDTH // PROJ_TN
_NZ = D_INNER // PROJ_TN
_NX = CONV_DIM // PROJ_TN
_NG = 2 * D_MODEL // PROJ_TN
_OFF_K = _NQ
_OFF_V = _OFF_K + _NK
_OFF_Z = _OFF_V + _NV
_OFF_X = _OFF_Z + _NZ
_OFF_G = _OFF_X + _NX
_N_COL_TILES = _OFF_G + _NG


def _in_proj_kernel(x_ref, nw_ref, w_ref, wdt_ref,
                    q_ref, k_ref, kb_ref, v_ref, vb_ref, z_ref, xbc_ref, g_ref, dt_ref, h_sc):
    j = pl.program_id(1)

    @pl.when(j == 0)
    def _():
        hb = _rms(x_ref[...], nw_ref[...], RMS_EPS).astype(BF16)
        h_sc[...] = hb
        dt_ref[...] = _dot(hb, wdt_ref[...])

    def tile():
        return _dot(h_sc[...], w_ref[...])

    @pl.when(j < _OFF_K)
    def _():
        q_ref[...] = tile().astype(BF16)

    @pl.when((j >= _OFF_K) & (j < _OFF_V))
    def _():
        res = tile()
        k_ref[...] = res
        kb_ref[...] = res.astype(BF16)

    @pl.when((j >= _OFF_V) & (j < _OFF_Z))
    def _():
        res = tile()
        v_ref[...] = res
        vb_ref[...] = res.astype(BF16)

    @pl.when((j >= _OFF_Z) & (j < _OFF_X))
    def _():
        z_ref[...] = tile()

    @pl.when((j >= _OFF_X) & (j < _OFF_G))
    def _():
        xbc_ref[...] = tile()

    @pl.when(j >= _OFF_G)
    def _():
        g_ref[...] = tile()


def _in_proj(x, norm_w, w_main, w_dt, tm):
    t = x.shape[0]

    def out_spec(off, n):
        return pl.BlockSpec((tm, PROJ_TN), lambda i, j: (i, jnp.clip(j - off, 0, n - 1)))

    out_shape = (
        jax.ShapeDtypeStruct((t, Q_DIM), BF16),
        jax.ShapeDtypeStruct((t, Q_DIM), F32),
        jax.ShapeDtypeStruct((t, Q_DIM), BF16),
        jax.ShapeDtypeStruct((t, ATT_WIDTH), F32),
        jax.ShapeDtypeStruct((t, ATT_WIDTH), BF16),
        jax.ShapeDtypeStruct((t, D_INNER), F32),
        jax.ShapeDtypeStruct((t, CONV_DIM), F32),
        jax.ShapeDtypeStruct((t, 2 * D_MODEL), F32),
        jax.ShapeDtypeStruct((t, LANES), F32),
    )
    out_specs = (
        out_spec(0, _NQ), out_spec(_OFF_K, _NK), out_spec(_OFF_K, _NK),
        out_spec(_OFF_V, _NV), out_spec(_OFF_V, _NV), out_spec(_OFF_Z, _NZ),
        out_spec(_OFF_X, _NX), out_spec(_OFF_G, _NG),
        pl.BlockSpec((tm, LANES), lambda i, j: (i, 0)),
    )
    return pl.pallas_call(
        _in_proj_kernel,
        grid=(t // tm, _N_COL_TILES),
        in_specs=[
            pl.BlockSpec((tm, D_MODEL), lambda i, j: (i, 0)),
            pl.BlockSpec((1, D_MODEL), lambda i, j: (0, 0)),
            pl.BlockSpec((D_MODEL, PROJ_TN), lambda i, j: (0, j)),
            pl.BlockSpec((D_MODEL, LANES), lambda i, j: (0, 0)),
        ],
        out_specs=out_specs,
        out_shape=out_shape,
        scratch_shapes=[pltpu.VMEM((tm, D_MODEL), BF16)],
        compiler_params=_cparams("arbitrary", "arbitrary"),
        name="in_proj",
    )(x, norm_w, w_main, w_dt)


def _stack_q(q):
    lane = lax.broadcasted_iota(jnp.int32, q.shape, 1)
    zero = jnp.zeros_like(q)
    return jnp.concatenate([jnp.where(lane < ATT_HEAD_DIM, q, zero),
                            jnp.where(lane >= ATT_HEAD_DIM, q, zero)], axis=0)


def _diff_epilogue(o, n, lamp_ref, subln_ref, lam_init):
    lp = lamp_ref[...]
    lam = (jnp.exp(jnp.sum(lp[0:1] * lp[1:2], axis=-1, keepdims=True))
           - jnp.exp(jnp.sum(lp[2:3] * lp[3:4], axis=-1, keepdims=True)) + lam_init)
    a = o[:n] - lam * o[n:]
    return _rms(a, subln_ref[...], ATT_SUBLN_EPS) * (1.0 - lam_init)


def _attn_prompt_kernel(lamp_ref, subln_ref, q_ref, k_ref, v_ref, o_ref, m_sc, l_sc, acc_sc, *, tq, lam_init):
    qi = pl.program_id(1)
    qs = _stack_q(q_ref[...])
    m_sc[...] = jnp.full(m_sc.shape, -jnp.inf, F32)
    l_sc[...] = jnp.zeros(l_sc.shape, F32)
    acc_sc[...] = jnp.zeros(acc_sc.shape, F32)
    n_lane_tiles = tq // LANES

    def block(start, masked, state):
        m_prev, l_prev, acc_prev = state
        kb = k_ref[pl.ds(start, tq), :]
        vb = v_ref[pl.ds(start, tq), :]
        s = _dot_nt(qs, kb)
        if masked:
            r = lax.broadcasted_iota(jnp.int32, s.shape, 0)
            c = lax.broadcasted_iota(jnp.int32, s.shape, 1)
            rq = jnp.where(r >= tq, r - tq, r)
            s = jnp.where((c // CHUNK) <= (rq // CHUNK), s, -jnp.inf)
        m_new = jnp.maximum(m_prev, jnp.max(s, axis=-1, keepdims=True))
        alpha = jnp.exp(m_prev - m_new)
        p = jnp.exp(s - jnp.concatenate([m_new] * n_lane_tiles, axis=1))
        p_lanes = p[:, 0:LANES]
        for t in range(1, n_lane_tiles):
            p_lanes = p_lanes + p[:, t * LANES:(t + 1) * LANES]
        return m_new, alpha * l_prev + p_lanes, alpha * acc_prev + _dot(p.astype(BF16), vb)

    def blocks(first, count, masked_last):
        state = (m_sc[...], l_sc[...], acc_sc[...])
        for n in range(count):
            start = pl.multiple_of((first + n) * tq, tq)
            state = block(start, masked_last and n == count - 1, state)
        m_sc[...], l_sc[...], acc_sc[...] = state

    def body(kp, carry):
        blocks(ATTN_UNROLL * kp, ATTN_UNROLL, False)
        return carry

    lax.fori_loop(0, qi // ATTN_UNROLL, body, 0)
    rest = qi % ATTN_UNROLL
    for r in range(ATTN_UNROLL):
        @pl.when(rest == r)
        def _():
            blocks(qi - r, r + 1, True)

    l = jnp.sum(l_sc[...], axis=-1, keepdims=True)
    o = acc_sc[...] * (1.0 / l)
    o_ref[...] = _diff_epilogue(o, tq, lamp_ref, subln_ref, lam_init).astype(o_ref.dtype)


def _attn_prompt(q, kb, vb, lamp, subln, lam_init, tq):
    t = q.shape[0]
    kern = functools.partial(_attn_prompt_kernel, tq=tq, lam_init=lam_init)
    return pl.pallas_call(
        kern,
        grid=(ATT_HEADS, t // tq),
        in_specs=[
            pl.BlockSpec((4, ATT_HEAD_DIM), lambda h, i: (0, 0)),
            pl.BlockSpec((1, ATT_V_DIM), lambda h, i: (0, 0)),
            pl.BlockSpec((tq, LANES), lambda h, i: (i, h)),
            pl.BlockSpec((t, LANES), lambda h, i: (0, h)),
            pl.BlockSpec((t, LANES), lambda h, i: (0, h)),
        ],
        out_specs=pl.BlockSpec((tq, LANES), lambda h, i: (i, h)),
        out_shape=jax.ShapeDtypeStruct((t, ATT_WIDTH), BF16),
        scratch_shapes=[pltpu.VMEM((2 * tq, LANES), F32), pltpu.VMEM((2 * tq, LANES), F32),
                        pltpu.VMEM((2 * tq, ATT_V_DIM), F32)],
        compiler_params=_cparams("arbitrary", "arbitrary"),
        name="attn_prompt",
    )(lamp, subln, q, kb, vb)


def _attn_sample_kernel(lamp_ref, subln_ref, q_ref, kc_ref, vc_ref, kn_ref, vn_ref, o_ref, *, n, lam_init):
    qs = _stack_q(q_ref[0])
    kc = kc_ref[0, 0]
    vc = vc_ref[0, 0]
    s_past = _dot_nt(qs, kc)
    s_new = _dot_nt(qs, kn_ref[0])
    m = jnp.maximum(jnp.max(s_past, axis=-1, keepdims=True), jnp.max(s_new, axis=-1, keepdims=True))
    p_past = jnp.exp(s_past - m)
    p_new = jnp.exp(s_new - m)
    l = jnp.sum(p_past, axis=-1, keepdims=True) + jnp.sum(p_new, axis=-1, keepdims=True)
    acc = _dot(p_past.astype(BF16), vc) + _dot(p_new.astype(BF16), vn_ref[0])
    o = acc * (1.0 / l)
    o_ref[0] = _diff_epilogue(o, n, lamp_ref, subln_ref, lam_init).astype(o_ref.dtype)


def _attn_sample(q, cache_k, cache_v, kb, vb, lamp, subln, lam_init, layer):
    b, n, _ = q.shape
    past = cache_k.shape[2]
    assert past % CHUNK == 0 and n <= CHUNK
    kern = functools.partial(_attn_sample_kernel, n=n, lam_init=lam_init)
    return pl.pallas_call(
        kern,
        grid=(b, ATT_HEADS),
        in_specs=[
            pl.BlockSpec((4, ATT_HEAD_DIM), lambda bi, h: (0, 0)),
            pl.BlockSpec((1, ATT_V_DIM), lambda bi, h: (0, 0)),
            pl.BlockSpec((1, n, LANES), lambda bi, h: (bi, 0, h)),
            pl.BlockSpec((1, 1, past, LANES), lambda bi, h: (layer, bi, 0, h)),
            pl.BlockSpec((1, 1, past, LANES), lambda bi, h: (layer, bi, 0, h)),
            pl.BlockSpec((1, n, LANES), lambda bi, h: (bi, 0, h)),
            pl.BlockSpec((1, n, LANES), lambda bi, h: (bi, 0, h)),
        ],
        out_specs=pl.BlockSpec((1, n, LANES), lambda bi, h: (bi, 0, h)),
        out_shape=jax.ShapeDtypeStruct((b, n, ATT_WIDTH), BF16),
        compiler_params=_cparams("arbitrary", "arbitrary"),
        name="attn_sample",
    )(lamp, subln, q, cache_k, cache_v, kb, vb)


def _ssd_kernel(xbc_ref, z_ref, dt_ref, cbuf_ref, st0_ref, cw_ref, cb_ref, dtb_ref, alog_ref, dsk_ref,
                nrm_ref, expand_ref, y_ref, cout_ref, sout_ref, xpad_sc, st_sc, *, lc, valid_last):
    c = pl.program_id(1)
    nchunks = pl.num_programs(1)

    @pl.when(c == 0)
    def _():
        xpad_sc[0:CONV_PAD, :] = jnp.zeros((CONV_PAD, CONV_DIM), F32)
        xpad_sc[CONV_PAD - (CONV_WIDTH - 1):CONV_PAD, :] = cbuf_ref[0]
        for g in range(SSM_GROUPS):
            st_sc[g] = st0_ref[0, g].T

    xpad_sc[CONV_PAD:CONV_PAD + lc, :] = xbc_ref[0]
    xp = xpad_sc[...]
    conv = cw_ref[0:1, :] * xp
    for j in range(1, CONV_WIDTH):
        conv = cw_ref[j:j + 1, :] * xp + pltpu.roll(conv, 1, 0)
    conv = conv[CONV_PAD:CONV_PAD + lc] + cb_ref[...]
    act = conv * jax.nn.sigmoid(conv)

    @pl.when(c == nchunks - 1)
    def _():
        end = CONV_PAD + valid_last
        cout_ref[0] = xpad_sc[end - (CONV_WIDTH - 1):end, :]

    xpad_sc[0:CONV_PAD, :] = xpad_sc[lc:lc + CONV_PAD, :]

    dtr = dt_ref[0] + dtb_ref[...]
    dt = jnp.maximum(dtr, 0.0) + jnp.log1p(jnp.exp(-jnp.abs(dtr)))
    if valid_last < lc:
        row = lax.broadcasted_iota(jnp.int32, dt.shape, 0)
        dt = jnp.where(row < valid_last, dt, 0.0)
    a = dt * (-jnp.exp(alog_ref[...]))
    rr = lax.broadcasted_iota(jnp.int32, (lc, lc), 0)
    cc = lax.broadcasted_iota(jnp.int32, (lc, lc), 1)
    causal = rr >= cc
    tri = jnp.where(causal, 1.0, 0.0).astype(BF16)
    a_cs = sum(_dot(tri, part) for part in _split3(a))
    a_last = a_cs[lc - 1:lc, :]
    a_cs_t = a_cs.T

    expand = expand_ref[...]
    stack = jnp.concatenate([dt, dt * jnp.exp(a_last - a_cs), jnp.exp(a_cs)], axis=0)
    ex = _dot(stack.astype(BF16), expand)
    w_dt = ex[0:lc]
    w_state = ex[lc:2 * lc]
    w_off = ex[2 * lc:3 * lc]
    carry = jnp.broadcast_to(jnp.exp(a_last), (2 * SUBLANES, LANES))
    w_carry = sum(_dot(part, expand) for part in _split3(carry))[0:1]

    xs = act[:, :D_INNER]
    xdt = xs * w_dt
    xst = xs * w_state
    lane = lax.broadcasted_iota(jnp.int32, (lc, LANES), 1)
    lo_half = lane < SSM_HEAD_DIM

    for g in range(SSM_GROUPS):
        gs = slice(g * GROUP_WIDTH, (g + 1) * GROUP_WIDTH)
        bg = act[:, D_INNER + g * SSM_STATE:D_INNER + (g + 1) * SSM_STATE]
        cg = act[:, D_INNER + SSM_GROUPS * SSM_STATE + g * SSM_STATE:
                 D_INNER + SSM_GROUPS * SSM_STATE + (g + 1) * SSM_STATE]
        bgb = bg.astype(BF16)
        cgb = cg.astype(BF16)
        cb = _dot_nt(cgb, bgb)
        st_prev = st_sc[g]
        y_off = _dot(cgb, st_prev.astype(BF16)) * w_off[:, gs]
        st_sc[g] = w_carry[:, gs] * st_prev + _dot(bg.T.astype(BF16), xst[:, gs].astype(BF16))

        pairs = []
        for pr in range(SSM_HEADS_PER_GROUP // 2):
            x_pair = xdt[:, g * GROUP_WIDTH + pr * LANES:g * GROUP_WIDTH + (pr + 1) * LANES]
            y_pair = None
            for half in range(2):
                h = g * SSM_HEADS_PER_GROUP + pr * 2 + half
                seg = a_cs[:, h:h + 1] - a_cs_t[h:h + 1, :]
                decay = jnp.exp(jnp.where(causal, seg, -jnp.inf))
                mix = (cb * decay).astype(BF16)
                keep = lo_half if half == 0 else jnp.logical_not(lo_half)
                contrib = _dot(mix, jnp.where(keep, x_pair, 0.0).astype(BF16))
                y_pair = contrib if y_pair is None else y_pair + contrib
            pairs.append(y_pair)
        y_g = jnp.concatenate(pairs, axis=1) + y_off + dsk_ref[:, gs] * xs[:, gs]

        zg = z_ref[0, :, gs]
        yz = y_g * (zg * jax.nn.sigmoid(zg))
        y_ref[0, :, gs] = _rms(yz, nrm_ref[:, gs], SSM_NORM_EPS).astype(y_ref.dtype)

    @pl.when(c == nchunks - 1)
    def _():
        for g in range(SSM_GROUPS):
            sout_ref[0, g] = st_sc[g].T


def _ssd(xbc, z, dt, conv_buf, state0, p, lc, valid_last):
    b, l, _ = xbc.shape
    assert l % lc == 0 and CONV_WIDTH - 1 <= valid_last <= lc
    assert valid_last == lc or l == lc
    kern = functools.partial(_ssd_kernel, lc=lc, valid_last=valid_last)
    return pl.pallas_call(
        kern,
        grid=(b, l // lc),
        in_specs=[
            pl.BlockSpec((1, lc, CONV_DIM), lambda bi, c: (bi, c, 0)),
            pl.BlockSpec((1, lc, D_INNER), lambda bi, c: (bi, c, 0)),
            pl.BlockSpec((1, lc, LANES), lambda bi, c: (bi, c, 0)),
            pl.BlockSpec((1, CONV_WIDTH - 1, CONV_DIM), lambda bi, c: (bi, 0, 0)),
            pl.BlockSpec((1, SSM_GROUPS, GROUP_WIDTH, SSM_STATE), lambda bi, c: (bi, 0, 0, 0)),
            pl.BlockSpec((CONV_WIDTH, CONV_DIM), lambda bi, c: (0, 0)),
            pl.BlockSpec((1, CONV_DIM), lambda bi, c: (0, 0)),
            pl.BlockSpec((1, LANES), lambda bi, c: (0, 0)),
            pl.BlockSpec((1, LANES), lambda bi, c: (0, 0)),
            pl.BlockSpec((1, D_INNER), lambda bi, c: (0, 0)),
            pl.BlockSpec((1, D_INNER), lambda bi, c: (0, 0)),
            pl.BlockSpec((LANES, D_INNER), lambda bi, c: (0, 0)),
        ],
        out_specs=(
            pl.BlockSpec((1, lc, D_INNER), lambda bi, c: (bi, c, 0)),
            pl.BlockSpec((1, CONV_WIDTH - 1, CONV_DIM), lambda bi, c: (bi, 0, 0)),
            pl.BlockSpec((1, SSM_GROUPS, GROUP_WIDTH, SSM_STATE), lambda bi, c: (bi, 0, 0, 0)),
        ),
        out_shape=(
            jax.ShapeDtypeStruct((b, l, D_INNER), BF16),
            jax.ShapeDtypeStruct((b, CONV_WIDTH - 1, CONV_DIM), F32),
            jax.ShapeDtypeStruct((b, SSM_GROUPS, GROUP_WIDTH, SSM_STATE), F32),
        ),
        scratch_shapes=[pltpu.VMEM((lc + CONV_PAD, CONV_DIM), F32),
                        pltpu.VMEM((SSM_GROUPS, SSM_STATE, GROUP_WIDTH), F32)],
        compiler_params=_cparams("arbitrary", "arbitrary"),
        name="ssd",
    )(xbc, z, dt, conv_buf, state0, p["conv_w"], p["conv_b"], p["dt_bias"], p["a_log"], p["d_skip"],
      p["ssm_norm"], p["expand"])


def _merge_kernel(x_ref, att_ref, y_ref, g_ref, wa_ref, ws_ref, wo_ref, o_ref):
    ba = _dot(att_ref[...], wa_ref[...])
    bs = _dot(y_ref[...], ws_ref[...])
    g = jax.nn.sigmoid(g_ref[...])
    merged = g[:, :D_MODEL] * ba + g[:, D_MODEL:] * bs
    o_ref[...] = x_ref[...] + _dot(merged.astype(BF16), wo_ref[...])


def _merge(x, att, y, gates, wa, ws, wo, tm):
    t = x.shape[0]
    return pl.pallas_call(
        _merge_kernel,
        grid=(t // tm,),
        in_specs=[
            pl.BlockSpec((tm, D_MODEL), lambda i: (i, 0)),
            pl.BlockSpec((tm, ATT_WIDTH), lambda i: (i, 0)),
            pl.BlockSpec((tm, D_INNER), lambda i: (i, 0)),
            pl.BlockSpec((tm, 2 * D_MODEL), lambda i: (i, 0)),
            _const_spec((ATT_WIDTH, D_MODEL)),
            _const_spec((D_INNER, D_MODEL)),
            _const_spec((D_MODEL, D_MODEL)),
        ],
        out_specs=pl.BlockSpec((tm, D_MODEL), lambda i: (i, 0)),
        out_shape=jax.ShapeDtypeStruct((t, D_MODEL), F32),
        compiler_params=_cparams("arbitrary"),
        name="merge",
    )(x, att, y, gates, wa, ws, wo)


FFN_SPLIT = 2
FFN_TF = D_FF // FFN_SPLIT


def _ffn_kernel(x_ref, nw_ref, wgu_ref, wd_ref, nf_ref, *out_refs, final_norm):
    x = x_ref[...]
    hb = _rms(x, nw_ref[...], RMS_EPS).astype(BF16)
    acc = x
    for c in range(FFN_SPLIT):
        gt = _dot(hb, wgu_ref[:, c * FFN_TF:(c + 1) * FFN_TF])
        up = _dot(hb, wgu_ref[:, D_FF + c * FFN_TF:D_FF + (c + 1) * FFN_TF])
        act = (gt * jax.nn.sigmoid(gt) * up).astype(BF16)
        acc = acc + _dot(act, wd_ref[c * FFN_TF:(c + 1) * FFN_TF, :])
    out_refs[0][...] = acc
    if final_norm:
        out_refs[1][...] = _rms(acc, nf_ref[...], RMS_EPS)


def _ffn(x, norm_w, wgu, wd, norm_final, final_norm, tm):
    t = x.shape[0]
    row_spec = pl.BlockSpec((tm, D_MODEL), lambda i: (i, 0))
    n_out = 2 if final_norm else 1
    return pl.pallas_call(
        functools.partial(_ffn_kernel, final_norm=final_norm),
        grid=(t // tm,),
        in_specs=[
            row_spec,
            pl.BlockSpec((1, D_MODEL), lambda i: (0, 0)),
            _const_spec((D_MODEL, 2 * D_FF)),
            _const_spec((D_FF, D_MODEL)),
            pl.BlockSpec((1, D_MODEL), lambda i: (0, 0)),
        ],
        out_specs=(row_spec,) * n_out,
        out_shape=(jax.ShapeDtypeStruct((t, D_MODEL), F32),) * n_out,
        compiler_params=_cparams("arbitrary"),
        name="ffn",
    )(x, norm_w, wgu, wd, norm_final)


def _layer_params(l, norm_mix, w_in, lambda_q1, lambda_k1, lambda_q2, lambda_k2, attn_subln, conv_w, conv_b,
                  dt_bias, a_log, d_skip, ssm_norm, w_branch_att, w_branch_ssd, w_out, norm_ffn, w_gate_up,
                  w_down):
    w = w_in[l]
    n_main = Q_DIM + Q_DIM + ATT_WIDTH + D_INNER + CONV_DIM
    scale = ATT_HEAD_DIM ** -0.5
    w_main = jnp.concatenate([(w[:, :Q_DIM] * scale).astype(BF16), w[:, Q_DIM:n_main].astype(BF16),
                              w[:, n_main + SSM_HEADS:].astype(BF16)], axis=1)
    w_dt = jnp.pad(w[:, n_main:n_main + SSM_HEADS], ((0, 0), (0, LANES - SSM_HEADS))).astype(BF16)
    head_of_channel = jnp.arange(D_INNER) // SSM_HEAD_DIM
    return {
        "norm_mix": norm_mix[l][None], "w_main": w_main, "w_dt": w_dt,
        "lamp": jnp.stack([lambda_q1[l], lambda_k1[l], lambda_q2[l], lambda_k2[l]]),
        "subln": attn_subln[l][None],
        "conv_w": conv_w[l], "conv_b": conv_b[l][None],
        "dt_bias": jnp.pad(dt_bias[l], (0, LANES - SSM_HEADS))[None],
        "a_log": jnp.pad(a_log[l], (0, LANES - SSM_HEADS))[None],
        "d_skip": jnp.repeat(d_skip[l], SSM_HEAD_DIM)[None],
        "ssm_norm": ssm_norm[l][None],
        "expand": (jnp.arange(LANES)[:, None] == head_of_channel[None, :]).astype(BF16),
        "wa": w_branch_att[l].astype(BF16), "ws": w_branch_ssd[l].astype(BF16), "wo": w_out[l].astype(BF16),
        "norm_ffn": norm_ffn[l][None], "wgu": w_gate_up[l].astype(BF16), "wd": w_down[l].astype(BF16),
    }


def _pad_time(x, b, n, lc):
    return jnp.pad(x.reshape(b, n, x.shape[-1]), ((0, 0), (0, lc - n), (0, 0)))


def kernel(x_prompt, x_sample, cache_k, cache_v, state_conv, state_ssm, norm_mix, w_in, lambda_q1, lambda_k1,
           lambda_q2, lambda_k2, attn_subln, conv_w, conv_b, dt_bias, a_log, d_skip, ssm_norm, w_branch_att,
           w_branch_ssd, w_out, norm_ffn, w_gate_up, w_down, norm_final):
    bp, lp, _ = x_prompt.shape
    bs, ls, _ = x_sample.shape
    assert bp == 1
    past = cache_k.shape[2]
    tp, ts = bp * lp, bs * ls
    xp = x_prompt.reshape(tp, D_MODEL)
    xs = x_sample.reshape(ts, D_MODEL)
    cache_k = cache_k.astype(BF16).reshape(DEPTH, bs, past, Q_DIM)
    cache_v = cache_v.astype(BF16).reshape(DEPTH, bs, past, ATT_WIDTH)
    state_ssm = state_ssm.reshape(DEPTH, bs, SSM_GROUPS, GROUP_WIDTH, SSM_STATE)
    conv0 = jnp.zeros((bp, CONV_WIDTH - 1, CONV_DIM), F32)
    ssm0 = jnp.zeros((bp, SSM_GROUPS, GROUP_WIDTH, SSM_STATE), F32)
    nf = norm_final[None]

    tm_p = 1024 if tp % 1024 == 0 else 128
    tm_row = 512 if tp % 512 == 0 else 128
    tq = 512 if lp % 512 == 0 else CHUNK

    outs_p = [[] for _ in range(4)]
    outs_s = [[] for _ in range(4)]
    yp = ys = None
    for l in range(DEPTH):
        p = _layer_params(l, norm_mix, w_in, lambda_q1, lambda_k1, lambda_q2, lambda_k2, attn_subln, conv_w,
                          conv_b, dt_bias, a_log, d_skip, ssm_norm, w_branch_att, w_branch_ssd, w_out,
                          norm_ffn, w_gate_up, w_down)
        lam_init = 0.8 - 0.6 * math.exp(-0.3 * l)
        last = l == DEPTH - 1

        q, k, kb, v, vb, z, xbc, gates, dt = _in_proj(xp, p["norm_mix"], p["w_main"], p["w_dt"], tm_p)
        att = _attn_prompt(q, kb, vb, p["lamp"], p["subln"], lam_init, tq)
        y, c_new, s_new = _ssd(xbc.reshape(bp, lp, CONV_DIM), z.reshape(bp, lp, D_INNER),
                               dt.reshape(bp, lp, LANES), conv0, ssm0, p, SSD_CHUNK, SSD_CHUNK)
        x1 = _merge(xp, att, y.reshape(tp, D_INNER), gates, p["wa"], p["ws"], p["wo"], tm_row)
        res = _ffn(x1, p["norm_ffn"], p["wgu"], p["wd"], nf, last, tm_row)
        xp = res[0]
        if last:
            yp = res[1]
        outs_p[0].append(k.reshape(bp, lp, ATT_HEADS, 2, ATT_HEAD_DIM))
        outs_p[1].append(v.reshape(bp, lp, ATT_HEADS, ATT_V_DIM))
        outs_p[2].append(c_new)
        outs_p[3].append(s_new.reshape(bp, SSM_GROUPS, SSM_HEADS_PER_GROUP, SSM_HEAD_DIM, SSM_STATE))

        q, k, kb, v, vb, z, xbc, gates, dt = _in_proj(xs, p["norm_mix"], p["w_main"], p["w_dt"], ts)
        att = _attn_sample(q.reshape(bs, ls, Q_DIM), cache_k, cache_v, kb.reshape(bs, ls, Q_DIM),
                           vb.reshape(bs, ls, ATT_WIDTH), p["lamp"], p["subln"], lam_init, l)
        y, c_new, s_new = _ssd(_pad_time(xbc, bs, ls, SSD_PAD_ROWS), _pad_time(z, bs, ls, SSD_PAD_ROWS),
                               _pad_time(dt, bs, ls, SSD_PAD_ROWS), state_conv[l], state_ssm[l], p,
                               SSD_PAD_ROWS, ls)
        x1 = _merge(xs, att.reshape(ts, ATT_WIDTH), y[:, :ls].reshape(ts, D_INNER), gates,
                    p["wa"], p["ws"], p["wo"], ts)
        res = _ffn(x1, p["norm_ffn"], p["wgu"], p["wd"], nf, last, ts)
        xs = res[0]
        if last:
            ys = res[1]
        outs_s[0].append(k.reshape(bs, ls, ATT_HEADS, 2, ATT_HEAD_DIM))
        outs_s[1].append(v.reshape(bs, ls, ATT_HEADS, ATT_V_DIM))
        outs_s[2].append(c_new)
        outs_s[3].append(s_new.reshape(bs, SSM_GROUPS, SSM_HEADS_PER_GROUP, SSM_HEAD_DIM, SSM_STATE))

    return (yp.reshape(bp, lp, D_MODEL), ys.reshape(bs, ls, D_MODEL),
            *(jnp.stack(o) for o in outs_p), *(jnp.stack(o) for o in outs_s))
```

```python
import functools
import math

import jax
import jax.numpy as jnp
from jax import lax
from jax.experimental import pallas as pl
from jax.experimental.pallas import tpu as pltpu

F32 = jnp.float32
BF16 = jnp.bfloat16

D_MODEL = 1024
DEPTH = 2
CHUNK = 64
ATT_HEADS = 8
ATT_HEAD_DIM = 64
ATT_V_DIM = 128
ATT_WIDTH = 1024
Q_DIM = 1024
ATT_SUBLN_EPS = 1e-5
D_INNER = 2048
SSM_HEAD_DIM = 64
SSM_HEADS = 32
SSM_GROUPS = 8
SSM_HEADS_PER_GROUP = 4
SSM_STATE = 128
CONV_WIDTH = 4
CONV_DIM = 4096
SSM_NORM_EPS = 1e-5
D_FF = 2816
RMS_EPS = 1e-6
GROUP_WIDTH = SSM_HEADS_PER_GROUP * SSM_HEAD_DIM

LANES = 128
SUBLANES = 8
VMEM_LIMIT_BYTES = 56 * 1024 * 1024

PROJ_TN = 512
SSD_CHUNK = 256
SSD_PAD_ROWS = 128
CONV_PAD = 8
SAMPLE_TS = 1024
ATTN_TQ = 1024
ATTN_TK = 512
ATTN_UNROLL = 2


def _cparams(*sem):
    return pltpu.CompilerParams(dimension_semantics=sem, vmem_limit_bytes=VMEM_LIMIT_BYTES)


def _const_spec(shape):
    return pl.BlockSpec(shape, lambda *_: (0,) * len(shape), pipeline_mode=pl.Buffered(1))


def _rms(x, w, eps):
    return x * lax.rsqrt(jnp.mean(x * x, axis=-1, keepdims=True) + eps) * w


def _split3(x):
    hi = x.astype(BF16)
    r1 = x - hi.astype(F32)
    mid = r1.astype(BF16)
    lo = (r1 - mid.astype(F32)).astype(BF16)
    return hi, mid, lo


def _dot(a, b):
    return jnp.dot(a, b, preferred_element_type=F32)


def _dot_nt(a, b):
    return lax.dot_general(a, b, (((1,), (1,)), ((), ())), preferred_element_type=F32)


_NQ = Q_DIM // PROJ_TN
_NK = Q_DIM // PROJ_TN
_NV = ATT_WIDTH // PROJ_TN
_NZ = D_INNER // PROJ_TN
_NX = CONV_DIM // PROJ_TN
_NG = 2 * D_MODEL // PROJ_TN
_OFF_K = _NQ
_OFF_V = _OFF_K + _NK
_OFF_Z = _OFF_V + _NV
_OFF_X = _OFF_Z + _NZ
_OFF_G = _OFF_X + _NX
_N_COL_TILES = _OFF_G + _NG


def _in_proj_kernel(x_ref, nw_ref, w_ref, wdt_ref,
                    q_ref, k_ref, kb_ref, v_ref, vb_ref, z_ref, xbc_ref, g_ref, dt_ref, h_sc):
    j = pl.program_id(1)

    @pl.when(j == 0)
    def _():
        hb = _rms(x_ref[...], nw_ref[...], RMS_EPS).astype(BF16)
        h_sc[...] = hb
        dt_ref[...] = _dot(hb, wdt_ref[...])

    def tile():
        return _dot(h_sc[...], w_ref[...])

    @pl.when(j < _OFF_K)
    def _():
        q_ref[...] = tile().astype(BF16)

    @pl.when((j >= _OFF_K) & (j < _OFF_V))
    def _():
        res = tile()
        k_ref[...] = res
        kb_ref[...] = res.astype(BF16)

    @pl.when((j >= _OFF_V) & (j < _OFF_Z))
    def _():
        res = tile()
        v_ref[...] = res
        vb_ref[...] = res.astype(BF16)

    @pl.when((j >= _OFF_Z) & (j < _OFF_X))
    def _():
        z_ref[...] = tile()

    @pl.when((j >= _OFF_X) & (j < _OFF_G))
    def _():
        xbc_ref[...] = tile()

    @pl.when(j >= _OFF_G)
    def _():
        g_ref[...] = tile()


def _in_proj(x, norm_w, w_main, w_dt, tm):
    t = x.shape[0]

    def out_spec(off, n):
        return pl.BlockSpec((tm, PROJ_TN), lambda i, j: (i, jnp.clip(j - off, 0, n - 1)))

    out_shape = (
        jax.ShapeDtypeStruct((t, Q_DIM), BF16),
        jax.ShapeDtypeStruct((t, Q_DIM), F32),
        jax.ShapeDtypeStruct((t, Q_DIM), BF16),
        jax.ShapeDtypeStruct((t, ATT_WIDTH), F32),
        jax.ShapeDtypeStruct((t, ATT_WIDTH), BF16),
        jax.ShapeDtypeStruct((t, D_INNER), F32),
        jax.ShapeDtypeStruct((t, CONV_DIM), F32),
        jax.ShapeDtypeStruct((t, 2 * D_MODEL), F32),
        jax.ShapeDtypeStruct((t, LANES), F32),
    )
    out_specs = (
        out_spec(0, _NQ), out_spec(_OFF_K, _NK), out_spec(_OFF_K, _NK),
        out_spec(_OFF_V, _NV), out_spec(_OFF_V, _NV), out_spec(_OFF_Z, _NZ),
        out_spec(_OFF_X, _NX), out_spec(_OFF_G, _NG),
        pl.BlockSpec((tm, LANES), lambda i, j: (i, 0)),
    )
    return pl.pallas_call(
        _in_proj_kernel,
        grid=(t // tm, _N_COL_TILES),
        in_specs=[
            pl.BlockSpec((tm, D_MODEL), lambda i, j: (i, 0)),
            pl.BlockSpec((1, D_MODEL), lambda i, j: (0, 0)),
            pl.BlockSpec((D_MODEL, PROJ_TN), lambda i, j: (0, j)),
            pl.BlockSpec((D_MODEL, LANES), lambda i, j: (0, 0)),
        ],
        out_specs=out_specs,
        out_shape=out_shape,
        scratch_shapes=[pltpu.VMEM((tm, D_MODEL), BF16)],
        compiler_params=_cparams("arbitrary", "arbitrary"),
        name="in_proj",
    )(x, norm_w, w_main, w_dt)


def _stack_q(q):
    lane = lax.broadcasted_iota(jnp.int32, q.shape, 1)
    zero = jnp.zeros_like(q)
    return jnp.concatenate([jnp.where(lane < ATT_HEAD_DIM, q, zero),
                            jnp.where(lane >= ATT_HEAD_DIM, q, zero)], axis=0)


def _diff_epilogue(o, n, lamp_ref, subln_ref, lam_init):
    lp = lamp_ref[...]
    lam = (jnp.exp(jnp.sum(lp[0:1] * lp[1:2], axis=-1, keepdims=True))
           - jnp.exp(jnp.sum(lp[2:3] * lp[3:4], axis=-1, keepdims=True)) + lam_init)
    a = o[:n] - lam * o[n:]
    return _rms(a, subln_ref[...], ATT_SUBLN_EPS) * (1.0 - lam_init)


def _attn_prompt_kernel(lamp_ref, subln_ref, q_ref, k_ref, v_ref, o_ref, m_sc, l_sc, acc_sc, *,
                        tq, tk, unroll, lam_init):
    qi = pl.program_id(1)
    qs = _stack_q(q_ref[...])
    m_sc[...] = jnp.full(m_sc.shape, -jnp.inf, F32)
    l_sc[...] = jnp.zeros(l_sc.shape, F32)
    acc_sc[...] = jnp.zeros(acc_sc.shape, F32)
    n_lane_tiles = tk // LANES
    full_blocks = qi * (tq // tk)

    def block(kblk, masked, state):
        m_prev, l_prev, acc_prev = state
        start = pl.multiple_of(kblk * tk, tk)
        kb = k_ref[pl.ds(start, tk), :]
        vb = v_ref[pl.ds(start, tk), :]
        s = _dot_nt(qs, kb)
        if masked:
            r = lax.broadcasted_iota(jnp.int32, s.shape, 0)
            c = lax.broadcasted_iota(jnp.int32, s.shape, 1)
            q_chunk = jnp.where(r >= tq, r - tq, r) // CHUNK
            k_chunk = c // CHUNK + (kblk - full_blocks) * (tk // CHUNK)
            s = jnp.where(k_chunk <= q_chunk, s, -jnp.inf)
        m_new = jnp.maximum(m_prev, jnp.max(s, axis=-1, keepdims=True))
        alpha = jnp.exp(m_prev - m_new)
        p = jnp.exp(s - jnp.concatenate([m_new] * n_lane_tiles, axis=1))
        p_lanes = p[:, 0:LANES]
        for t in range(1, n_lane_tiles):
            p_lanes = p_lanes + p[:, t * LANES:(t + 1) * LANES]
        return m_new, alpha * l_prev + p_lanes, alpha * acc_prev + _dot(p.astype(BF16), vb)

    def blocks(first, n_plain, n_masked):
        state = (m_sc[...], l_sc[...], acc_sc[...])
        for n in range(n_plain + n_masked):
            state = block(first + n, n >= n_plain, state)
        m_sc[...], l_sc[...], acc_sc[...] = state

    def body(kp, carry):
        blocks(unroll * kp, unroll, 0)
        return carry

    lax.fori_loop(0, full_blocks // unroll, body, 0)
    diag_blocks = tq // tk
    if diag_blocks % unroll == 0:
        blocks(full_blocks, 0, diag_blocks)
    else:
        rest = full_blocks % unroll
        for r in range(unroll):
            @pl.when(rest == r)
            def _():
                blocks(full_blocks - r, r, diag_blocks)

    l = jnp.sum(l_sc[...], axis=-1, keepdims=True)
    o = acc_sc[...] * (1.0 / l)
    o_ref[...] = _diff_epilogue(o, tq, lamp_ref, subln_ref, lam_init).astype(o_ref.dtype)


def _attn_prompt(q, kb, vb, lamp, subln, lam_init, tq, tk, unroll):
    t = q.shape[0]
    assert t % tq == 0 and tq % tk == 0 and tk % LANES == 0 and tk % CHUNK == 0
    kern = functools.partial(_attn_prompt_kernel, tq=tq, tk=tk, unroll=unroll, lam_init=lam_init)
    return pl.pallas_call(
        kern,
        grid=(ATT_HEADS, t // tq),
        in_specs=[
            pl.BlockSpec((4, ATT_HEAD_DIM), lambda h, i: (0, 0)),
            pl.BlockSpec((1, ATT_V_DIM), lambda h, i: (0, 0)),
            pl.BlockSpec((tq, LANES), lambda h, i: (i, h)),
            pl.BlockSpec((t, LANES), lambda h, i: (0, h)),
            pl.BlockSpec((t, LANES), lambda h, i: (0, h)),
        ],
        out_specs=pl.BlockSpec((tq, LANES), lambda h, i: (i, h)),
        out_shape=jax.ShapeDtypeStruct((t, ATT_WIDTH), BF16),
        scratch_shapes=[pltpu.VMEM((2 * tq, LANES), F32), pltpu.VMEM((2 * tq, LANES), F32),
                        pltpu.VMEM((2 * tq, ATT_V_DIM), F32)],
        compiler_params=_cparams("arbitrary", "arbitrary"),
        name="attn_prompt",
    )(lamp, subln, q, kb, vb)


def _attn_sample_kernel(lamp_ref, subln_ref, q_ref, kt_ref, vc_ref, kn_ref, vn_ref, o_ref, m_sc, l_sc, acc_sc, *,
                        n, ts, lam_init):
    piece = pl.program_id(1)

    @pl.when(piece == 0)
    def _():
        m_sc[...] = jnp.full(m_sc.shape, -jnp.inf, F32)
        l_sc[...] = jnp.zeros(l_sc.shape, F32)
        acc_sc[...] = jnp.zeros(acc_sc.shape, F32)

    def update(h, s, v):
        width = s.shape[1]
        m_prev = m_sc[h]
        m_new = jnp.maximum(m_prev, jnp.max(s, axis=-1, keepdims=True))
        alpha = jnp.exp(m_prev - m_new)
        m_wide = m_new[:, :width] if width <= LANES else jnp.concatenate([m_new] * (width // LANES), axis=1)
        p = jnp.exp(s - m_wide)
        l_sc[h] = alpha * l_sc[h] + jnp.sum(p, axis=-1, keepdims=True)
        acc_sc[h] = alpha * acc_sc[h] + _dot(p.astype(BF16), v)
        m_sc[h] = m_new

    for h in range(ATT_HEADS):
        hs = slice(h * LANES, (h + 1) * LANES)
        qs = _stack_q(q_ref[0, :, hs])
        s = _dot(qs, kt_ref[0, 0, hs, :].astype(BF16))
        update(h, s, vc_ref[0, 0, pl.ds(h, ts, stride=ATT_HEADS), :].astype(BF16))

    @pl.when(piece == pl.num_programs(1) - 1)
    def _():
        for h in range(ATT_HEADS):
            hs = slice(h * LANES, (h + 1) * LANES)
            qs = _stack_q(q_ref[0, :, hs])
            update(h, _dot_nt(qs, kn_ref[0, :, hs]), vn_ref[0, :, hs])
            o = acc_sc[h] * (1.0 / l_sc[h])
            o_ref[0, :, hs] = _diff_epilogue(o, n, lamp_ref, subln_ref, lam_init).astype(o_ref.dtype)


def _attn_sample(q, cache_kt, cache_v, kb, vb, lamp, subln, lam_init, layer):
    b, n, _ = q.shape
    past = cache_kt.shape[3]
    ts = min(past, SAMPLE_TS)
    assert past % CHUNK == 0 and n <= CHUNK and past % ts == 0
    kern = functools.partial(_attn_sample_kernel, n=n, ts=ts, lam_init=lam_init)
    row_spec = pl.BlockSpec((1, n, Q_DIM), lambda bi, s: (bi, 0, 0))
    return pl.pallas_call(
        kern,
        grid=(b, past // ts),
        in_specs=[
            pl.BlockSpec((4, ATT_HEAD_DIM), lambda bi, s: (0, 0)),
            pl.BlockSpec((1, ATT_V_DIM), lambda bi, s: (0, 0)),
            row_spec,
            pl.BlockSpec((1, 1, Q_DIM, ts), lambda bi, s: (layer, bi, 0, s)),
            pl.BlockSpec((1, 1, ts * ATT_HEADS, ATT_V_DIM), lambda bi, s: (layer, bi, s, 0)),
            row_spec,
            row_spec,
        ],
        out_specs=row_spec,
        out_shape=jax.ShapeDtypeStruct((b, n, ATT_WIDTH), BF16),
        scratch_shapes=[pltpu.VMEM((ATT_HEADS, 2 * n, LANES), F32)] * 3,
        compiler_params=_cparams("arbitrary", "arbitrary"),
        name="attn_sample",
    )(lamp, subln, q, cache_kt, cache_v, kb, vb)


def _ssd_kernel(xbc_ref, z_ref, dt_ref, cbuf_ref, st0_ref, cw_ref, cb_ref, dtb_ref, alog_ref, dsk_ref,
                nrm_ref, expand_ref, y_ref, cout_ref, sout_ref, xpad_sc, st_sc, *, lc, valid_last):
    c = pl.program_id(1)
    nchunks = pl.num_programs(1)

    @pl.when(c == 0)
    def _():
        xpad_sc[0:CONV_PAD, :] = jnp.zeros((CONV_PAD, CONV_DIM), F32)
        xpad_sc[CONV_PAD - (CONV_WIDTH - 1):CONV_PAD, :] = cbuf_ref[0]
        for g in range(SSM_GROUPS):
            st_sc[g] = st0_ref[0, g].T

    xpad_sc[CONV_PAD:CONV_PAD + lc, :] = xbc_ref[0]
    xp = xpad_sc[...]
    conv = cw_ref[0:1, :] * xp
    for j in range(1, CONV_WIDTH):
        conv = cw_ref[j:j + 1, :] * xp + pltpu.roll(conv, 1, 0)
    conv = conv[CONV_PAD:CONV_PAD + lc] + cb_ref[...]
    act = conv * jax.nn.sigmoid(conv)

    @pl.when(c == nchunks - 1)
    def _():
        end = CONV_PAD + valid_last
        cout_ref[0] = xpad_sc[end - (CONV_WIDTH - 1):end, :]

    xpad_sc[0:CONV_PAD, :] = xpad_sc[lc:lc + CONV_PAD, :]

    dtr = dt_ref[0] + dtb_ref[...]
    dt = jnp.maximum(dtr, 0.0) + jnp.log1p(jnp.exp(-jnp.abs(dtr)))
    if valid_last < lc:
        row = lax.broadcasted_iota(jnp.int32, dt.shape, 0)
        dt = jnp.where(row < valid_last, dt, 0.0)
    a = dt * (-jnp.exp(alog_ref[...]))
    rr = lax.broadcasted_iota(jnp.int32, (lc, lc), 0)
    cc = lax.broadcasted_iota(jnp.int32, (lc, lc), 1)
    causal = rr >= cc
    tri = jnp.where(causal, 1.0, 0.0).astype(BF16)
    a_cs = sum(_dot(tri, part) for part in _split3(a))
    a_last = a_cs[lc - 1:lc, :]
    a_cs_t = a_cs.T

    expand = expand_ref[...]
    stack = jnp.concatenate([dt, dt * jnp.exp(a_last - a_cs), jnp.exp(a_cs)], axis=0)
    ex = _dot(stack.astype(BF16), expand)
    w_dt = ex[0:lc]
    w_state = ex[lc:2 * lc]
    w_off = ex[2 * lc:3 * lc]
    carry = jnp.broadcast_to(jnp.exp(a_last), (2 * SUBLANES, LANES))
    w_carry = sum(_dot(part, expand) for part in _split3(carry))[0:1]

    xs = act[:, :D_INNER]
    xdt = xs * w_dt
    xst = xs * w_state
    lane = lax.broadcasted_iota(jnp.int32, (lc, LANES), 1)
    lo_half = lane < SSM_HEAD_DIM

    for g in range(SSM_GROUPS):
        gs = slice(g * GROUP_WIDTH, (g + 1) * GROUP_WIDTH)
        bg = act[:, D_INNER + g * SSM_STATE:D_INNER + (g + 1) * SSM_STATE]
        cg = act[:, D_INNER + SSM_GROUPS * SSM_STATE + g * SSM_STATE:
                 D_INNER + SSM_GROUPS * SSM_STATE + (g + 1) * SSM_STATE]
        bgb = bg.astype(BF16)
        cgb = cg.astype(BF16)
        cb = _dot_nt(cgb, bgb)
        st_prev = st_sc[g]
        y_off = _dot(cgb, st_prev.astype(BF16)) * w_off[:, gs]
        st_sc[g] = w_carry[:, gs] * st_prev + _dot(bg.T.astype(BF16), xst[:, gs].astype(BF16))

        pairs = []
        for pr in range(SSM_HEADS_PER_GROUP // 2):
            x_pair = xdt[:, g * GROUP_WIDTH + pr * LANES:g * GROUP_WIDTH + (pr + 1) * LANES]
            y_pair = None
            for half in range(2):
                h = g * SSM_HEADS_PER_GROUP + pr * 2 + half
                seg = a_cs[:, h:h + 1] - a_cs_t[h:h + 1, :]
                decay = jnp.exp(jnp.where(causal, seg, -jnp.inf))
                mix = (cb * decay).astype(BF16)
                keep = lo_half if half == 0 else jnp.logical_not(lo_half)
                contrib = _dot(mix, jnp.where(keep, x_pair, 0.0).astype(BF16))
                y_pair = contrib if y_pair is None else y_pair + contrib
            pairs.append(y_pair)
        y_g = jnp.concatenate(pairs, axis=1) + y_off + dsk_ref[:, gs] * xs[:, gs]

        zg = z_ref[0, :, gs]
        yz = y_g * (zg * jax.nn.sigmoid(zg))
        y_ref[0, :, gs] = _rms(yz, nrm_ref[:, gs], SSM_NORM_EPS).astype(y_ref.dtype)

    @pl.when(c == nchunks - 1)
    def _():
        for g in range(SSM_GROUPS):
            sout_ref[0, g] = st_sc[g].T


def _ssd(xbc, z, dt, conv_buf, state0, p, lc, valid_last):
    b, l, _ = xbc.shape
    assert l % lc == 0 and CONV_WIDTH - 1 <= valid_last <= lc
    assert valid_last == lc or l == lc
    kern = functools.partial(_ssd_kernel, lc=lc, valid_last=valid_last)
    return pl.pallas_call(
        kern,
        grid=(b, l // lc),
        in_specs=[
            pl.BlockSpec((1, lc, CONV_DIM), lambda bi, c: (bi, c, 0)),
            pl.BlockSpec((1, lc, D_INNER), lambda bi, c: (bi, c, 0)),
            pl.BlockSpec((1, lc, LANES), lambda bi, c: (bi, c, 0)),
            pl.BlockSpec((1, CONV_WIDTH - 1, CONV_DIM), lambda bi, c: (bi, 0, 0)),
            pl.BlockSpec((1, SSM_GROUPS, GROUP_WIDTH, SSM_STATE), lambda bi, c: (bi, 0, 0, 0)),
            pl.BlockSpec((CONV_WIDTH, CONV_DIM), lambda bi, c: (0, 0)),
            pl.BlockSpec((1, CONV_DIM), lambda bi, c: (0, 0)),
            pl.BlockSpec((1, LANES), lambda bi, c: (0, 0)),
            pl.BlockSpec((1, LANES), lambda bi, c: (0, 0)),
            pl.BlockSpec((1, D_INNER), lambda bi, c: (0, 0)),
            pl.BlockSpec((1, D_INNER), lambda bi, c: (0, 0)),
            pl.BlockSpec((LANES, D_INNER), lambda bi, c: (0, 0)),
        ],
        out_specs=(
            pl.BlockSpec((1, lc, D_INNER), lambda bi, c: (bi, c, 0)),
            pl.BlockSpec((1, CONV_WIDTH - 1, CONV_DIM), lambda bi, c: (bi, 0, 0)),
            pl.BlockSpec((1, SSM_GROUPS, GROUP_WIDTH, SSM_STATE), lambda bi, c: (bi, 0, 0, 0)),
        ),
        out_shape=(
            jax.ShapeDtypeStruct((b, l, D_INNER), BF16),
            jax.ShapeDtypeStruct((b, CONV_WIDTH - 1, CONV_DIM), F32),
            jax.ShapeDtypeStruct((b, SSM_GROUPS, GROUP_WIDTH, SSM_STATE), F32),
        ),
        scratch_shapes=[pltpu.VMEM((lc + CONV_PAD, CONV_DIM), F32),
                        pltpu.VMEM((SSM_GROUPS, SSM_STATE, GROUP_WIDTH), F32)],
        compiler_params=_cparams("arbitrary", "arbitrary"),
        name="ssd",
    )(xbc, z, dt, conv_buf, state0, p["conv_w"], p["conv_b"], p["dt_bias"], p["a_log"], p["d_skip"],
      p["ssm_norm"], p["expand"])


def _merge_kernel(x_ref, att_ref, y_ref, g_ref, wa_ref, ws_ref, wo_ref, o_ref):
    ba = _dot(att_ref[...], wa_ref[...])
    bs = _dot(y_ref[...], ws_ref[...])
    g = jax.nn.sigmoid(g_ref[...])
    merged = g[:, :D_MODEL] * ba + g[:, D_MODEL:] * bs
    o_ref[...] = x_ref[...] + _dot(merged.astype(BF16), wo_ref[...])


def _merge(x, att, y, gates, wa, ws, wo, tm):
    t = x.shape[0]
    return pl.pallas_call(
        _merge_kernel,
        grid=(t // tm,),
        in_specs=[
            pl.BlockSpec((tm, D_MODEL), lambda i: (i, 0)),
            pl.BlockSpec((tm, ATT_WIDTH), lambda i: (i, 0)),
            pl.BlockSpec((tm, D_INNER), lambda i: (i, 0)),
            pl.BlockSpec((tm, 2 * D_MODEL), lambda i: (i, 0)),
            _const_spec((ATT_WIDTH, D_MODEL)),
            _const_spec((D_INNER, D_MODEL)),
            _const_spec((D_MODEL, D_MODEL)),
        ],
        out_specs=pl.BlockSpec((tm, D_MODEL), lambda i: (i, 0)),
        out_shape=jax.ShapeDtypeStruct((t, D_MODEL), F32),
        compiler_params=_cparams("arbitrary"),
        name="merge",
    )(x, att, y, gates, wa, ws, wo)


FFN_SPLIT = 2
FFN_TF = D_FF // FFN_SPLIT


def _ffn_kernel(x_ref, nw_ref, wgu_ref, wd_ref, nf_ref, *out_refs, final_norm):
    x = x_ref[...]
    hb = _rms(x, nw_ref[...], RMS_EPS).astype(BF16)
    acc = x
    for c in range(FFN_SPLIT):
        gt = _dot(hb, wgu_ref[:, c * FFN_TF:(c + 1) * FFN_TF])
        up = _dot(hb, wgu_ref[:, D_FF + c * FFN_TF:D_FF + (c + 1) * FFN_TF])
        act = (gt * jax.nn.sigmoid(gt) * up).astype(BF16)
        acc = acc + _dot(act, wd_ref[c * FFN_TF:(c + 1) * FFN_TF, :])
    out_refs[0][...] = acc
    if final_norm:
        out_refs[1][...] = _rms(acc, nf_ref[...], RMS_EPS)


def _ffn(x, norm_w, wgu, wd, norm_final, final_norm, tm):
    t = x.shape[0]
    row_spec = pl.BlockSpec((tm, D_MODEL), lambda i: (i, 0))
    n_out = 2 if final_norm else 1
    return pl.pallas_call(
        functools.partial(_ffn_kernel, final_norm=final_norm),
        grid=(t // tm,),
        in_specs=[
            row_spec,
            pl.BlockSpec((1, D_MODEL), lambda i: (0, 0)),
            _const_spec((D_MODEL, 2 * D_FF)),
            _const_spec((D_FF, D_MODEL)),
            pl.BlockSpec((1, D_MODEL), lambda i: (0, 0)),
        ],
        out_specs=(row_spec,) * n_out,
        out_shape=(jax.ShapeDtypeStruct((t, D_MODEL), F32),) * n_out,
        compiler_params=_cparams("arbitrary"),
        name="ffn",
    )(x, norm_w, wgu, wd, norm_final)


def _layer_params(l, norm_mix, w_in, lambda_q1, lambda_k1, lambda_q2, lambda_k2, attn_subln, conv_w, conv_b,
                  dt_bias, a_log, d_skip, ssm_norm, w_branch_att, w_branch_ssd, w_out, norm_ffn, w_gate_up,
                  w_down):
    w = w_in[l]
    n_main = Q_DIM + Q_DIM + ATT_WIDTH + D_INNER + CONV_DIM
    scale = ATT_HEAD_DIM ** -0.5
    w_main = jnp.concatenate([(w[:, :Q_DIM] * scale).astype(BF16), w[:, Q_DIM:n_main].astype(BF16),
                              w[:, n_main + SSM_HEADS:].astype(BF16)], axis=1)
    w_dt = jnp.pad(w[:, n_main:n_main + SSM_HEADS], ((0, 0), (0, LANES - SSM_HEADS))).astype(BF16)
    head_of_channel = jnp.arange(D_INNER) // SSM_HEAD_DIM
    return {
        "norm_mix": norm_mix[l][None], "w_main": w_main, "w_dt": w_dt,
        "lamp": jnp.stack([lambda_q1[l], lambda_k1[l], lambda_q2[l], lambda_k2[l]]),
        "subln": attn_subln[l][None],
        "conv_w": conv_w[l], "conv_b": conv_b[l][None],
        "dt_bias": jnp.pad(dt_bias[l], (0, LANES - SSM_HEADS))[None],
        "a_log": jnp.pad(a_log[l], (0, LANES - SSM_HEADS))[None],
        "d_skip": jnp.repeat(d_skip[l], SSM_HEAD_DIM)[None],
        "ssm_norm": ssm_norm[l][None],
        "expand": (jnp.arange(LANES)[:, None] == head_of_channel[None, :]).astype(BF16),
        "wa": w_branch_att[l].astype(BF16), "ws": w_branch_ssd[l].astype(BF16), "wo": w_out[l].astype(BF16),
        "norm_ffn": norm_ffn[l][None], "wgu": w_gate_up[l].astype(BF16), "wd": w_down[l].astype(BF16),
    }


def _pad_time(x, b, n, lc):
    return jnp.pad(x.reshape(b, n, x.shape[-1]), ((0, 0), (0, lc - n), (0, 0)))


def kernel(x_prompt, x_sample, cache_k, cache_v, state_conv, state_ssm, norm_mix, w_in, lambda_q1, lambda_k1,
           lambda_q2, lambda_k2, attn_subln, conv_w, conv_b, dt_bias, a_log, d_skip, ssm_norm, w_branch_att,
           w_branch_ssd, w_out, norm_ffn, w_gate_up, w_down, norm_final):
    bp, lp, _ = x_prompt.shape
    bs, ls, _ = x_sample.shape
    assert bp == 1
    past = cache_k.shape[2]
    tp, ts = bp * lp, bs * ls
    xp = x_prompt.reshape(tp, D_MODEL)
    xs = x_sample.reshape(ts, D_MODEL)
    cache_k = jnp.transpose(cache_k, (0, 1, 3, 4, 5, 2)).reshape(DEPTH, bs, Q_DIM, past)
    cache_v = cache_v.reshape(DEPTH, bs, past * ATT_HEADS, ATT_V_DIM)
    state_ssm = state_ssm.reshape(DEPTH, bs, SSM_GROUPS, GROUP_WIDTH, SSM_STATE)
    conv0 = jnp.zeros((bp, CONV_WIDTH - 1, CONV_DIM), F32)
    ssm0 = jnp.zeros((bp, SSM_GROUPS, GROUP_WIDTH, SSM_STATE), F32)
    nf = norm_final[None]

    tm_p = 1024 if tp % 1024 == 0 else 128
    tm_row = 512 if tp % 512 == 0 else 128
    tq = ATTN_TQ if lp % ATTN_TQ == 0 else CHUNK
    tk = ATTN_TK if tq % ATTN_TK == 0 else tq

    outs_p = [[] for _ in range(4)]
    outs_s = [[] for _ in range(4)]
    yp = ys = None
    for l in range(DEPTH):
        p = _layer_params(l, norm_mix, w_in, lambda_q1, lambda_k1, lambda_q2, lambda_k2, attn_subln, conv_w,
                          conv_b, dt_bias, a_log, d_skip, ssm_norm, w_branch_att, w_branch_ssd, w_out,
                          norm_ffn, w_gate_up, w_down)
        lam_init = 0.8 - 0.6 * math.exp(-0.3 * l)
        last = l == DEPTH - 1

        q, k, kb, v, vb, z, xbc, gates, dt = _in_proj(xp, p["norm_mix"], p["w_main"], p["w_dt"], tm_p)
        att = _attn_prompt(q, kb, vb, p["lamp"], p["subln"], lam_init, tq, tk, ATTN_UNROLL)
        y, c_new, s_new = _ssd(xbc.reshape(bp, lp, CONV_DIM), z.reshape(bp, lp, D_INNER),
                               dt.reshape(bp, lp, LANES), conv0, ssm0, p, SSD_CHUNK, SSD_CHUNK)
        x1 = _merge(xp, att, y.reshape(tp, D_INNER), gates, p["wa"], p["ws"], p["wo"], tm_row)
        res = _ffn(x1, p["norm_ffn"], p["wgu"], p["wd"], nf, last, tm_row)
        xp = res[0]
        if last:
            yp = res[1]
        outs_p[0].append(k.reshape(bp, lp, ATT_HEADS, 2, ATT_HEAD_DIM))
        outs_p[1].append(v.reshape(bp, lp, ATT_HEADS, ATT_V_DIM))
        outs_p[2].append(c_new)
        outs_p[3].append(s_new.reshape(bp, SSM_GROUPS, SSM_HEADS_PER_GROUP, SSM_HEAD_DIM, SSM_STATE))

        q, k, kb, v, vb, z, xbc, gates, dt = _in_proj(xs, p["norm_mix"], p["w_main"], p["w_dt"], ts)
        att = _attn_sample(q.reshape(bs, ls, Q_DIM), cache_k, cache_v, kb.reshape(bs, ls, Q_DIM),
                           vb.reshape(bs, ls, ATT_WIDTH), p["lamp"], p["subln"], lam_init, l)
        y, c_new, s_new = _ssd(_pad_time(xbc, bs, ls, SSD_PAD_ROWS), _pad_time(z, bs, ls, SSD_PAD_ROWS),
                               _pad_time(dt, bs, ls, SSD_PAD_ROWS), state_conv[l], state_ssm[l], p,
                               SSD_PAD_ROWS, ls)
        x1 = _merge(xs, att.reshape(ts, ATT_WIDTH), y[:, :ls].reshape(ts, D_INNER), gates,
                    p["wa"], p["ws"], p["wo"], ts)
        res = _ffn(x1, p["norm_ffn"], p["wgu"], p["wd"], nf, last, ts)
        xs = res[0]
        if last:
            ys = res[1]
        outs_s[0].append(k.reshape(bs, ls, ATT_HEADS, 2, ATT_HEAD_DIM))
        outs_s[1].append(v.reshape(bs, ls, ATT_HEADS, ATT_V_DIM))
        outs_s[2].append(c_new)
        outs_s[3].append(s_new.reshape(bs, SSM_GROUPS, SSM_HEADS_PER_GROUP, SSM_HEAD_DIM, SSM_STATE))

    return (yp.reshape(bp, lp, D_MODEL), ys.reshape(bs, ls, D_MODEL),
            *(jnp.stack(o) for o in outs_p), *(jnp.stack(o) for o in outs_s))
```

```python
import functools
import math

import jax
import jax.numpy as jnp
from jax import lax
from jax.experimental import pallas as pl
from jax.experimental.pallas import tpu as pltpu

F32 = jnp.float32
BF16 = jnp.bfloat16
LOG2E = math.log2(math.e)

D_MODEL = 1024
DEPTH = 2
CHUNK = 64
ATT_HEADS = 8
ATT_HEAD_DIM = 64
ATT_V_DIM = 128
ATT_WIDTH = 1024
Q_DIM = 1024
ATT_SUBLN_EPS = 1e-5
D_INNER = 2048
SSM_HEAD_DIM = 64
SSM_HEADS = 32
SSM_GROUPS = 8
SSM_HEADS_PER_GROUP = 4
SSM_STATE = 128
CONV_WIDTH = 4
CONV_DIM = 4096
SSM_NORM_EPS = 1e-5
D_FF = 2816
RMS_EPS = 1e-6
GROUP_WIDTH = SSM_HEADS_PER_GROUP * SSM_HEAD_DIM

LANES = 128
SUBLANES = 8
VMEM_LIMIT_BYTES = 56 * 1024 * 1024

PROJ_TN = 512
SSD_CHUNK = 256
SSD_PAD_ROWS = 128
CONV_PAD = 8
SAMPLE_TS = 1024
ATTN_TQ = 1024
ATTN_TK = 512
ATTN_UNROLL = 2


def _cparams(*sem):
    return pltpu.CompilerParams(dimension_semantics=sem, vmem_limit_bytes=VMEM_LIMIT_BYTES)


def _const_spec(shape):
    return pl.BlockSpec(shape, lambda *_: (0,) * len(shape), pipeline_mode=pl.Buffered(1))


def _rms(x, w, eps):
    return x * lax.rsqrt(jnp.mean(x * x, axis=-1, keepdims=True) + eps) * w


def _split3(x):
    hi = x.astype(BF16)
    r1 = x - hi.astype(F32)
    mid = r1.astype(BF16)
    lo = (r1 - mid.astype(F32)).astype(BF16)
    return hi, mid, lo


def _dot(a, b):
    return jnp.dot(a, b, preferred_element_type=F32)


def _dot_nt(a, b):
    return lax.dot_general(a, b, (((1,), (1,)), ((), ())), preferred_element_type=F32)


_NQ = Q_DIM // PROJ_TN
_NK = Q_DIM // PROJ_TN
_NV = ATT_WIDTH // PROJ_TN
_NZ = D_INNER // PROJ_TN
_NX = CONV_DIM // PROJ_TN
_NG = 2 * D_MODEL // PROJ_TN
_OFF_K = _NQ
_OFF_V = _OFF_K + _NK
_OFF_Z = _OFF_V + _NV
_OFF_X = _OFF_Z + _NZ
_OFF_G = _OFF_X + _NX
_N_COL_TILES = _OFF_G + _NG


def _in_proj_kernel(x_ref, nw_ref, w_ref, wdt_ref,
                    q_ref, k_ref, kb_ref, v_ref, vb_ref, z_ref, xbc_ref, g_ref, dt_ref, h_sc):
    j = pl.program_id(1)

    @pl.when(j == 0)
    def _():
        hb = _rms(x_ref[...], nw_ref[...], RMS_EPS).astype(BF16)
        h_sc[...] = hb
        dt_ref[...] = _dot(hb, wdt_ref[...])

    def tile():
        return _dot(h_sc[...], w_ref[...])

    @pl.when(j < _OFF_K)
    def _():
        q_ref[...] = (tile() * LOG2E).astype(BF16)

    @pl.when((j >= _OFF_K) & (j < _OFF_V))
    def _():
        res = tile()
        k_ref[...] = res
        kb_ref[...] = res.astype(BF16)

    @pl.when((j >= _OFF_V) & (j < _OFF_Z))
    def _():
        res = tile()
        v_ref[...] = res
        vb_ref[...] = res.astype(BF16)

    @pl.when((j >= _OFF_Z) & (j < _OFF_X))
    def _():
        z_ref[...] = tile()

    @pl.when((j >= _OFF_X) & (j < _OFF_G))
    def _():
        xbc_ref[...] = tile()

    @pl.when(j >= _OFF_G)
    def _():
        g_ref[...] = tile()


def _in_proj(x, norm_w, w_main, w_dt, tm):
    t = x.shape[0]

    def out_spec(off, n):
        return pl.BlockSpec((tm, PROJ_TN), lambda i, j: (i, jnp.clip(j - off, 0, n - 1)))

    out_shape = (
        jax.ShapeDtypeStruct((t, Q_DIM), BF16),
        jax.ShapeDtypeStruct((t, Q_DIM), F32),
        jax.ShapeDtypeStruct((t, Q_DIM), BF16),
        jax.ShapeDtypeStruct((t, ATT_WIDTH), F32),
        jax.ShapeDtypeStruct((t, ATT_WIDTH), BF16),
        jax.ShapeDtypeStruct((t, D_INNER), F32),
        jax.ShapeDtypeStruct((t, CONV_DIM), F32),
        jax.ShapeDtypeStruct((t, 2 * D_MODEL), F32),
        jax.ShapeDtypeStruct((t, LANES), F32),
    )
    out_specs = (
        out_spec(0, _NQ), out_spec(_OFF_K, _NK), out_spec(_OFF_K, _NK),
        out_spec(_OFF_V, _NV), out_spec(_OFF_V, _NV), out_spec(_OFF_Z, _NZ),
        out_spec(_OFF_X, _NX), out_spec(_OFF_G, _NG),
        pl.BlockSpec((tm, LANES), lambda i, j: (i, 0)),
    )
    return pl.pallas_call(
        _in_proj_kernel,
        grid=(t // tm, _N_COL_TILES),
        in_specs=[
            pl.BlockSpec((tm, D_MODEL), lambda i, j: (i, 0)),
            pl.BlockSpec((1, D_MODEL), lambda i, j: (0, 0)),
            pl.BlockSpec((D_MODEL, PROJ_TN), lambda i, j: (0, j)),
            pl.BlockSpec((D_MODEL, LANES), lambda i, j: (0, 0)),
        ],
        out_specs=out_specs,
        out_shape=out_shape,
        scratch_shapes=[pltpu.VMEM((tm, D_MODEL), BF16)],
        compiler_params=_cparams("arbitrary", "arbitrary"),
        name="in_proj",
    )(x, norm_w, w_main, w_dt)


def _stack_q(q):
    lane = lax.broadcasted_iota(jnp.int32, q.shape, 1)
    zero = jnp.zeros_like(q)
    return jnp.concatenate([jnp.where(lane < ATT_HEAD_DIM, q, zero),
                            jnp.where(lane >= ATT_HEAD_DIM, q, zero)], axis=0)


def _diff_epilogue(o, n, lamp_ref, subln_ref, lam_init):
    lp = lamp_ref[...]
    lam = (jnp.exp(jnp.sum(lp[0:1] * lp[1:2], axis=-1, keepdims=True))
           - jnp.exp(jnp.sum(lp[2:3] * lp[3:4], axis=-1, keepdims=True)) + lam_init)
    a = o[:n] - lam * o[n:]
    return _rms(a, subln_ref[...], ATT_SUBLN_EPS) * (1.0 - lam_init)


def _attn_prompt_kernel(lamp_ref, subln_ref, q_ref, k_ref, v_ref, o_ref, m_sc, l_sc, acc_sc, s_sc, *,
                        tq, tk, nq, unroll, lam_init):
    qi = pl.program_id(1)
    qs = _stack_q(q_ref[...])
    m_sc[...] = jnp.full(m_sc.shape, -jnp.inf, F32)
    l_sc[...] = jnp.zeros(l_sc.shape, F32)
    acc_sc[...] = jnp.zeros(acc_sc.shape, F32)
    n_lane_tiles = tk // LANES
    full_blocks = qi * (tq // tk)

    def scores(kblk, masked):
        start = pl.multiple_of(kblk * tk, tk)
        s = _dot_nt(qs, k_ref[pl.ds(start, tk), :])
        if masked:
            r = lax.broadcasted_iota(jnp.int32, s.shape, 0)
            c = lax.broadcasted_iota(jnp.int32, s.shape, 1)
            q_chunk = jnp.where(r >= tq, r - tq, r) // CHUNK
            k_chunk = c // CHUNK + (kblk - full_blocks) * (tk // CHUNK)
            s = jnp.where(k_chunk <= q_chunk, s, -jnp.inf)
        return s

    def absorb(kblk, s, state):
        m_prev, l_prev, acc_prev = state
        vb = v_ref[pl.ds(pl.multiple_of(kblk * tk, tk), tk), :]
        m_new = jnp.maximum(m_prev, jnp.max(s, axis=-1, keepdims=True))
        alpha = jnp.exp2(m_prev - m_new)
        p = jnp.exp2(s - jnp.concatenate([m_new] * n_lane_tiles, axis=1))
        pv = _dot(p.astype(BF16), jnp.concatenate([vb, jnp.ones((tk, LANES), BF16)], axis=1))
        return m_new, alpha * l_prev + pv[:, ATT_V_DIM:], alpha * acc_prev + pv[:, :ATT_V_DIM]

    PLAIN, MASKED = "plain", "masked"

    def pipeline(first, produce):
        state = (m_sc[...], l_sc[...], acc_sc[...])
        s = s_sc[...]
        for i, kind in enumerate(produce):
            s_next = None if kind is None else scores(first + i + 1, kind == MASKED)
            state = absorb(first + i, s, state)
            s = s_next
        if s is not None:
            s_sc[...] = s
        m_sc[...], l_sc[...], acc_sc[...] = state

    diag_blocks = tq // tk

    @pl.when(qi == 0)
    def _():
        s_sc[...] = scores(0, True)
        pipeline(0, [MASKED] * (diag_blocks - 1) + [None])

    @pl.when(qi > 0)
    def _():
        s_sc[...] = scores(0, False)

    def body(t, carry):
        pipeline(unroll * t, [PLAIN] * unroll)
        return carry

    plain_steps = jnp.maximum(full_blocks - 1, 0)
    lax.fori_loop(0, plain_steps // unroll, body, 0)
    for r in sorted({(diag_blocks * q - 1) % unroll for q in range(1, nq)}):
        @pl.when((qi > 0) & (plain_steps % unroll == r))
        def _():
            pipeline(full_blocks - 1 - r, [PLAIN] * r + [MASKED] * diag_blocks + [None])

    o = acc_sc[...] * (1.0 / l_sc[...])
    o_ref[...] = _diff_epilogue(o, tq, lamp_ref, subln_ref, lam_init).astype(o_ref.dtype)


def _attn_prompt(q, kb, vb, lamp, subln, lam_init, tq, tk, unroll):
    t = q.shape[0]
    assert t % tq == 0 and tq % tk == 0 and tk % LANES == 0 and tk % CHUNK == 0
    kern = functools.partial(_attn_prompt_kernel, tq=tq, tk=tk, nq=t // tq, unroll=unroll, lam_init=lam_init)
    return pl.pallas_call(
        kern,
        grid=(ATT_HEADS, t // tq),
        in_specs=[
            pl.BlockSpec((4, ATT_HEAD_DIM), lambda h, i: (0, 0)),
            pl.BlockSpec((1, ATT_V_DIM), lambda h, i: (0, 0)),
            pl.BlockSpec((tq, LANES), lambda h, i: (i, h)),
            pl.BlockSpec((t, LANES), lambda h, i: (0, h)),
            pl.BlockSpec((t, LANES), lambda h, i: (0, h)),
        ],
        out_specs=pl.BlockSpec((tq, LANES), lambda h, i: (i, h)),
        out_shape=jax.ShapeDtypeStruct((t, ATT_WIDTH), BF16),
        scratch_shapes=[pltpu.VMEM((2 * tq, LANES), F32), pltpu.VMEM((2 * tq, LANES), F32),
                        pltpu.VMEM((2 * tq, ATT_V_DIM), F32), pltpu.VMEM((2 * tq, tk), F32)],
        compiler_params=_cparams("arbitrary", "arbitrary"),
        name="attn_prompt",
    )(lamp, subln, q, kb, vb)


def _attn_sample_kernel(lamp_ref, subln_ref, q_ref, kt_ref, vc_ref, kn_ref, vn_ref, o_ref, m_sc, l_sc, acc_sc, *,
                        n, ts, lam_init):
    piece = pl.program_id(1)

    @pl.when(piece == 0)
    def _():
        m_sc[...] = jnp.full(m_sc.shape, -jnp.inf, F32)
        l_sc[...] = jnp.zeros(l_sc.shape, F32)
        acc_sc[...] = jnp.zeros(acc_sc.shape, F32)

    def update(h, s, v):
        width = s.shape[1]
        m_prev = m_sc[h]
        m_new = jnp.maximum(m_prev, jnp.max(s, axis=-1, keepdims=True))
        alpha = jnp.exp2(m_prev - m_new)
        m_wide = m_new[:, :width] if width <= LANES else jnp.concatenate([m_new] * (width // LANES), axis=1)
        p = jnp.exp2(s - m_wide)
        l_sc[h] = alpha * l_sc[h] + jnp.sum(p, axis=-1, keepdims=True)
        acc_sc[h] = alpha * acc_sc[h] + _dot(p.astype(BF16), v)
        m_sc[h] = m_new

    for h in range(ATT_HEADS):
        hs = slice(h * LANES, (h + 1) * LANES)
        qs = _stack_q(q_ref[0, :, hs])
        s = _dot(qs, kt_ref[0, 0, hs, :].astype(BF16))
        update(h, s, vc_ref[0, 0, pl.ds(h, ts, stride=ATT_HEADS), :].astype(BF16))

    @pl.when(piece == pl.num_programs(1) - 1)
    def _():
        for h in range(ATT_HEADS):
            hs = slice(h * LANES, (h + 1) * LANES)
            qs = _stack_q(q_ref[0, :, hs])
            update(h, _dot_nt(qs, kn_ref[0, :, hs]), vn_ref[0, :, hs])
            o = acc_sc[h] * (1.0 / l_sc[h])
            o_ref[0, :, hs] = _diff_epilogue(o, n, lamp_ref, subln_ref, lam_init).astype(o_ref.dtype)


def _attn_sample(q, cache_kt, cache_v, kb, vb, lamp, subln, lam_init, layer):
    b, n, _ = q.shape
    past = cache_kt.shape[3]
    ts = min(past, SAMPLE_TS)
    assert past % CHUNK == 0 and n <= CHUNK and past % ts == 0
    kern = functools.partial(_attn_sample_kernel, n=n, ts=ts, lam_init=lam_init)
    row_spec = pl.BlockSpec((1, n, Q_DIM), lambda bi, s: (bi, 0, 0))
    return pl.pallas_call(
        kern,
        grid=(b, past // ts),
        in_specs=[
            pl.BlockSpec((4, ATT_HEAD_DIM), lambda bi, s: (0, 0)),
            pl.BlockSpec((1, ATT_V_DIM), lambda bi, s: (0, 0)),
            row_spec,
            pl.BlockSpec((1, 1, Q_DIM, ts), lambda bi, s: (layer, bi, 0, s)),
            pl.BlockSpec((1, 1, ts * ATT_HEADS, ATT_V_DIM), lambda bi, s: (layer, bi, s, 0)),
            row_spec,
            row_spec,
        ],
        out_specs=row_spec,
        out_shape=jax.ShapeDtypeStruct((b, n, ATT_WIDTH), BF16),
        scratch_shapes=[pltpu.VMEM((ATT_HEADS, 2 * n, LANES), F32)] * 3,
        compiler_params=_cparams("arbitrary", "arbitrary"),
        name="attn_sample",
    )(lamp, subln, q, cache_kt, cache_v, kb, vb)


def _ssd_kernel(xbc_ref, z_ref, dt_ref, cbuf_ref, st0_ref, cw_ref, cb_ref, dtb_ref, alog_ref, dsk_ref,
                nrm_ref, expand_ref, y_ref, cout_ref, sout_ref, xpad_sc, st_sc, *, lc, valid_last):
    c = pl.program_id(1)
    nchunks = pl.num_programs(1)

    @pl.when(c == 0)
    def _():
        xpad_sc[0:CONV_PAD, :] = jnp.zeros((CONV_PAD, CONV_DIM), F32)
        xpad_sc[CONV_PAD - (CONV_WIDTH - 1):CONV_PAD, :] = cbuf_ref[0]
        for g in range(SSM_GROUPS):
            st_sc[g] = st0_ref[0, g].T

    xpad_sc[CONV_PAD:CONV_PAD + lc, :] = xbc_ref[0]
    xp = xpad_sc[...]
    conv = cw_ref[0:1, :] * xp
    for j in range(1, CONV_WIDTH):
        conv = cw_ref[j:j + 1, :] * xp + pltpu.roll(conv, 1, 0)
    conv = conv[CONV_PAD:CONV_PAD + lc] + cb_ref[...]
    act = conv * jax.nn.sigmoid(conv)

    @pl.when(c == nchunks - 1)
    def _():
        end = CONV_PAD + valid_last
        cout_ref[0] = xpad_sc[end - (CONV_WIDTH - 1):end, :]

    xpad_sc[0:CONV_PAD, :] = xpad_sc[lc:lc + CONV_PAD, :]

    dtr = dt_ref[0] + dtb_ref[...]
    dt = jnp.maximum(dtr, 0.0) + jnp.log1p(jnp.exp(-jnp.abs(dtr)))
    if valid_last < lc:
        row = lax.broadcasted_iota(jnp.int32, dt.shape, 0)
        dt = jnp.where(row < valid_last, dt, 0.0)
    a = dt * (-jnp.exp(alog_ref[...]))
    rr = lax.broadcasted_iota(jnp.int32, (lc, lc), 0)
    cc = lax.broadcasted_iota(jnp.int32, (lc, lc), 1)
    causal = rr >= cc
    tri = jnp.where(causal, 1.0, 0.0).astype(BF16)
    a_cs = sum(_dot(tri, part) for part in _split3(a))
    a_last = a_cs[lc - 1:lc, :]
    a_cs_t = a_cs.T

    expand = expand_ref[...]
    stack = jnp.concatenate([dt, dt * jnp.exp(a_last - a_cs), jnp.exp(a_cs)], axis=0)
    ex = _dot(stack.astype(BF16), expand)
    w_dt = ex[0:lc]
    w_state = ex[lc:2 * lc]
    w_off = ex[2 * lc:3 * lc]
    carry = jnp.broadcast_to(jnp.exp(a_last), (2 * SUBLANES, LANES))
    w_carry = sum(_dot(part, expand) for part in _split3(carry))[0:1]

    xs = act[:, :D_INNER]
    xdt = xs * w_dt
    xst = xs * w_state
    lane = lax.broadcasted_iota(jnp.int32, (lc, LANES), 1)
    lo_half = lane < SSM_HEAD_DIM

    for g in range(SSM_GROUPS):
        gs = slice(g * GROUP_WIDTH, (g + 1) * GROUP_WIDTH)
        bg = act[:, D_INNER + g * SSM_STATE:D_INNER + (g + 1) * SSM_STATE]
        cg = act[:, D_INNER + SSM_GROUPS * SSM_STATE + g * SSM_STATE:
                 D_INNER + SSM_GROUPS * SSM_STATE + (g + 1) * SSM_STATE]
        bgb = bg.astype(BF16)
        cgb = cg.astype(BF16)
        cb = _dot_nt(cgb, bgb)
        st_prev = st_sc[g]
        y_off = _dot(cgb, st_prev.astype(BF16)) * w_off[:, gs]
        st_sc[g] = w_carry[:, gs] * st_prev + _dot(bg.T.astype(BF16), xst[:, gs].astype(BF16))

        pairs = []
        for pr in range(SSM_HEADS_PER_GROUP // 2):
            x_pair = xdt[:, g * GROUP_WIDTH + pr * LANES:g * GROUP_WIDTH + (pr + 1) * LANES]
            y_pair = None
            for half in range(2):
                h = g * SSM_HEADS_PER_GROUP + pr * 2 + half
                seg = a_cs[:, h:h + 1] - a_cs_t[h:h + 1, :]
                decay = jnp.exp(jnp.where(causal, seg, -jnp.inf))
                mix = (cb * decay).astype(BF16)
                keep = lo_half if half == 0 else jnp.logical_not(lo_half)
                contrib = _dot(mix, jnp.where(keep, x_pair, 0.0).astype(BF16))
                y_pair = contrib if y_pair is None else y_pair + contrib
            pairs.append(y_pair)
        y_g = jnp.concatenate(pairs, axis=1) + y_off + dsk_ref[:, gs] * xs[:, gs]

        zg = z_ref[0, :, gs]
        yz = y_g * (zg * jax.nn.sigmoid(zg))
        y_ref[0, :, gs] = _rms(yz, nrm_ref[:, gs], SSM_NORM_EPS).astype(y_ref.dtype)

    @pl.when(c == nchunks - 1)
    def _():
        for g in range(SSM_GROUPS):
            sout_ref[0, g] = st_sc[g].T


def _ssd(xbc, z, dt, conv_buf, state0, p, lc, valid_last):
    b, l, _ = xbc.shape
    assert l % lc == 0 and CONV_WIDTH - 1 <= valid_last <= lc
    assert valid_last == lc or l == lc
    kern = functools.partial(_ssd_kernel, lc=lc, valid_last=valid_last)
    return pl.pallas_call(
        kern,
        grid=(b, l // lc),
        in_specs=[
            pl.BlockSpec((1, lc, CONV_DIM), lambda bi, c: (bi, c, 0)),
            pl.BlockSpec((1, lc, D_INNER), lambda bi, c: (bi, c, 0)),
            pl.BlockSpec((1, lc, LANES), lambda bi, c: (bi, c, 0)),
            pl.BlockSpec((1, CONV_WIDTH - 1, CONV_DIM), lambda bi, c: (bi, 0, 0)),
            pl.BlockSpec((1, SSM_GROUPS, GROUP_WIDTH, SSM_STATE), lambda bi, c: (bi, 0, 0, 0)),
            pl.BlockSpec((CONV_WIDTH, CONV_DIM), lambda bi, c: (0, 0)),
            pl.BlockSpec((1, CONV_DIM), lambda bi, c: (0, 0)),
            pl.BlockSpec((1, LANES), lambda bi, c: (0, 0)),
            pl.BlockSpec((1, LANES), lambda bi, c: (0, 0)),
            pl.BlockSpec((1, D_INNER), lambda bi, c: (0, 0)),
            pl.BlockSpec((1, D_INNER), lambda bi, c: (0, 0)),
            pl.BlockSpec((LANES, D_INNER), lambda bi, c: (0, 0)),
        ],
        out_specs=(
            pl.BlockSpec((1, lc, D_INNER), lambda bi, c: (bi, c, 0)),
            pl.BlockSpec((1, CONV_WIDTH - 1, CONV_DIM), lambda bi, c: (bi, 0, 0)),
            pl.BlockSpec((1, SSM_GROUPS, GROUP_WIDTH, SSM_STATE), lambda bi, c: (bi, 0, 0, 0)),
        ),
        out_shape=(
            jax.ShapeDtypeStruct((b, l, D_INNER), BF16),
            jax.ShapeDtypeStruct((b, CONV_WIDTH - 1, CONV_DIM), F32),
            jax.ShapeDtypeStruct((b, SSM_GROUPS, GROUP_WIDTH, SSM_STATE), F32),
        ),
        scratch_shapes=[pltpu.VMEM((lc + CONV_PAD, CONV_DIM), F32),
                        pltpu.VMEM((SSM_GROUPS, SSM_STATE, GROUP_WIDTH), F32)],
        compiler_params=_cparams("arbitrary", "arbitrary"),
        name="ssd",
    )(xbc, z, dt, conv_buf, state0, p["conv_w"], p["conv_b"], p["dt_bias"], p["a_log"], p["d_skip"],
      p["ssm_norm"], p["expand"])


def _merge_kernel(x_ref, att_ref, y_ref, g_ref, wa_ref, ws_ref, wo_ref, o_ref):
    ba = _dot(att_ref[...], wa_ref[...])
    bs = _dot(y_ref[...], ws_ref[...])
    g = jax.nn.sigmoid(g_ref[...])
    merged = g[:, :D_MODEL] * ba + g[:, D_MODEL:] * bs
    o_ref[...] = x_ref[...] + _dot(merged.astype(BF16), wo_ref[...])


def _merge(x, att, y, gates, wa, ws, wo, tm):
    t = x.shape[0]
    return pl.pallas_call(
        _merge_kernel,
        grid=(t // tm,),
        in_specs=[
            pl.BlockSpec((tm, D_MODEL), lambda i: (i, 0)),
            pl.BlockSpec((tm, ATT_WIDTH), lambda i: (i, 0)),
            pl.BlockSpec((tm, D_INNER), lambda i: (i, 0)),
            pl.BlockSpec((tm, 2 * D_MODEL), lambda i: (i, 0)),
            _const_spec((ATT_WIDTH, D_MODEL)),
            _const_spec((D_INNER, D_MODEL)),
            _const_spec((D_MODEL, D_MODEL)),
        ],
        out_specs=pl.BlockSpec((tm, D_MODEL), lambda i: (i, 0)),
        out_shape=jax.ShapeDtypeStruct((t, D_MODEL), F32),
        compiler_params=_cparams("arbitrary"),
        name="merge",
    )(x, att, y, gates, wa, ws, wo)


FFN_SPLIT = 2
FFN_TF = D_FF // FFN_SPLIT


def _ffn_kernel(x_ref, nw_ref, wgu_ref, wd_ref, nf_ref, *out_refs, final_norm):
    x = x_ref[...]
    hb = _rms(x, nw_ref[...], RMS_EPS).astype(BF16)
    acc = x
    for c in range(FFN_SPLIT):
        gt = _dot(hb, wgu_ref[:, c * FFN_TF:(c + 1) * FFN_TF])
        up = _dot(hb, wgu_ref[:, D_FF + c * FFN_TF:D_FF + (c + 1) * FFN_TF])
        act = (gt * jax.nn.sigmoid(gt) * up).astype(BF16)
        acc = acc + _dot(act, wd_ref[c * FFN_TF:(c + 1) * FFN_TF, :])
    out_refs[0][...] = acc
    if final_norm:
        out_refs[1][...] = _rms(acc, nf_ref[...], RMS_EPS)


def _ffn(x, norm_w, wgu, wd, norm_final, final_norm, tm):
    t = x.shape[0]
    row_spec = pl.BlockSpec((tm, D_MODEL), lambda i: (i, 0))
    n_out = 2 if final_norm else 1
    return pl.pallas_call(
        functools.partial(_ffn_kernel, final_norm=final_norm),
        grid=(t // tm,),
        in_specs=[
            row_spec,
            pl.BlockSpec((1, D_MODEL), lambda i: (0, 0)),
            _const_spec((D_MODEL, 2 * D_FF)),
            _const_spec((D_FF, D_MODEL)),
            pl.BlockSpec((1, D_MODEL), lambda i: (0, 0)),
        ],
        out_specs=(row_spec,) * n_out,
        out_shape=(jax.ShapeDtypeStruct((t, D_MODEL), F32),) * n_out,
        compiler_params=_cparams("arbitrary"),
        name="ffn",
    )(x, norm_w, wgu, wd, norm_final)


def _layer_params(l, norm_mix, w_in, lambda_q1, lambda_k1, lambda_q2, lambda_k2, attn_subln, conv_w, conv_b,
                  dt_bias, a_log, d_skip, ssm_norm, w_branch_att, w_branch_ssd, w_out, norm_ffn, w_gate_up,
                  w_down):
    w = w_in[l]
    n_main = Q_DIM + Q_DIM + ATT_WIDTH + D_INNER + CONV_DIM
    scale = ATT_HEAD_DIM ** -0.5
    w_main = jnp.concatenate([(w[:, :Q_DIM] * scale).astype(BF16), w[:, Q_DIM:n_main].astype(BF16),
                              w[:, n_main + SSM_HEADS:].astype(BF16)], axis=1)
    w_dt = jnp.pad(w[:, n_main:n_main + SSM_HEADS], ((0, 0), (0, LANES - SSM_HEADS))).astype(BF16)
    head_of_channel = jnp.arange(D_INNER) // SSM_HEAD_DIM
    return {
        "norm_mix": norm_mix[l][None], "w_main": w_main, "w_dt": w_dt,
        "lamp": jnp.stack([lambda_q1[l], lambda_k1[l], lambda_q2[l], lambda_k2[l]]),
        "subln": attn_subln[l][None],
        "conv_w": conv_w[l], "conv_b": conv_b[l][None],
        "dt_bias": jnp.pad(dt_bias[l], (0, LANES - SSM_HEADS))[None],
        "a_log": jnp.pad(a_log[l], (0, LANES - SSM_HEADS))[None],
        "d_skip": jnp.repeat(d_skip[l], SSM_HEAD_DIM)[None],
        "ssm_norm": ssm_norm[l][None],
        "expand": (jnp.arange(LANES)[:, None] == head_of_channel[None, :]).astype(BF16),
        "wa": w_branch_att[l].astype(BF16), "ws": w_branch_ssd[l].astype(BF16), "wo": w_out[l].astype(BF16),
        "norm_ffn": norm_ffn[l][None], "wgu": w_gate_up[l].astype(BF16), "wd": w_down[l].astype(BF16),
    }


def _pad_time(x, b, n, lc):
    return jnp.pad(x.reshape(b, n, x.shape[-1]), ((0, 0), (0, lc - n), (0, 0)))


def kernel(x_prompt, x_sample, cache_k, cache_v, state_conv, state_ssm, norm_mix, w_in, lambda_q1, lambda_k1,
           lambda_q2, lambda_k2, attn_subln, conv_w, conv_b, dt_bias, a_log, d_skip, ssm_norm, w_branch_att,
           w_branch_ssd, w_out, norm_ffn, w_gate_up, w_down, norm_final):
    bp, lp, _ = x_prompt.shape
    bs, ls, _ = x_sample.shape
    assert bp == 1
    past = cache_k.shape[2]
    tp, ts = bp * lp, bs * ls
    xp = x_prompt.reshape(tp, D_MODEL)
    xs = x_sample.reshape(ts, D_MODEL)
    cache_k = jnp.transpose(cache_k, (0, 1, 3, 4, 5, 2)).reshape(DEPTH, bs, Q_DIM, past)
    cache_v = cache_v.reshape(DEPTH, bs, past * ATT_HEADS, ATT_V_DIM)
    state_ssm = state_ssm.reshape(DEPTH, bs, SSM_GROUPS, GROUP_WIDTH, SSM_STATE)
    conv0 = jnp.zeros((bp, CONV_WIDTH - 1, CONV_DIM), F32)
    ssm0 = jnp.zeros((bp, SSM_GROUPS, GROUP_WIDTH, SSM_STATE), F32)
    nf = norm_final[None]

    tm_p = 1024 if tp % 1024 == 0 else 128
    tm_row = 512 if tp % 512 == 0 else 128
    tq = ATTN_TQ if lp % ATTN_TQ == 0 else CHUNK
    tk = ATTN_TK if tq % ATTN_TK == 0 else tq

    outs_p = [[] for _ in range(4)]
    outs_s = [[] for _ in range(4)]
    yp = ys = None
    for l in range(DEPTH):
        p = _layer_params(l, norm_mix, w_in, lambda_q1, lambda_k1, lambda_q2, lambda_k2, attn_subln, conv_w,
                          conv_b, dt_bias, a_log, d_skip, ssm_norm, w_branch_att, w_branch_ssd, w_out,
                          norm_ffn, w_gate_up, w_down)
        lam_init = 0.8 - 0.6 * math.exp(-0.3 * l)
        last = l == DEPTH - 1

        q, k, kb, v, vb, z, xbc, gates, dt = _in_proj(xp, p["norm_mix"], p["w_main"], p["w_dt"], tm_p)
        att = _attn_prompt(q, kb, vb, p["lamp"], p["subln"], lam_init, tq, tk, ATTN_UNROLL)
        y, c_new, s_new = _ssd(xbc.reshape(bp, lp, CONV_DIM), z.reshape(bp, lp, D_INNER),
                               dt.reshape(bp, lp, LANES), conv0, ssm0, p, SSD_CHUNK, SSD_CHUNK)
        x1 = _merge(xp, att, y.reshape(tp, D_INNER), gates, p["wa"], p["ws"], p["wo"], tm_row)
        res = _ffn(x1, p["norm_ffn"], p["wgu"], p["wd"], nf, last, tm_row)
        xp = res[0]
        if last:
            yp = res[1]
        outs_p[0].append(k.reshape(bp, lp, ATT_HEADS, 2, ATT_HEAD_DIM))
        outs_p[1].append(v.reshape(bp, lp, ATT_HEADS, ATT_V_DIM))
        outs_p[2].append(c_new)
        outs_p[3].append(s_new.reshape(bp, SSM_GROUPS, SSM_HEADS_PER_GROUP, SSM_HEAD_DIM, SSM_STATE))

        q, k, kb, v, vb, z, xbc, gates, dt = _in_proj(xs, p["norm_mix"], p["w_main"], p["w_dt"], ts)
        att = _attn_sample(q.reshape(bs, ls, Q_DIM), cache_k, cache_v, kb.reshape(bs, ls, Q_DIM),
                           vb.reshape(bs, ls, ATT_WIDTH), p["lamp"], p["subln"], lam_init, l)
        y, c_new, s_new = _ssd(_pad_time(xbc, bs, ls, SSD_PAD_ROWS), _pad_time(z, bs, ls, SSD_PAD_ROWS),
                               _pad_time(dt, bs, ls, SSD_PAD_ROWS), state_conv[l], state_ssm[l], p,
                               SSD_PAD_ROWS, ls)
        x1 = _merge(xs, att.reshape(ts, ATT_WIDTH), y[:, :ls].reshape(ts, D_INNER), gates,
                    p["wa"], p["ws"], p["wo"], ts)
        res = _ffn(x1, p["norm_ffn"], p["wgu"], p["wd"], nf, last, ts)
        xs = res[0]
        if last:
            ys = res[1]
        outs_s[0].append(k.reshape(bs, ls, ATT_HEADS, 2, ATT_HEAD_DIM))
        outs_s[1].append(v.reshape(bs, ls, ATT_HEADS, ATT_V_DIM))
        outs_s[2].append(c_new)
        outs_s[3].append(s_new.reshape(bs, SSM_GROUPS, SSM_HEADS_PER_GROUP, SSM_HEAD_DIM, SSM_STATE))

    return (yp.reshape(bp, lp, D_MODEL), ys.reshape(bs, ls, D_MODEL),
            *(jnp.stack(o) for o in outs_p), *(jnp.stack(o) for o in outs_s))
```

```python
import functools
import math

import jax
import jax.numpy as jnp
from jax import lax
from jax.experimental import pallas as pl
from jax.experimental.pallas import tpu as pltpu

F32 = jnp.float32
BF16 = jnp.bfloat16
LOG2E = math.log2(math.e)

D_MODEL = 1024
DEPTH = 2
CHUNK = 64
ATT_HEADS = 8
ATT_HEAD_DIM = 64
ATT_V_DIM = 128
ATT_WIDTH = 1024
Q_DIM = 1024
ATT_SUBLN_EPS = 1e-5
D_INNER = 2048
SSM_HEAD_DIM = 64
SSM_HEADS = 32
SSM_GROUPS = 8
SSM_HEADS_PER_GROUP = 4
SSM_STATE = 128
CONV_WIDTH = 4
CONV_DIM = 4096
SSM_NORM_EPS = 1e-5
D_FF = 2816
RMS_EPS = 1e-6
GROUP_WIDTH = SSM_HEADS_PER_GROUP * SSM_HEAD_DIM

LANES = 128
SUBLANES = 8
VMEM_LIMIT_BYTES = 56 * 1024 * 1024

PROJ_TN = 512
SSD_CHUNK = 256
SSD_PAD_ROWS = 128
CONV_PAD = 8
SAMPLE_TS = 1024
ATTN_TQ = 1024
ATTN_TK = 512
ATTN_UNROLL = 2


def _cparams(*sem):
    return pltpu.CompilerParams(dimension_semantics=sem, vmem_limit_bytes=VMEM_LIMIT_BYTES)


def _const_spec(shape):
    return pl.BlockSpec(shape, lambda *_: (0,) * len(shape), pipeline_mode=pl.Buffered(1))


def _rms(x, w, eps):
    return x * lax.rsqrt(jnp.mean(x * x, axis=-1, keepdims=True) + eps) * w


def _split3(x):
    hi = x.astype(BF16)
    r1 = x - hi.astype(F32)
    mid = r1.astype(BF16)
    lo = (r1 - mid.astype(F32)).astype(BF16)
    return hi, mid, lo


def _dot(a, b):
    return jnp.dot(a, b, preferred_element_type=F32)


def _dot_nt(a, b):
    return lax.dot_general(a, b, (((1,), (1,)), ((), ())), preferred_element_type=F32)


_NQ = Q_DIM // PROJ_TN
_NK = Q_DIM // PROJ_TN
_NV = ATT_WIDTH // PROJ_TN
_NZ = D_INNER // PROJ_TN
_NX = CONV_DIM // PROJ_TN
_NG = 2 * D_MODEL // PROJ_TN
_OFF_K = _NQ
_OFF_V = _OFF_K + _NK
_OFF_Z = _OFF_V + _NV
_OFF_X = _OFF_Z + _NZ
_OFF_G = _OFF_X + _NX
_N_COL_TILES = _OFF_G + _NG


def _in_proj_kernel(*refs, tm, fuse_conv, n_alias):
    x_ref, nw_ref, w_ref, wdt_ref = refs[:4]
    pos = 4
    if fuse_conv:
        cw_ref, cb_ref, hist0_ref = refs[pos:pos + 3]
        pos += 3
    pos += n_alias
    q_ref, k_ref, kb_ref, v_ref, vb_ref, z_ref, xbc_ref, g_ref, dt_ref = refs[pos:pos + 9]
    pos += 9
    if fuse_conv:
        tail_ref = refs[pos]
        pos += 1
    h_sc = refs[pos]
    i = pl.program_id(0)
    j = pl.program_id(1)

    @pl.when(j == 0)
    def _():
        hb = _rms(x_ref[...], nw_ref[...], RMS_EPS).astype(BF16)
        h_sc[...] = hb
        dt_ref[...] = _dot(hb, wdt_ref[...])

    def tile():
        return _dot(h_sc[...], w_ref[...])

    @pl.when(j < _OFF_K)
    def _():
        q_ref[...] = (tile() * LOG2E).astype(BF16)

    @pl.when((j >= _OFF_K) & (j < _OFF_V))
    def _():
        res = tile()
        k_ref[...] = res
        kb_ref[...] = res.astype(BF16)

    @pl.when((j >= _OFF_V) & (j < _OFF_Z))
    def _():
        res = tile()
        v_ref[...] = res
        vb_ref[...] = res.astype(BF16)

    @pl.when((j >= _OFF_Z) & (j < _OFF_X))
    def _():
        z_ref[...] = tile().astype(z_ref.dtype)

    @pl.when((j >= _OFF_X) & (j < _OFF_G))
    def _():
        res = tile()
        if not fuse_conv:
            xbc_ref[...] = res
        else:
            hist_sc = refs[pos + 1]
            jx = j - _OFF_X

            @pl.when(i == 0)
            def _():
                hist_sc[jx] = hist0_ref[...]

            xp = jnp.concatenate([hist_sc[jx], res], axis=0)
            conv = cw_ref[0:1, :] * xp
            for tap in range(1, CONV_WIDTH):
                conv = cw_ref[tap:tap + 1, :] * xp + pltpu.roll(conv, 1, 0)
            conv = conv[CONV_PAD:] + cb_ref[...]
            xbc_ref[...] = (conv * jax.nn.sigmoid(conv)).astype(xbc_ref.dtype)
            last_rows = res[tm - CONV_PAD:, :]
            hist_sc[jx] = last_rows
            tail_ref[...] = last_rows

    @pl.when(j >= _OFF_G)
    def _():
        g_ref[...] = tile().astype(g_ref.dtype)


def _in_proj(x, norm_w, w_main, w_dt, tm, conv=None, stacks=None, layer=0):
    t = x.shape[0]
    fuse_conv = conv is not None

    def col(off, n):
        return lambda i, j: (i, jnp.clip(j - off, 0, n - 1))

    def out_spec(off, n):
        return pl.BlockSpec((tm, PROJ_TN), col(off, n))

    def stack_spec(off, n):
        return pl.BlockSpec((None, tm, PROJ_TN), lambda i, j: (layer,) + col(off, n)(i, j))

    def xcol(i, j):
        return (0, jnp.clip(j - _OFF_X, 0, _NX - 1))

    in_specs = [
        pl.BlockSpec((tm, D_MODEL), lambda i, j: (i, 0)),
        pl.BlockSpec((1, D_MODEL), lambda i, j: (0, 0)),
        pl.BlockSpec((D_MODEL, PROJ_TN), lambda i, j: (0, j)),
        pl.BlockSpec((D_MODEL, LANES), lambda i, j: (0, 0)),
    ]
    args = [x, norm_w, w_main, w_dt]
    if fuse_conv:
        in_specs += [pl.BlockSpec((CONV_WIDTH, PROJ_TN), xcol), pl.BlockSpec((1, PROJ_TN), xcol),
                     pl.BlockSpec((CONV_PAD, PROJ_TN), xcol)]
        args += list(conv)
    aliases = {}
    if stacks is not None:
        in_specs += [pl.BlockSpec(memory_space=pl.ANY)] * 2
        aliases = {len(args): 1, len(args) + 1: 3}
        args += list(stacks)
    out_shape = [
        jax.ShapeDtypeStruct((t, Q_DIM), BF16),
        jax.ShapeDtypeStruct((DEPTH, t, Q_DIM), F32),
        jax.ShapeDtypeStruct((t, Q_DIM), BF16),
        jax.ShapeDtypeStruct((DEPTH, t, ATT_WIDTH), F32),
        jax.ShapeDtypeStruct((t, ATT_WIDTH), BF16),
        jax.ShapeDtypeStruct((t, D_INNER), BF16),
        jax.ShapeDtypeStruct((t, CONV_DIM), BF16 if fuse_conv else F32),
        jax.ShapeDtypeStruct((t, 2 * D_MODEL), BF16),
        jax.ShapeDtypeStruct((t, LANES), F32),
    ]
    out_specs = [
        out_spec(0, _NQ), stack_spec(_OFF_K, _NK), out_spec(_OFF_K, _NK),
        stack_spec(_OFF_V, _NV), out_spec(_OFF_V, _NV), out_spec(_OFF_Z, _NZ),
        out_spec(_OFF_X, _NX), out_spec(_OFF_G, _NG),
        pl.BlockSpec((tm, LANES), lambda i, j: (i, 0)),
    ]
    scratch = [pltpu.VMEM((tm, D_MODEL), BF16)]
    if fuse_conv:
        out_shape.append(jax.ShapeDtypeStruct((t // tm, CONV_PAD, CONV_DIM), F32))
        out_specs.append(pl.BlockSpec((None, CONV_PAD, PROJ_TN), lambda i, j: (i,) + xcol(i, j)))
        scratch.append(pltpu.VMEM((_NX, CONV_PAD, PROJ_TN), F32))
    return pl.pallas_call(
        functools.partial(_in_proj_kernel, tm=tm, fuse_conv=fuse_conv, n_alias=len(aliases)),
        grid=(t // tm, _N_COL_TILES),
        in_specs=in_specs,
        out_specs=out_specs,
        out_shape=out_shape,
        scratch_shapes=scratch,
        input_output_aliases=aliases,
        compiler_params=_cparams("arbitrary", "arbitrary"),
        name="in_proj",
    )(*args)


def _stack_q(q):
    lane = lax.broadcasted_iota(jnp.int32, q.shape, 1)
    zero = jnp.zeros_like(q)
    return jnp.concatenate([jnp.where(lane < ATT_HEAD_DIM, q, zero),
                            jnp.where(lane >= ATT_HEAD_DIM, q, zero)], axis=0)


def _diff_epilogue(o, n, lamp_ref, subln_ref, lam_init):
    lp = lamp_ref[...]
    lam = (jnp.exp(jnp.sum(lp[0:1] * lp[1:2], axis=-1, keepdims=True))
           - jnp.exp(jnp.sum(lp[2:3] * lp[3:4], axis=-1, keepdims=True)) + lam_init)
    a = o[:n] - lam * o[n:]
    return _rms(a, subln_ref[...], ATT_SUBLN_EPS) * (1.0 - lam_init)


def _attn_prompt_kernel(lamp_ref, subln_ref, q_ref, k_ref, v_ref, o_ref, m_sc, l_sc, acc_sc, s_sc, *,
                        tq, tk, nq, unroll, lam_init):
    qi = pl.program_id(1)
    qs = _stack_q(q_ref[...])
    m_sc[...] = jnp.full(m_sc.shape, -jnp.inf, F32)
    l_sc[...] = jnp.zeros(l_sc.shape, F32)
    acc_sc[...] = jnp.zeros(acc_sc.shape, F32)
    n_lane_tiles = tk // LANES
    full_blocks = qi * (tq // tk)

    def scores(kblk, masked):
        start = pl.multiple_of(kblk * tk, tk)
        s = _dot_nt(qs, k_ref[pl.ds(start, tk), :])
        if masked:
            r = lax.broadcasted_iota(jnp.int32, s.shape, 0)
            c = lax.broadcasted_iota(jnp.int32, s.shape, 1)
            q_chunk = jnp.where(r >= tq, r - tq, r) // CHUNK
            k_chunk = c // CHUNK + (kblk - full_blocks) * (tk // CHUNK)
            s = jnp.where(k_chunk <= q_chunk, s, -jnp.inf)
        return s

    def absorb(kblk, s, state):
        m_prev, l_prev, acc_prev = state
        vb = v_ref[pl.ds(pl.multiple_of(kblk * tk, tk), tk), :]
        m_new = jnp.maximum(m_prev, jnp.max(s, axis=-1, keepdims=True))
        alpha = jnp.exp2(m_prev - m_new)
        p = jnp.exp2(s - jnp.concatenate([m_new] * n_lane_tiles, axis=1))
        pv = _dot(p.astype(BF16), jnp.concatenate([vb, jnp.ones((tk, LANES), BF16)], axis=1))
        return m_new, alpha * l_prev + pv[:, ATT_V_DIM:], alpha * acc_prev + pv[:, :ATT_V_DIM]

    PLAIN, MASKED = "plain", "masked"

    def pipeline(first, produce):
        state = (m_sc[...], l_sc[...], acc_sc[...])
        s = s_sc[...]
        for i, kind in enumerate(produce):
            s_next = None if kind is None else scores(first + i + 1, kind == MASKED)
            state = absorb(first + i, s, state)
            s = s_next
        if s is not None:
            s_sc[...] = s
        m_sc[...], l_sc[...], acc_sc[...] = state

    diag_blocks = tq // tk

    @pl.when(qi == 0)
    def _():
        s_sc[...] = scores(0, True)
        pipeline(0, [MASKED] * (diag_blocks - 1) + [None])

    @pl.when(qi > 0)
    def _():
        s_sc[...] = scores(0, False)

    def body(t, carry):
        pipeline(unroll * t, [PLAIN] * unroll)
        return carry

    plain_steps = jnp.maximum(full_blocks - 1, 0)
    lax.fori_loop(0, plain_steps // unroll, body, 0)
    for r in sorted({(diag_blocks * q - 1) % unroll for q in range(1, nq)}):
        @pl.when((qi > 0) & (plain_steps % unroll == r))
        def _():
            pipeline(full_blocks - 1 - r, [PLAIN] * r + [MASKED] * diag_blocks + [None])

    o = acc_sc[...] * (1.0 / l_sc[...])
    o_ref[...] = _diff_epilogue(o, tq, lamp_ref, subln_ref, lam_init).astype(o_ref.dtype)


def _attn_prompt(q, kb, vb, lamp, subln, lam_init, tq, tk, unroll):
    t = q.shape[0]
    assert t % tq == 0 and tq % tk == 0 and tk % LANES == 0 and tk % CHUNK == 0
    kern = functools.partial(_attn_prompt_kernel, tq=tq, tk=tk, nq=t // tq, unroll=unroll, lam_init=lam_init)
    return pl.pallas_call(
        kern,
        grid=(ATT_HEADS, t // tq),
        in_specs=[
            pl.BlockSpec((4, ATT_HEAD_DIM), lambda h, i: (0, 0)),
            pl.BlockSpec((1, ATT_V_DIM), lambda h, i: (0, 0)),
            pl.BlockSpec((tq, LANES), lambda h, i: (i, h)),
            pl.BlockSpec((t, LANES), lambda h, i: (0, h)),
            pl.BlockSpec((t, LANES), lambda h, i: (0, h)),
        ],
        out_specs=pl.BlockSpec((tq, LANES), lambda h, i: (i, h)),
        out_shape=jax.ShapeDtypeStruct((t, ATT_WIDTH), BF16),
        scratch_shapes=[pltpu.VMEM((2 * tq, LANES), F32), pltpu.VMEM((2 * tq, LANES), F32),
                        pltpu.VMEM((2 * tq, ATT_V_DIM), F32), pltpu.VMEM((2 * tq, tk), F32)],
        compiler_params=_cparams("arbitrary", "arbitrary"),
        name="attn_prompt",
    )(lamp, subln, q, kb, vb)


def _attn_sample_kernel(lamp_ref, subln_ref, q_ref, kt_ref, vc_ref, kn_ref, vn_ref, o_ref, m_sc, l_sc, acc_sc, *,
                        n, ts, lam_init):
    piece = pl.program_id(1)

    @pl.when(piece == 0)
    def _():
        m_sc[...] = jnp.full(m_sc.shape, -jnp.inf, F32)
        l_sc[...] = jnp.zeros(l_sc.shape, F32)
        acc_sc[...] = jnp.zeros(acc_sc.shape, F32)

    def update(h, s, v):
        width = s.shape[1]
        m_prev = m_sc[h]
        m_new = jnp.maximum(m_prev, jnp.max(s, axis=-1, keepdims=True))
        alpha = jnp.exp2(m_prev - m_new)
        m_wide = m_new[:, :width] if width <= LANES else jnp.concatenate([m_new] * (width // LANES), axis=1)
        p = jnp.exp2(s - m_wide)
        l_sc[h] = alpha * l_sc[h] + jnp.sum(p, axis=-1, keepdims=True)
        acc_sc[h] = alpha * acc_sc[h] + _dot(p.astype(BF16), v)
        m_sc[h] = m_new

    for h in range(ATT_HEADS):
        hs = slice(h * LANES, (h + 1) * LANES)
        qs = _stack_q(q_ref[0, :, hs])
        s = _dot(qs, kt_ref[0, 0, hs, :].astype(BF16))
        update(h, s, vc_ref[0, 0, pl.ds(h, ts, stride=ATT_HEADS), :].astype(BF16))

    @pl.when(piece == pl.num_programs(1) - 1)
    def _():
        for h in range(ATT_HEADS):
            hs = slice(h * LANES, (h + 1) * LANES)
            qs = _stack_q(q_ref[0, :, hs])
            update(h, _dot_nt(qs, kn_ref[0, :, hs]), vn_ref[0, :, hs])
            o = acc_sc[h] * (1.0 / l_sc[h])
            o_ref[0, :, hs] = _diff_epilogue(o, n, lamp_ref, subln_ref, lam_init).astype(o_ref.dtype)


def _attn_sample(q, cache_kt, cache_v, kb, vb, lamp, subln, lam_init, layer):
    b, n, _ = q.shape
    past = cache_kt.shape[3]
    ts = min(past, SAMPLE_TS)
    assert past % CHUNK == 0 and n <= CHUNK and past % ts == 0
    kern = functools.partial(_attn_sample_kernel, n=n, ts=ts, lam_init=lam_init)
    row_spec = pl.BlockSpec((1, n, Q_DIM), lambda bi, s: (bi, 0, 0))
    return pl.pallas_call(
        kern,
        grid=(b, past // ts),
        in_specs=[
            pl.BlockSpec((4, ATT_HEAD_DIM), lambda bi, s: (0, 0)),
            pl.BlockSpec((1, ATT_V_DIM), lambda bi, s: (0, 0)),
            row_spec,
            pl.BlockSpec((1, 1, Q_DIM, ts), lambda bi, s: (layer, bi, 0, s)),
            pl.BlockSpec((1, 1, ts * ATT_HEADS, ATT_V_DIM), lambda bi, s: (layer, bi, s, 0)),
            row_spec,
            row_spec,
        ],
        out_specs=row_spec,
        out_shape=jax.ShapeDtypeStruct((b, n, ATT_WIDTH), BF16),
        scratch_shapes=[pltpu.VMEM((ATT_HEADS, 2 * n, LANES), F32)] * 3,
        compiler_params=_cparams("arbitrary", "arbitrary"),
        name="attn_sample",
    )(lamp, subln, q, cache_kt, cache_v, kb, vb)


def _ssd_kernel(*refs, lc, valid_last, conv_done):
    if conv_done:
        (xbc_ref, z_ref, dt_ref, st0_ref, dtb_ref, alog_ref, dsk_ref, nrm_ref, expand_ref,
         y_ref, sout_ref, st_sc) = refs
    else:
        (xbc_ref, z_ref, dt_ref, st0_ref, dtb_ref, alog_ref, dsk_ref, nrm_ref, expand_ref,
         cbuf_ref, cw_ref, cb_ref, y_ref, sout_ref, cout_ref, st_sc, xpad_sc) = refs
    c = pl.program_id(1)
    nchunks = pl.num_programs(1)

    @pl.when(c == 0)
    def _():
        for g in range(SSM_GROUPS):
            st_sc[g] = st0_ref[0, g].T

    if conv_done:
        act = xbc_ref[0]
    else:
        @pl.when(c == 0)
        def _():
            xpad_sc[0:CONV_PAD, :] = jnp.zeros((CONV_PAD, CONV_DIM), F32)
            xpad_sc[CONV_PAD - (CONV_WIDTH - 1):CONV_PAD, :] = cbuf_ref[0]

        xpad_sc[CONV_PAD:CONV_PAD + lc, :] = xbc_ref[0]
        xp = xpad_sc[...]
        conv = cw_ref[0:1, :] * xp
        for j in range(1, CONV_WIDTH):
            conv = cw_ref[j:j + 1, :] * xp + pltpu.roll(conv, 1, 0)
        conv = conv[CONV_PAD:CONV_PAD + lc] + cb_ref[...]
        act = conv * jax.nn.sigmoid(conv)

        @pl.when(c == nchunks - 1)
        def _():
            end = CONV_PAD + valid_last
            cout_ref[0] = xpad_sc[end - (CONV_WIDTH - 1):end, :]

        xpad_sc[0:CONV_PAD, :] = xpad_sc[lc:lc + CONV_PAD, :]

    dtr = dt_ref[0] + dtb_ref[...]
    dt = jnp.maximum(dtr, 0.0) + jnp.log1p(jnp.exp(-jnp.abs(dtr)))
    if valid_last < lc:
        row = lax.broadcasted_iota(jnp.int32, dt.shape, 0)
        dt = jnp.where(row < valid_last, dt, 0.0)
    a = dt * (-jnp.exp(alog_ref[...]))
    rr = lax.broadcasted_iota(jnp.int32, (lc, lc), 0)
    cc = lax.broadcasted_iota(jnp.int32, (lc, lc), 1)
    causal = rr >= cc
    tri = jnp.where(causal, 1.0, 0.0).astype(BF16)
    a_cs = sum(_dot(tri, part) for part in _split3(a))
    a_last = a_cs[lc - 1:lc, :]
    a_cs_t = a_cs.T

    expand = expand_ref[...]
    stack = jnp.concatenate([dt, dt * jnp.exp(a_last - a_cs), jnp.exp(a_cs)], axis=0)
    ex = _dot(stack.astype(BF16), expand)
    w_dt = ex[0:lc]
    w_state = ex[lc:2 * lc]
    w_off = ex[2 * lc:3 * lc]
    carry = jnp.broadcast_to(jnp.exp(a_last), (2 * SUBLANES, LANES))
    w_carry = sum(_dot(part, expand) for part in _split3(carry))[0:1]

    xs = act[:, :D_INNER].astype(F32)
    xdt = xs * w_dt
    xst = xs * w_state
    lane = lax.broadcasted_iota(jnp.int32, (lc, LANES), 1)
    lo_half = lane < SSM_HEAD_DIM

    for g in range(SSM_GROUPS):
        gs = slice(g * GROUP_WIDTH, (g + 1) * GROUP_WIDTH)
        bg = act[:, D_INNER + g * SSM_STATE:D_INNER + (g + 1) * SSM_STATE]
        cg = act[:, D_INNER + SSM_GROUPS * SSM_STATE + g * SSM_STATE:
                 D_INNER + SSM_GROUPS * SSM_STATE + (g + 1) * SSM_STATE]
        bgb = bg.astype(BF16)
        cgb = cg.astype(BF16)
        cb = _dot_nt(cgb, bgb)
        st_prev = st_sc[g]
        y_off = _dot(cgb, st_prev.astype(BF16)) * w_off[:, gs]
        st_sc[g] = w_carry[:, gs] * st_prev + _dot(bg.astype(F32).T.astype(BF16), xst[:, gs].astype(BF16))

        pairs = []
        for pr in range(SSM_HEADS_PER_GROUP // 2):
            x_pair = xdt[:, g * GROUP_WIDTH + pr * LANES:g * GROUP_WIDTH + (pr + 1) * LANES]
            y_pair = None
            for half in range(2):
                h = g * SSM_HEADS_PER_GROUP + pr * 2 + half
                seg = a_cs[:, h:h + 1] - a_cs_t[h:h + 1, :]
                decay = jnp.exp(jnp.where(causal, seg, -jnp.inf))
                mix = (cb * decay).astype(BF16)
                keep = lo_half if half == 0 else jnp.logical_not(lo_half)
                contrib = _dot(mix, jnp.where(keep, x_pair, 0.0).astype(BF16))
                y_pair = contrib if y_pair is None else y_pair + contrib
            pairs.append(y_pair)
        y_g = jnp.concatenate(pairs, axis=1) + y_off + dsk_ref[:, gs] * xs[:, gs]

        zg = z_ref[0, :, gs].astype(F32)
        yz = y_g * (zg * jax.nn.sigmoid(zg))
        y_ref[0, :, gs] = _rms(yz, nrm_ref[:, gs], SSM_NORM_EPS).astype(y_ref.dtype)

    @pl.when(c == nchunks - 1)
    def _():
        for g in range(SSM_GROUPS):
            sout_ref[0, g] = st_sc[g].T


def _ssd(xbc, z, dt, state0, p, lc, valid_last, conv_buf=None):
    b, l, _ = xbc.shape
    conv_done = conv_buf is None
    assert l % lc == 0 and CONV_WIDTH - 1 <= valid_last <= lc
    assert valid_last == lc or l == lc
    state_spec = pl.BlockSpec((1, SSM_GROUPS, GROUP_WIDTH, SSM_STATE), lambda bi, c: (bi, 0, 0, 0))
    in_specs = [
        pl.BlockSpec((1, lc, CONV_DIM), lambda bi, c: (bi, c, 0)),
        pl.BlockSpec((1, lc, D_INNER), lambda bi, c: (bi, c, 0)),
        pl.BlockSpec((1, lc, LANES), lambda bi, c: (bi, c, 0)),
        state_spec,
        pl.BlockSpec((1, LANES), lambda bi, c: (0, 0)),
        pl.BlockSpec((1, LANES), lambda bi, c: (0, 0)),
        pl.BlockSpec((1, D_INNER), lambda bi, c: (0, 0)),
        pl.BlockSpec((1, D_INNER), lambda bi, c: (0, 0)),
        pl.BlockSpec((LANES, D_INNER), lambda bi, c: (0, 0)),
    ]
    args = [xbc, z, dt, state0, p["dt_bias"], p["a_log"], p["d_skip"], p["ssm_norm"], p["expand"]]
    out_specs = [pl.BlockSpec((1, lc, D_INNER), lambda bi, c: (bi, c, 0)), state_spec]
    out_shape = [jax.ShapeDtypeStruct((b, l, D_INNER), BF16),
                 jax.ShapeDtypeStruct((b, SSM_GROUPS, GROUP_WIDTH, SSM_STATE), F32)]
    scratch = [pltpu.VMEM((SSM_GROUPS, SSM_STATE, GROUP_WIDTH), F32)]
    if not conv_done:
        conv_spec = pl.BlockSpec((1, CONV_WIDTH - 1, CONV_DIM), lambda bi, c: (bi, 0, 0))
        in_specs += [conv_spec, pl.BlockSpec((CONV_WIDTH, CONV_DIM), lambda bi, c: (0, 0)),
                     pl.BlockSpec((1, CONV_DIM), lambda bi, c: (0, 0))]
        args += [conv_buf, p["conv_w"], p["conv_b"]]
        out_specs.append(conv_spec)
        out_shape.append(jax.ShapeDtypeStruct((b, CONV_WIDTH - 1, CONV_DIM), F32))
        scratch.append(pltpu.VMEM((lc + CONV_PAD, CONV_DIM), F32))
    return pl.pallas_call(
        functools.partial(_ssd_kernel, lc=lc, valid_last=valid_last, conv_done=conv_done),
        grid=(b, l // lc),
        in_specs=in_specs,
        out_specs=out_specs,
        out_shape=out_shape,
        scratch_shapes=scratch,
        compiler_params=_cparams("arbitrary", "arbitrary"),
        name="ssd",
    )(*args)


def _merge_kernel(x_ref, att_ref, y_ref, g_ref, wa_ref, ws_ref, wo_ref, o_ref):
    ba = _dot(att_ref[...], wa_ref[...])
    bs = _dot(y_ref[...], ws_ref[...])
    g = jax.nn.sigmoid(g_ref[...].astype(F32))
    merged = g[:, :D_MODEL] * ba + g[:, D_MODEL:] * bs
    o_ref[...] = x_ref[...] + _dot(merged.astype(BF16), wo_ref[...])


def _merge(x, att, y, gates, wa, ws, wo, tm):
    t = x.shape[0]
    return pl.pallas_call(
        _merge_kernel,
        grid=(t // tm,),
        in_specs=[
            pl.BlockSpec((tm, D_MODEL), lambda i: (i, 0)),
            pl.BlockSpec((tm, ATT_WIDTH), lambda i: (i, 0)),
            pl.BlockSpec((tm, D_INNER), lambda i: (i, 0)),
            pl.BlockSpec((tm, 2 * D_MODEL), lambda i: (i, 0)),
            _const_spec((ATT_WIDTH, D_MODEL)),
            _const_spec((D_INNER, D_MODEL)),
            _const_spec((D_MODEL, D_MODEL)),
        ],
        out_specs=pl.BlockSpec((tm, D_MODEL), lambda i: (i, 0)),
        out_shape=jax.ShapeDtypeStruct((t, D_MODEL), F32),
        compiler_params=_cparams("arbitrary"),
        name="merge",
    )(x, att, y, gates, wa, ws, wo)


FFN_SPLIT = 2
FFN_TF = D_FF // FFN_SPLIT


def _ffn_kernel(x_ref, nw_ref, wgu_ref, wd_ref, nf_ref, *out_refs, final_norm):
    x = x_ref[...]
    hb = _rms(x, nw_ref[...], RMS_EPS).astype(BF16)
    acc = x
    for c in range(FFN_SPLIT):
        gt = _dot(hb, wgu_ref[:, c * FFN_TF:(c + 1) * FFN_TF])
        up = _dot(hb, wgu_ref[:, D_FF + c * FFN_TF:D_FF + (c + 1) * FFN_TF])
        act = (gt * jax.nn.sigmoid(gt) * up).astype(BF16)
        acc = acc + _dot(act, wd_ref[c * FFN_TF:(c + 1) * FFN_TF, :])
    out_refs[0][...] = acc
    if final_norm:
        out_refs[1][...] = _rms(acc, nf_ref[...], RMS_EPS)


def _ffn(x, norm_w, wgu, wd, norm_final, final_norm, tm):
    t = x.shape[0]
    row_spec = pl.BlockSpec((tm, D_MODEL), lambda i: (i, 0))
    n_out = 2 if final_norm else 1
    return pl.pallas_call(
        functools.partial(_ffn_kernel, final_norm=final_norm),
        grid=(t // tm,),
        in_specs=[
            row_spec,
            pl.BlockSpec((1, D_MODEL), lambda i: (0, 0)),
            _const_spec((D_MODEL, 2 * D_FF)),
            _const_spec((D_FF, D_MODEL)),
            pl.BlockSpec((1, D_MODEL), lambda i: (0, 0)),
        ],
        out_specs=(row_spec,) * n_out,
        out_shape=(jax.ShapeDtypeStruct((t, D_MODEL), F32),) * n_out,
        compiler_params=_cparams("arbitrary"),
        name="ffn",
    )(x, norm_w, wgu, wd, norm_final)


def _layer_params(l, norm_mix, w_in, lambda_q1, lambda_k1, lambda_q2, lambda_k2, attn_subln, conv_w, conv_b,
                  dt_bias, a_log, d_skip, ssm_norm, w_branch_att, w_branch_ssd, w_out, norm_ffn, w_gate_up,
                  w_down):
    w = w_in[l]
    n_main = Q_DIM + Q_DIM + ATT_WIDTH + D_INNER + CONV_DIM
    scale = ATT_HEAD_DIM ** -0.5
    w_main = jnp.concatenate([(w[:, :Q_DIM] * scale).astype(BF16), w[:, Q_DIM:n_main].astype(BF16),
                              w[:, n_main + SSM_HEADS:].astype(BF16)], axis=1)
    w_dt = jnp.pad(w[:, n_main:n_main + SSM_HEADS], ((0, 0), (0, LANES - SSM_HEADS))).astype(BF16)
    head_of_channel = jnp.arange(D_INNER) // SSM_HEAD_DIM
    return {
        "norm_mix": norm_mix[l][None], "w_main": w_main, "w_dt": w_dt,
        "lamp": jnp.stack([lambda_q1[l], lambda_k1[l], lambda_q2[l], lambda_k2[l]]),
        "subln": attn_subln[l][None],
        "conv_w": conv_w[l], "conv_b": conv_b[l][None],
        "dt_bias": jnp.pad(dt_bias[l], (0, LANES - SSM_HEADS))[None],
        "a_log": jnp.pad(a_log[l], (0, LANES - SSM_HEADS))[None],
        "d_skip": jnp.repeat(d_skip[l], SSM_HEAD_DIM)[None],
        "ssm_norm": ssm_norm[l][None],
        "expand": (jnp.arange(LANES)[:, None] == head_of_channel[None, :]).astype(BF16),
        "wa": w_branch_att[l].astype(BF16), "ws": w_branch_ssd[l].astype(BF16), "wo": w_out[l].astype(BF16),
        "norm_ffn": norm_ffn[l][None], "wgu": w_gate_up[l].astype(BF16), "wd": w_down[l].astype(BF16),
    }


def _pad_time(x, b, n, lc):
    return jnp.pad(x.reshape(b, n, x.shape[-1]), ((0, 0), (0, lc - n), (0, 0)))


def kernel(x_prompt, x_sample, cache_k, cache_v, state_conv, state_ssm, norm_mix, w_in, lambda_q1, lambda_k1,
           lambda_q2, lambda_k2, attn_subln, conv_w, conv_b, dt_bias, a_log, d_skip, ssm_norm, w_branch_att,
           w_branch_ssd, w_out, norm_ffn, w_gate_up, w_down, norm_final):
    bp, lp, _ = x_prompt.shape
    bs, ls, _ = x_sample.shape
    assert bp == 1
    past = cache_k.shape[2]
    tp, ts = bp * lp, bs * ls
    xp = x_prompt.reshape(tp, D_MODEL)
    xs = x_sample.reshape(ts, D_MODEL)
    cache_k = jnp.transpose(cache_k, (0, 1, 3, 4, 5, 2)).reshape(DEPTH, bs, Q_DIM, past)
    cache_v = cache_v.reshape(DEPTH, bs, past * ATT_HEADS, ATT_V_DIM)
    state_ssm = state_ssm.reshape(DEPTH, bs, SSM_GROUPS, GROUP_WIDTH, SSM_STATE)
    conv0 = jnp.zeros((CONV_PAD, CONV_DIM), F32)
    ssm0 = jnp.zeros((bp, SSM_GROUPS, GROUP_WIDTH, SSM_STATE), F32)
    nf = norm_final[None]

    tm_p = 1024 if tp % 1024 == 0 else 128
    tm_row = 512 if tp % 512 == 0 else 128
    tq = ATTN_TQ if lp % ATTN_TQ == 0 else CHUNK
    tk = ATTN_TK if tq % ATTN_TK == 0 else tq

    conv_p, ssm_p, conv_s, ssm_s = [], [], [], []
    kv_p = kv_s = None
    yp = ys = None
    for l in range(DEPTH):
        p = _layer_params(l, norm_mix, w_in, lambda_q1, lambda_k1, lambda_q2, lambda_k2, attn_subln, conv_w,
                          conv_b, dt_bias, a_log, d_skip, ssm_norm, w_branch_att, w_branch_ssd, w_out,
                          norm_ffn, w_gate_up, w_down)
        lam_init = 0.8 - 0.6 * math.exp(-0.3 * l)
        last = l == DEPTH - 1

        q, k_all, kb, v_all, vb, z, act, gates, dt, tail = _in_proj(
            xp, p["norm_mix"], p["w_main"], p["w_dt"], tm_p, conv=(p["conv_w"], p["conv_b"], conv0),
            stacks=kv_p, layer=l)
        kv_p = (k_all, v_all)
        att = _attn_prompt(q, kb, vb, p["lamp"], p["subln"], lam_init, tq, tk, ATTN_UNROLL)
        y, s_new = _ssd(act.reshape(bp, lp, CONV_DIM), z.reshape(bp, lp, D_INNER), dt.reshape(bp, lp, LANES),
                        ssm0, p, SSD_CHUNK, SSD_CHUNK)
        x1 = _merge(xp, att, y.reshape(tp, D_INNER), gates, p["wa"], p["ws"], p["wo"], tm_row)
        res = _ffn(x1, p["norm_ffn"], p["wgu"], p["wd"], nf, last, tm_row)
        xp = res[0]
        if last:
            yp = res[1]
        conv_p.append(tail[-1, CONV_PAD - (CONV_WIDTH - 1):][None])
        ssm_p.append(s_new.reshape(bp, SSM_GROUPS, SSM_HEADS_PER_GROUP, SSM_HEAD_DIM, SSM_STATE))

        q, k_all, kb, v_all, vb, z, xbc, gates, dt = _in_proj(xs, p["norm_mix"], p["w_main"], p["w_dt"], ts,
                                                              stacks=kv_s, layer=l)
        kv_s = (k_all, v_all)
        att = _attn_sample(q.reshape(bs, ls, Q_DIM), cache_k, cache_v, kb.reshape(bs, ls, Q_DIM),
                           vb.reshape(bs, ls, ATT_WIDTH), p["lamp"], p["subln"], lam_init, l)
        y, s_new, c_new = _ssd(_pad_time(xbc, bs, ls, SSD_PAD_ROWS), _pad_time(z, bs, ls, SSD_PAD_ROWS),
                               _pad_time(dt, bs, ls, SSD_PAD_ROWS), state_ssm[l], p, SSD_PAD_ROWS, ls,
                               conv_buf=state_conv[l])
        x1 = _merge(xs, att.reshape(ts, ATT_WIDTH), y[:, :ls].reshape(ts, D_INNER), gates,
                    p["wa"], p["ws"], p["wo"], ts)
        res = _ffn(x1, p["norm_ffn"], p["wgu"], p["wd"], nf, last, ts)
        xs = res[0]
        if last:
            ys = res[1]
        conv_s.append(c_new)
        ssm_s.append(s_new.reshape(bs, SSM_GROUPS, SSM_HEADS_PER_GROUP, SSM_HEAD_DIM, SSM_STATE))

    return (yp.reshape(bp, lp, D_MODEL), ys.reshape(bs, ls, D_MODEL),
            kv_p[0].reshape(DEPTH, bp, lp, ATT_HEADS, 2, ATT_HEAD_DIM),
            kv_p[1].reshape(DEPTH, bp, lp, ATT_HEADS, ATT_V_DIM),
            jnp.stack(conv_p), jnp.stack(ssm_p),
            kv_s[0].reshape(DEPTH, bs, ls, ATT_HEADS, 2, ATT_HEAD_DIM),
            kv_s[1].reshape(DEPTH, bs, ls, ATT_HEADS, ATT_V_DIM),
            jnp.stack(conv_s), jnp.stack(ssm_s))
```

```python
import functools
import math

import jax
import jax.numpy as jnp
from jax import lax
from jax.experimental import pallas as pl
from jax.experimental.pallas import tpu as pltpu

F32 = jnp.float32
BF16 = jnp.bfloat16
LOG2E = math.log2(math.e)

D_MODEL = 1024
DEPTH = 2
CHUNK = 64
ATT_HEADS = 8
ATT_HEAD_DIM = 64
ATT_V_DIM = 128
ATT_WIDTH = 1024
Q_DIM = 1024
ATT_SUBLN_EPS = 1e-5
D_INNER = 2048
SSM_HEAD_DIM = 64
SSM_HEADS = 32
SSM_GROUPS = 8
SSM_HEADS_PER_GROUP = 4
SSM_STATE = 128
CONV_WIDTH = 4
CONV_DIM = 4096
SSM_NORM_EPS = 1e-5
D_FF = 2816
RMS_EPS = 1e-6
GROUP_WIDTH = SSM_HEADS_PER_GROUP * SSM_HEAD_DIM

LANES = 128
SUBLANES = 8
VMEM_LIMIT_BYTES = 56 * 1024 * 1024

PROJ_TN = 512
SSD_CHUNK = 256
SSD_PAD_ROWS = 128
CONV_PAD = 8
CONV_ROWS = 32
SAMPLE_TS = 1024
ATTN_TQ = 1024
ATTN_TK = 512
ATTN_UNROLL = 2


def _cparams(*sem):
    return pltpu.CompilerParams(dimension_semantics=sem, vmem_limit_bytes=VMEM_LIMIT_BYTES)


def _const_spec(shape):
    return pl.BlockSpec(shape, lambda *_: (0,) * len(shape), pipeline_mode=pl.Buffered(1))


def _rms(x, w, eps):
    return x * lax.rsqrt(jnp.mean(x * x, axis=-1, keepdims=True) + eps) * w


def _split3(x):
    hi = x.astype(BF16)
    r1 = x - hi.astype(F32)
    mid = r1.astype(BF16)
    lo = (r1 - mid.astype(F32)).astype(BF16)
    return hi, mid, lo


def _dot(a, b):
    return jnp.dot(a, b, preferred_element_type=F32)


def _dot_nt(a, b):
    return lax.dot_general(a, b, (((1,), (1,)), ((), ())), preferred_element_type=F32)


_NQ = Q_DIM // PROJ_TN
_NK = Q_DIM // PROJ_TN
_NV = ATT_WIDTH // PROJ_TN
_NZ = D_INNER // PROJ_TN
_NX = CONV_DIM // PROJ_TN
_NG = 2 * D_MODEL // PROJ_TN
_OFF_K = _NQ
_OFF_V = _OFF_K + _NK
_OFF_Z = _OFF_V + _NV
_OFF_X = _OFF_Z + _NZ
_OFF_G = _OFF_X + _NX
_N_COL_TILES = _OFF_G + _NG


def _in_proj_kernel(*refs, tm, fuse_conv, n_alias):
    x_ref, nw_ref, w_ref, wdt_ref = refs[:4]
    pos = 4
    if fuse_conv:
        cw_ref, cb_ref, hist0_ref = refs[pos:pos + 3]
        pos += 3
    pos += n_alias
    q_ref, k_ref, kb_ref, v_ref, vb_ref, z_ref, xbc_ref, g_ref, dt_ref = refs[pos:pos + 9]
    pos += 9
    if fuse_conv:
        tail_ref = refs[pos]
        pos += 1
    h_sc = refs[pos]
    i = pl.program_id(0)
    j = pl.program_id(1)

    @pl.when(j == 0)
    def _():
        hb = _rms(x_ref[...], nw_ref[...], RMS_EPS).astype(BF16)
        h_sc[...] = hb
        dt_ref[...] = _dot(hb, wdt_ref[...])

    def tile():
        return _dot(h_sc[...], w_ref[...])

    @pl.when(j < _OFF_K)
    def _():
        q_ref[...] = (tile() * (LOG2E * ATT_HEAD_DIM ** -0.5)).astype(BF16)

    @pl.when((j >= _OFF_K) & (j < _OFF_V))
    def _():
        res = tile()
        k_ref[...] = res
        kb_ref[...] = res.astype(BF16)

    @pl.when((j >= _OFF_V) & (j < _OFF_Z))
    def _():
        res = tile()
        v_ref[...] = res
        vb_ref[...] = res.astype(BF16)

    @pl.when((j >= _OFF_Z) & (j < _OFF_X))
    def _():
        z_ref[...] = tile().astype(z_ref.dtype)

    if not fuse_conv:
        @pl.when((j >= _OFF_X) & (j < _OFF_G))
        def _():
            xbc_ref[...] = tile()

        @pl.when(j >= _OFF_G)
        def _():
            g_ref[...] = tile().astype(g_ref.dtype)
        return

    hist_sc, raw_sc = refs[pos + 1], refs[pos + 2]

    @pl.when((i == 0) & (j == 0))
    def _():
        for c in range(_NX):
            hist_sc[c] = hist0_ref[:, c * PROJ_TN:(c + 1) * PROJ_TN]

    def activate(jc, slot):
        for r0 in range(0, tm, CONV_ROWS):
            before = hist_sc[jc] if r0 == 0 else raw_sc[slot, r0 - CONV_PAD:r0, :]
            xp = jnp.concatenate([before, raw_sc[slot, r0:r0 + CONV_ROWS, :]], axis=0)
            conv = cw_ref[0:1, :] * xp
            for tap in range(1, CONV_WIDTH):
                conv = cw_ref[tap:tap + 1, :] * xp + pltpu.roll(conv, 1, 0)
            conv = conv[CONV_PAD:] + cb_ref[...]
            xbc_ref[r0:r0 + CONV_ROWS, :] = (conv * jax.nn.sigmoid(conv)).astype(xbc_ref.dtype)
        last_rows = raw_sc[slot, tm - CONV_PAD:tm, :]
        hist_sc[jc] = last_rows
        tail_ref[...] = last_rows

    @pl.when(j == _OFF_X)
    def _():
        raw_sc[0] = tile()

    for slot in range(2):
        @pl.when((j > _OFF_X) & (j < _OFF_G) & (lax.rem(j - _OFF_X, 2) == slot))
        def _():
            activate(j - _OFF_X - 1, 1 - slot)
            raw_sc[slot] = tile()

    @pl.when(j == _OFF_G)
    def _():
        activate(_NX - 1, (_NX - 1) % 2)
        g_ref[...] = tile().astype(g_ref.dtype)

    @pl.when(j > _OFF_G)
    def _():
        g_ref[...] = tile().astype(g_ref.dtype)


def _in_proj(x, norm_w, w_main, w_dt, tm, conv=None, stacks=None, layer=0):
    t = x.shape[0]
    fuse_conv = conv is not None

    def col(off, n):
        return lambda i, j: (i, jnp.clip(j - off, 0, n - 1))

    def out_spec(off, n):
        return pl.BlockSpec((tm, PROJ_TN), col(off, n))

    def stack_spec(off, n):
        return pl.BlockSpec((None, tm, PROJ_TN), lambda i, j: (layer,) + col(off, n)(i, j))

    xbc_lag = 1 if fuse_conv else 0

    def xcol(i, j):
        return (0, jnp.clip(j - _OFF_X - xbc_lag, 0, _NX - 1))

    in_specs = [
        pl.BlockSpec((tm, D_MODEL), lambda i, j: (i, 0)),
        pl.BlockSpec((1, D_MODEL), lambda i, j: (0, 0)),
        pl.BlockSpec((D_MODEL, PROJ_TN), lambda i, j: (0, j)),
        pl.BlockSpec((D_MODEL, LANES), lambda i, j: (0, 0)),
    ]
    args = [x, norm_w, w_main, w_dt]
    if fuse_conv:
        in_specs += [pl.BlockSpec((CONV_WIDTH, PROJ_TN), xcol), pl.BlockSpec((1, PROJ_TN), xcol),
                     pl.BlockSpec((CONV_PAD, CONV_DIM), lambda i, j: (0, 0))]
        args += list(conv)
    aliases = {}
    if stacks is not None:
        in_specs += [pl.BlockSpec(memory_space=pl.ANY)] * 2
        aliases = {len(args): 1, len(args) + 1: 3}
        args += list(stacks)
    out_shape = [
        jax.ShapeDtypeStruct((t, Q_DIM), BF16),
        jax.ShapeDtypeStruct((DEPTH, t, Q_DIM), F32),
        jax.ShapeDtypeStruct((t, Q_DIM), BF16),
        jax.ShapeDtypeStruct((DEPTH, t, ATT_WIDTH), F32),
        jax.ShapeDtypeStruct((t, ATT_WIDTH), BF16),
        jax.ShapeDtypeStruct((t, D_INNER), BF16),
        jax.ShapeDtypeStruct((t, CONV_DIM), BF16 if fuse_conv else F32),
        jax.ShapeDtypeStruct((t, 2 * D_MODEL), BF16),
        jax.ShapeDtypeStruct((t, LANES), F32),
    ]
    out_specs = [
        out_spec(0, _NQ), stack_spec(_OFF_K, _NK), out_spec(_OFF_K, _NK),
        stack_spec(_OFF_V, _NV), out_spec(_OFF_V, _NV), out_spec(_OFF_Z, _NZ),
        out_spec(_OFF_X + xbc_lag, _NX), out_spec(_OFF_G, _NG),
        pl.BlockSpec((tm, LANES), lambda i, j: (i, 0)),
    ]
    scratch = [pltpu.VMEM((tm, D_MODEL), BF16)]
    if fuse_conv:
        out_shape.append(jax.ShapeDtypeStruct((t // tm, CONV_PAD, CONV_DIM), F32))
        out_specs.append(pl.BlockSpec((None, CONV_PAD, PROJ_TN), lambda i, j: (i,) + xcol(i, j)))
        scratch += [pltpu.VMEM((_NX, CONV_PAD, PROJ_TN), F32), pltpu.VMEM((2, tm, PROJ_TN), F32)]
    return pl.pallas_call(
        functools.partial(_in_proj_kernel, tm=tm, fuse_conv=fuse_conv, n_alias=len(aliases)),
        grid=(t // tm, _N_COL_TILES),
        in_specs=in_specs,
        out_specs=out_specs,
        out_shape=out_shape,
        scratch_shapes=scratch,
        input_output_aliases=aliases,
        compiler_params=_cparams("arbitrary", "arbitrary"),
        name="in_proj",
    )(*args)


def _stack_q(q):
    lane = lax.broadcasted_iota(jnp.int32, q.shape, 1)
    zero = jnp.zeros_like(q)
    return jnp.concatenate([jnp.where(lane < ATT_HEAD_DIM, q, zero),
                            jnp.where(lane >= ATT_HEAD_DIM, q, zero)], axis=0)


def _diff_epilogue(o, n, lamp_ref, subln_ref, lam_init):
    lp = lamp_ref[...]
    lam = (jnp.exp(jnp.sum(lp[0:1] * lp[1:2], axis=-1, keepdims=True))
           - jnp.exp(jnp.sum(lp[2:3] * lp[3:4], axis=-1, keepdims=True)) + lam_init)
    a = o[:n] - lam * o[n:]
    return _rms(a, subln_ref[...], ATT_SUBLN_EPS) * (1.0 - lam_init)


def _attn_prompt_kernel(lamp_ref, subln_ref, q_ref, k_ref, v_ref, o_ref, m_sc, l_sc, acc_sc, s_sc, *,
                        tq, tk, nq, unroll, lam_init):
    qi = pl.program_id(1)
    qs = _stack_q(q_ref[...])
    m_sc[...] = jnp.full(m_sc.shape, -jnp.inf, F32)
    l_sc[...] = jnp.zeros(l_sc.shape, F32)
    acc_sc[...] = jnp.zeros(acc_sc.shape, F32)
    n_lane_tiles = tk // LANES
    full_blocks = qi * (tq // tk)

    def scores(kblk, masked):
        start = pl.multiple_of(kblk * tk, tk)
        s = _dot_nt(qs, k_ref[pl.ds(start, tk), :])
        if masked:
            r = lax.broadcasted_iota(jnp.int32, s.shape, 0)
            c = lax.broadcasted_iota(jnp.int32, s.shape, 1)
            q_chunk = jnp.where(r >= tq, r - tq, r) // CHUNK
            k_chunk = c // CHUNK + (kblk - full_blocks) * (tk // CHUNK)
            s = jnp.where(k_chunk <= q_chunk, s, -jnp.inf)
        return s

    def absorb(kblk, s, state):
        m_prev, l_prev, acc_prev = state
        vb = v_ref[pl.ds(pl.multiple_of(kblk * tk, tk), tk), :]
        m_new = jnp.maximum(m_prev, jnp.max(s, axis=-1, keepdims=True))
        alpha = jnp.exp2(m_prev - m_new)
        p = jnp.exp2(s - jnp.concatenate([m_new] * n_lane_tiles, axis=1))
        pv = _dot(p.astype(BF16), jnp.concatenate([vb, jnp.ones((tk, LANES), BF16)], axis=1))
        return m_new, alpha * l_prev + pv[:, ATT_V_DIM:], alpha * acc_prev + pv[:, :ATT_V_DIM]

    PLAIN, MASKED = "plain", "masked"

    def pipeline(first, produce):
        state = (m_sc[...], l_sc[...], acc_sc[...])
        s = s_sc[...]
        for i, kind in enumerate(produce):
            s_next = None if kind is None else scores(first + i + 1, kind == MASKED)
            state = absorb(first + i, s, state)
            s = s_next
        if s is not None:
            s_sc[...] = s
        m_sc[...], l_sc[...], acc_sc[...] = state

    diag_blocks = tq // tk

    @pl.when(qi == 0)
    def _():
        s_sc[...] = scores(0, True)
        pipeline(0, [MASKED] * (diag_blocks - 1) + [None])

    @pl.when(qi > 0)
    def _():
        s_sc[...] = scores(0, False)

    def body(t, carry):
        pipeline(unroll * t, [PLAIN] * unroll)
        return carry

    plain_steps = jnp.maximum(full_blocks - 1, 0)
    lax.fori_loop(0, plain_steps // unroll, body, 0)
    for r in sorted({(diag_blocks * q - 1) % unroll for q in range(1, nq)}):
        @pl.when((qi > 0) & (plain_steps % unroll == r))
        def _():
            pipeline(full_blocks - 1 - r, [PLAIN] * r + [MASKED] * diag_blocks + [None])

    o = acc_sc[...] * (1.0 / l_sc[...])
    o_ref[...] = _diff_epilogue(o, tq, lamp_ref, subln_ref, lam_init).astype(o_ref.dtype)


def _attn_prompt(q, kb, vb, lamp, subln, lam_init, tq, tk, unroll):
    t = q.shape[0]
    assert t % tq == 0 and tq % tk == 0 and tk % LANES == 0 and tk % CHUNK == 0
    kern = functools.partial(_attn_prompt_kernel, tq=tq, tk=tk, nq=t // tq, unroll=unroll, lam_init=lam_init)
    return pl.pallas_call(
        kern,
        grid=(ATT_HEADS, t // tq),
        in_specs=[
            pl.BlockSpec((4, ATT_HEAD_DIM), lambda h, i: (0, 0)),
            pl.BlockSpec((1, ATT_V_DIM), lambda h, i: (0, 0)),
            pl.BlockSpec((tq, LANES), lambda h, i: (i, h)),
            pl.BlockSpec((t, LANES), lambda h, i: (0, h)),
            pl.BlockSpec((t, LANES), lambda h, i: (0, h)),
        ],
        out_specs=pl.BlockSpec((tq, LANES), lambda h, i: (i, h)),
        out_shape=jax.ShapeDtypeStruct((t, ATT_WIDTH), BF16),
        scratch_shapes=[pltpu.VMEM((2 * tq, LANES), F32), pltpu.VMEM((2 * tq, LANES), F32),
                        pltpu.VMEM((2 * tq, ATT_V_DIM), F32), pltpu.VMEM((2 * tq, tk), F32)],
        compiler_params=_cparams("arbitrary", "arbitrary"),
        name="attn_prompt",
    )(lamp, subln, q, kb, vb)


def _attn_sample_kernel(lamp_ref, subln_ref, q_ref, kt_ref, vc_ref, kn_ref, vn_ref, o_ref, m_sc, l_sc, acc_sc, *,
                        n, ts, lam_init):
    piece = pl.program_id(1)

    @pl.when(piece == 0)
    def _():
        m_sc[...] = jnp.full(m_sc.shape, -jnp.inf, F32)
        l_sc[...] = jnp.zeros(l_sc.shape, F32)
        acc_sc[...] = jnp.zeros(acc_sc.shape, F32)

    def update(h, s, v):
        width = s.shape[1]
        m_prev = m_sc[h]
        m_new = jnp.maximum(m_prev, jnp.max(s, axis=-1, keepdims=True))
        alpha = jnp.exp2(m_prev - m_new)
        m_wide = m_new[:, :width] if width <= LANES else jnp.concatenate([m_new] * (width // LANES), axis=1)
        p = jnp.exp2(s - m_wide)
        l_sc[h] = alpha * l_sc[h] + jnp.sum(p, axis=-1, keepdims=True)
        acc_sc[h] = alpha * acc_sc[h] + _dot(p.astype(BF16), v)
        m_sc[h] = m_new

    for h in range(ATT_HEADS):
        hs = slice(h * LANES, (h + 1) * LANES)
        qs = _stack_q(q_ref[0, :, hs])
        s = _dot(qs, kt_ref[0, 0, hs, :].astype(BF16))
        update(h, s, vc_ref[0, 0, pl.ds(h, ts, stride=ATT_HEADS), :].astype(BF16))

    @pl.when(piece == pl.num_programs(1) - 1)
    def _():
        for h in range(ATT_HEADS):
            hs = slice(h * LANES, (h + 1) * LANES)
            qs = _stack_q(q_ref[0, :, hs])
            update(h, _dot_nt(qs, kn_ref[0, :, hs]), vn_ref[0, :, hs])
            o = acc_sc[h] * (1.0 / l_sc[h])
            o_ref[0, :, hs] = _diff_epilogue(o, n, lamp_ref, subln_ref, lam_init).astype(o_ref.dtype)


def _attn_sample(q, cache_kt, cache_v, kb, vb, lamp, subln, lam_init, layer):
    b, n, _ = q.shape
    past = cache_kt.shape[3]
    ts = min(past, SAMPLE_TS)
    assert past % CHUNK == 0 and n <= CHUNK and past % ts == 0
    kern = functools.partial(_attn_sample_kernel, n=n, ts=ts, lam_init=lam_init)
    row_spec = pl.BlockSpec((1, n, Q_DIM), lambda bi, s: (bi, 0, 0))
    return pl.pallas_call(
        kern,
        grid=(b, past // ts),
        in_specs=[
            pl.BlockSpec((4, ATT_HEAD_DIM), lambda bi, s: (0, 0)),
            pl.BlockSpec((1, ATT_V_DIM), lambda bi, s: (0, 0)),
            row_spec,
            pl.BlockSpec((1, 1, Q_DIM, ts), lambda bi, s: (layer, bi, 0, s)),
            pl.BlockSpec((1, 1, ts * ATT_HEADS, ATT_V_DIM), lambda bi, s: (layer, bi, s, 0)),
            row_spec,
            row_spec,
        ],
        out_specs=row_spec,
        out_shape=jax.ShapeDtypeStruct((b, n, ATT_WIDTH), BF16),
        scratch_shapes=[pltpu.VMEM((ATT_HEADS, 2 * n, LANES), F32)] * 3,
        compiler_params=_cparams("arbitrary", "arbitrary"),
        name="attn_sample",
    )(lamp, subln, q, cache_kt, cache_v, kb, vb)


def _ssd_kernel(*refs, lc, valid_last, conv_done):
    if conv_done:
        (xbc_ref, z_ref, dt_ref, st0_ref, dtb_ref, alog_ref, dsk_ref, nrm_ref, expand_ref,
         y_ref, sout_ref, st_sc) = refs
    else:
        (xbc_ref, z_ref, dt_ref, st0_ref, dtb_ref, alog_ref, dsk_ref, nrm_ref, expand_ref,
         cbuf_ref, cw_ref, cb_ref, y_ref, sout_ref, cout_ref, st_sc, xpad_sc) = refs
    c = pl.program_id(1)
    nchunks = pl.num_programs(1)

    @pl.when(c == 0)
    def _():
        for g in range(SSM_GROUPS):
            st_sc[g] = st0_ref[0, g].T

    if conv_done:
        act = xbc_ref[0]
    else:
        @pl.when(c == 0)
        def _():
            xpad_sc[0:CONV_PAD, :] = jnp.zeros((CONV_PAD, CONV_DIM), F32)
            xpad_sc[CONV_PAD - (CONV_WIDTH - 1):CONV_PAD, :] = cbuf_ref[0]

        xpad_sc[CONV_PAD:CONV_PAD + lc, :] = xbc_ref[0]
        xp = xpad_sc[...]
        conv = cw_ref[0:1, :] * xp
        for j in range(1, CONV_WIDTH):
            conv = cw_ref[j:j + 1, :] * xp + pltpu.roll(conv, 1, 0)
        conv = conv[CONV_PAD:CONV_PAD + lc] + cb_ref[...]
        act = conv * jax.nn.sigmoid(conv)

        @pl.when(c == nchunks - 1)
        def _():
            end = CONV_PAD + valid_last
            cout_ref[0] = xpad_sc[end - (CONV_WIDTH - 1):end, :]

        xpad_sc[0:CONV_PAD, :] = xpad_sc[lc:lc + CONV_PAD, :]

    dtr = dt_ref[0] + dtb_ref[...]
    dt = jnp.maximum(dtr, 0.0) + jnp.log1p(jnp.exp(-jnp.abs(dtr)))
    if valid_last < lc:
        row = lax.broadcasted_iota(jnp.int32, dt.shape, 0)
        dt = jnp.where(row < valid_last, dt, 0.0)
    a = dt * (-jnp.exp(alog_ref[...]))
    rr = lax.broadcasted_iota(jnp.int32, (lc, lc), 0)
    cc = lax.broadcasted_iota(jnp.int32, (lc, lc), 1)
    causal = rr >= cc
    tri = jnp.where(causal, 1.0, 0.0).astype(BF16)
    a_cs = sum(_dot(tri, part) for part in _split3(a))
    a_last = a_cs[lc - 1:lc, :]
    a_cs_t = a_cs.T

    expand = expand_ref[...]
    stack = jnp.concatenate([dt, dt * jnp.exp(a_last - a_cs), jnp.exp(a_cs)], axis=0)
    ex = _dot(stack.astype(BF16), expand)
    w_dt = ex[0:lc]
    w_state = ex[lc:2 * lc]
    w_off = ex[2 * lc:3 * lc]
    carry = jnp.broadcast_to(jnp.exp(a_last), (2 * SUBLANES, LANES))
    w_carry = sum(_dot(part, expand) for part in _split3(carry))[0:1]

    xs = act[:, :D_INNER].astype(F32)
    xdt = xs * w_dt
    xst = xs * w_state
    lane = lax.broadcasted_iota(jnp.int32, (lc, LANES), 1)
    lo_half = lane < SSM_HEAD_DIM

    for g in range(SSM_GROUPS):
        gs = slice(g * GROUP_WIDTH, (g + 1) * GROUP_WIDTH)
        bg = act[:, D_INNER + g * SSM_STATE:D_INNER + (g + 1) * SSM_STATE]
        cg = act[:, D_INNER + SSM_GROUPS * SSM_STATE + g * SSM_STATE:
                 D_INNER + SSM_GROUPS * SSM_STATE + (g + 1) * SSM_STATE]
        bgb = bg.astype(BF16)
        cgb = cg.astype(BF16)
        cb = _dot_nt(cgb, bgb)
        st_prev = st_sc[g]
        y_off = _dot(cgb, st_prev.astype(BF16)) * w_off[:, gs]
        st_sc[g] = w_carry[:, gs] * st_prev + _dot(bg.astype(F32).T.astype(BF16), xst[:, gs].astype(BF16))

        pairs = []
        for pr in range(SSM_HEADS_PER_GROUP // 2):
            x_pair = xdt[:, g * GROUP_WIDTH + pr * LANES:g * GROUP_WIDTH + (pr + 1) * LANES]
            y_pair = None
            for half in range(2):
                h = g * SSM_HEADS_PER_GROUP + pr * 2 + half
                seg = a_cs[:, h:h + 1] - a_cs_t[h:h + 1, :]
                decay = jnp.exp(jnp.where(causal, seg, -jnp.inf))
                mix = (cb * decay).astype(BF16)
                keep = lo_half if half == 0 else jnp.logical_not(lo_half)
                contrib = _dot(mix, jnp.where(keep, x_pair, 0.0).astype(BF16))
                y_pair = contrib if y_pair is None else y_pair + contrib
            pairs.append(y_pair)
        y_g = jnp.concatenate(pairs, axis=1) + y_off + dsk_ref[:, gs] * xs[:, gs]

        zg = z_ref[0, :, gs].astype(F32)
        yz = y_g * (zg * jax.nn.sigmoid(zg))
        y_ref[0, :, gs] = _rms(yz, nrm_ref[:, gs], SSM_NORM_EPS).astype(y_ref.dtype)

    @pl.when(c == nchunks - 1)
    def _():
        for g in range(SSM_GROUPS):
            sout_ref[0, g] = st_sc[g].T


def _ssd(xbc, z, dt, state0, p, lc, valid_last, conv_buf=None):
    b, l, _ = xbc.shape
    conv_done = conv_buf is None
    assert l % lc == 0 and CONV_WIDTH - 1 <= valid_last <= lc
    assert valid_last == lc or l == lc
    state_spec = pl.BlockSpec((1, SSM_GROUPS, GROUP_WIDTH, SSM_STATE), lambda bi, c: (bi, 0, 0, 0))
    in_specs = [
        pl.BlockSpec((1, lc, CONV_DIM), lambda bi, c: (bi, c, 0)),
        pl.BlockSpec((1, lc, D_INNER), lambda bi, c: (bi, c, 0)),
        pl.BlockSpec((1, lc, LANES), lambda bi, c: (bi, c, 0)),
        state_spec,
        pl.BlockSpec((1, LANES), lambda bi, c: (0, 0)),
        pl.BlockSpec((1, LANES), lambda bi, c: (0, 0)),
        pl.BlockSpec((1, D_INNER), lambda bi, c: (0, 0)),
        pl.BlockSpec((1, D_INNER), lambda bi, c: (0, 0)),
        pl.BlockSpec((LANES, D_INNER), lambda bi, c: (0, 0)),
    ]
    args = [xbc, z, dt, state0, p["dt_bias"], p["a_log"], p["d_skip"], p["ssm_norm"], p["expand"]]
    out_specs = [pl.BlockSpec((1, lc, D_INNER), lambda bi, c: (bi, c, 0)), state_spec]
    out_shape = [jax.ShapeDtypeStruct((b, l, D_INNER), BF16),
                 jax.ShapeDtypeStruct((b, SSM_GROUPS, GROUP_WIDTH, SSM_STATE), F32)]
    scratch = [pltpu.VMEM((SSM_GROUPS, SSM_STATE, GROUP_WIDTH), F32)]
    if not conv_done:
        conv_spec = pl.BlockSpec((1, CONV_WIDTH - 1, CONV_DIM), lambda bi, c: (bi, 0, 0))
        in_specs += [conv_spec, pl.BlockSpec((CONV_WIDTH, CONV_DIM), lambda bi, c: (0, 0)),
                     pl.BlockSpec((1, CONV_DIM), lambda bi, c: (0, 0))]
        args += [conv_buf, p["conv_w"], p["conv_b"]]
        out_specs.append(conv_spec)
        out_shape.append(jax.ShapeDtypeStruct((b, CONV_WIDTH - 1, CONV_DIM), F32))
        scratch.append(pltpu.VMEM((lc + CONV_PAD, CONV_DIM), F32))
    return pl.pallas_call(
        functools.partial(_ssd_kernel, lc=lc, valid_last=valid_last, conv_done=conv_done),
        grid=(b, l // lc),
        in_specs=in_specs,
        out_specs=out_specs,
        out_shape=out_shape,
        scratch_shapes=scratch,
        compiler_params=_cparams("arbitrary", "arbitrary"),
        name="ssd",
    )(*args)


def _merge_kernel(x_ref, att_ref, y_ref, g_ref, wa_ref, ws_ref, wo_ref, o_ref):
    ba = _dot(att_ref[...], wa_ref[...])
    bs = _dot(y_ref[...], ws_ref[...])
    g = jax.nn.sigmoid(g_ref[...].astype(F32))
    merged = g[:, :D_MODEL] * ba + g[:, D_MODEL:] * bs
    o_ref[...] = x_ref[...] + _dot(merged.astype(BF16), wo_ref[...])


def _merge(x, att, y, gates, wa, ws, wo, tm):
    t = x.shape[0]
    return pl.pallas_call(
        _merge_kernel,
        grid=(t // tm,),
        in_specs=[
            pl.BlockSpec((tm, D_MODEL), lambda i: (i, 0)),
            pl.BlockSpec((tm, ATT_WIDTH), lambda i: (i, 0)),
            pl.BlockSpec((tm, D_INNER), lambda i: (i, 0)),
            pl.BlockSpec((tm, 2 * D_MODEL), lambda i: (i, 0)),
            _const_spec((ATT_WIDTH, D_MODEL)),
            _const_spec((D_INNER, D_MODEL)),
            _const_spec((D_MODEL, D_MODEL)),
        ],
        out_specs=pl.BlockSpec((tm, D_MODEL), lambda i: (i, 0)),
        out_shape=jax.ShapeDtypeStruct((t, D_MODEL), F32),
        compiler_params=_cparams("arbitrary"),
        name="merge",
    )(x, att, y, gates, wa, ws, wo)


FFN_SPLIT = 2
FFN_TF = D_FF // FFN_SPLIT


def _ffn_kernel(x_ref, nw_ref, wgu_ref, wd_ref, nf_ref, *out_refs, final_norm):
    x = x_ref[...]
    hb = _rms(x, nw_ref[...], RMS_EPS).astype(BF16)
    acc = x
    for c in range(FFN_SPLIT):
        gt = _dot(hb, wgu_ref[:, c * FFN_TF:(c + 1) * FFN_TF])
        up = _dot(hb, wgu_ref[:, D_FF + c * FFN_TF:D_FF + (c + 1) * FFN_TF])
        act = (gt * jax.nn.sigmoid(gt) * up).astype(BF16)
        acc = acc + _dot(act, wd_ref[c * FFN_TF:(c + 1) * FFN_TF, :])
    out_refs[0][...] = acc
    if final_norm:
        out_refs[1][...] = _rms(acc, nf_ref[...], RMS_EPS)


def _ffn(x, norm_w, wgu, wd, norm_final, final_norm, tm):
    t = x.shape[0]
    row_spec = pl.BlockSpec((tm, D_MODEL), lambda i: (i, 0))
    n_out = 2 if final_norm else 1
    return pl.pallas_call(
        functools.partial(_ffn_kernel, final_norm=final_norm),
        grid=(t // tm,),
        in_specs=[
            row_spec,
            pl.BlockSpec((1, D_MODEL), lambda i: (0, 0)),
            _const_spec((D_MODEL, 2 * D_FF)),
            _const_spec((D_FF, D_MODEL)),
            pl.BlockSpec((1, D_MODEL), lambda i: (0, 0)),
        ],
        out_specs=(row_spec,) * n_out,
        out_shape=(jax.ShapeDtypeStruct((t, D_MODEL), F32),) * n_out,
        compiler_params=_cparams("arbitrary"),
        name="ffn",
    )(x, norm_w, wgu, wd, norm_final)


def _layer_params(l, norm_mix, w_in, lambda_q1, lambda_k1, lambda_q2, lambda_k2, attn_subln, conv_w, conv_b,
                  dt_bias, a_log, d_skip, ssm_norm, w_branch_att, w_branch_ssd, w_out, norm_ffn, w_gate_up,
                  w_down):
    w = w_in[l]
    n_main = Q_DIM + Q_DIM + ATT_WIDTH + D_INNER + CONV_DIM
    w_main = jnp.concatenate([w[:, :n_main], w[:, n_main + SSM_HEADS:]], axis=1)
    w_dt = jnp.pad(w[:, n_main:n_main + SSM_HEADS], ((0, 0), (0, LANES - SSM_HEADS)))
    head_of_channel = jnp.arange(D_INNER) // SSM_HEAD_DIM
    return {
        "norm_mix": norm_mix[l][None], "w_main": w_main, "w_dt": w_dt,
        "lamp": jnp.stack([lambda_q1[l], lambda_k1[l], lambda_q2[l], lambda_k2[l]]),
        "subln": attn_subln[l][None],
        "conv_w": conv_w[l], "conv_b": conv_b[l][None],
        "dt_bias": jnp.pad(dt_bias[l], (0, LANES - SSM_HEADS))[None],
        "a_log": jnp.pad(a_log[l], (0, LANES - SSM_HEADS))[None],
        "d_skip": jnp.repeat(d_skip[l], SSM_HEAD_DIM)[None],
        "ssm_norm": ssm_norm[l][None],
        "expand": (jnp.arange(LANES)[:, None] == head_of_channel[None, :]).astype(BF16),
        "wa": w_branch_att[l].astype(BF16), "ws": w_branch_ssd[l].astype(BF16), "wo": w_out[l].astype(BF16),
        "norm_ffn": norm_ffn[l][None], "wgu": w_gate_up[l].astype(BF16), "wd": w_down[l].astype(BF16),
    }


def _pad_time(x, b, n, lc):
    return jnp.pad(x.reshape(b, n, x.shape[-1]), ((0, 0), (0, lc - n), (0, 0)))


def kernel(x_prompt, x_sample, cache_k, cache_v, state_conv, state_ssm, norm_mix, w_in, lambda_q1, lambda_k1,
           lambda_q2, lambda_k2, attn_subln, conv_w, conv_b, dt_bias, a_log, d_skip, ssm_norm, w_branch_att,
           w_branch_ssd, w_out, norm_ffn, w_gate_up, w_down, norm_final):
    bp, lp, _ = x_prompt.shape
    bs, ls, _ = x_sample.shape
    assert bp == 1
    past = cache_k.shape[2]
    tp, ts = bp * lp, bs * ls
    xp = x_prompt.reshape(tp, D_MODEL)
    xs = x_sample.reshape(ts, D_MODEL)
    cache_k = jnp.transpose(cache_k, (0, 1, 3, 4, 5, 2)).reshape(DEPTH, bs, Q_DIM, past)
    cache_v = cache_v.reshape(DEPTH, bs, past * ATT_HEADS, ATT_V_DIM)
    state_ssm = state_ssm.reshape(DEPTH, bs, SSM_GROUPS, GROUP_WIDTH, SSM_STATE)
    conv0 = jnp.zeros((CONV_PAD, CONV_DIM), F32)
    ssm0 = jnp.zeros((bp, SSM_GROUPS, GROUP_WIDTH, SSM_STATE), F32)
    nf = norm_final[None]

    tm_p = 1024 if tp % 1024 == 0 else 128
    tm_row = 512 if tp % 512 == 0 else 128
    tq = ATTN_TQ if lp % ATTN_TQ == 0 else CHUNK
    tk = ATTN_TK if tq % ATTN_TK == 0 else tq

    w_in = w_in.astype(BF16)
    conv_p, ssm_p, conv_s, ssm_s = [], [], [], []
    kv_p = kv_s = None
    yp = ys = None
    for l in range(DEPTH):
        p = _layer_params(l, norm_mix, w_in, lambda_q1, lambda_k1, lambda_q2, lambda_k2, attn_subln, conv_w,
                          conv_b, dt_bias, a_log, d_skip, ssm_norm, w_branch_att, w_branch_ssd, w_out,
                          norm_ffn, w_gate_up, w_down)
        lam_init = 0.8 - 0.6 * math.exp(-0.3 * l)
        last = l == DEPTH - 1

        q, k_all, kb, v_all, vb, z, act, gates, dt, tail = _in_proj(
            xp, p["norm_mix"], p["w_main"], p["w_dt"], tm_p, conv=(p["conv_w"], p["conv_b"], conv0),
            stacks=kv_p, layer=l)
        kv_p = (k_all, v_all)
        att = _attn_prompt(q, kb, vb, p["lamp"], p["subln"], lam_init, tq, tk, ATTN_UNROLL)
        y, s_new = _ssd(act.reshape(bp, lp, CONV_DIM), z.reshape(bp, lp, D_INNER), dt.reshape(bp, lp, LANES),
                        ssm0, p, SSD_CHUNK, SSD_CHUNK)
        x1 = _merge(xp, att, y.reshape(tp, D_INNER), gates, p["wa"], p["ws"], p["wo"], tm_row)
        res = _ffn(x1, p["norm_ffn"], p["wgu"], p["wd"], nf, last, tm_row)
        xp = res[0]
        if last:
            yp = res[1]
        conv_p.append(tail[-1, CONV_PAD - (CONV_WIDTH - 1):][None])
        ssm_p.append(s_new.reshape(bp, SSM_GROUPS, SSM_HEADS_PER_GROUP, SSM_HEAD_DIM, SSM_STATE))

        q, k_all, kb, v_all, vb, z, xbc, gates, dt = _in_proj(xs, p["norm_mix"], p["w_main"], p["w_dt"], ts,
                                                              stacks=kv_s, layer=l)
        kv_s = (k_all, v_all)
        att = _attn_sample(q.reshape(bs, ls, Q_DIM), cache_k, cache_v, kb.reshape(bs, ls, Q_DIM),
                           vb.reshape(bs, ls, ATT_WIDTH), p["lamp"], p["subln"], lam_init, l)
        y, s_new, c_new = _ssd(_pad_time(xbc, bs, ls, SSD_PAD_ROWS), _pad_time(z, bs, ls, SSD_PAD_ROWS),
                               _pad_time(dt, bs, ls, SSD_PAD_ROWS), state_ssm[l], p, SSD_PAD_ROWS, ls,
                               conv_buf=state_conv[l])
        x1 = _merge(xs, att.reshape(ts, ATT_WIDTH), y[:, :ls].reshape(ts, D_INNER), gates,
                    p["wa"], p["ws"], p["wo"], ts)
        res = _ffn(x1, p["norm_ffn"], p["wgu"], p["wd"], nf, last, ts)
        xs = res[0]
        if last:
            ys = res[1]
        conv_s.append(c_new)
        ssm_s.append(s_new.reshape(bs, SSM_GROUPS, SSM_HEADS_PER_GROUP, SSM_HEAD_DIM, SSM_STATE))

    return (yp.reshape(bp, lp, D_MODEL), ys.reshape(bs, ls, D_MODEL),
            kv_p[0].reshape(DEPTH, bp, lp, ATT_HEADS, 2, ATT_HEAD_DIM),
            kv_p[1].reshape(DEPTH, bp, lp, ATT_HEADS, ATT_V_DIM),
            jnp.stack(conv_p), jnp.stack(ssm_p),
            kv_s[0].reshape(DEPTH, bs, ls, ATT_HEADS, 2, ATT_HEAD_DIM),
            kv_s[1].reshape(DEPTH, bs, ls, ATT_HEADS, ATT_V_DIM),
            jnp.stack(conv_s), jnp.stack(ssm_s))
```

```python
import functools
import math

import jax
import jax.numpy as jnp
from jax import lax
from jax.experimental import pallas as pl
from jax.experimental.pallas import tpu as pltpu

F32 = jnp.float32
BF16 = jnp.bfloat16
LOG2E = math.log2(math.e)

D_MODEL = 1024
DEPTH = 2
CHUNK = 64
ATT_HEADS = 8
ATT_HEAD_DIM = 64
ATT_V_DIM = 128
ATT_WIDTH = 1024
Q_DIM = 1024
ATT_SUBLN_EPS = 1e-5
D_INNER = 2048
SSM_HEAD_DIM = 64
SSM_HEADS = 32
SSM_GROUPS = 8
SSM_HEADS_PER_GROUP = 4
SSM_STATE = 128
CONV_WIDTH = 4
CONV_DIM = 4096
SSM_NORM_EPS = 1e-5
D_FF = 2816
RMS_EPS = 1e-6
GROUP_WIDTH = SSM_HEADS_PER_GROUP * SSM_HEAD_DIM

LANES = 128
SUBLANES = 8
VMEM_LIMIT_BYTES = 56 * 1024 * 1024

PROJ_TN = 512
SSD_CHUNK = 256
SSD_PAD_ROWS = 128
CONV_PAD = 8
CONV_ROWS = 32
SAMPLE_TS = 1024
ATTN_TQ = 1024
ATTN_TK = 512
ATTN_UNROLL = 2


def _cparams(*sem):
    return pltpu.CompilerParams(dimension_semantics=sem, vmem_limit_bytes=VMEM_LIMIT_BYTES)


def _const_spec(shape):
    return pl.BlockSpec(shape, lambda *_: (0,) * len(shape), pipeline_mode=pl.Buffered(1))


def _rms(x, w, eps):
    return x * lax.rsqrt(jnp.mean(x * x, axis=-1, keepdims=True) + eps) * w


def _split3(x):
    hi = x.astype(BF16)
    r1 = x - hi.astype(F32)
    mid = r1.astype(BF16)
    lo = (r1 - mid.astype(F32)).astype(BF16)
    return hi, mid, lo


def _dot(a, b):
    return jnp.dot(a, b, preferred_element_type=F32)


def _dot_nt(a, b):
    return lax.dot_general(a, b, (((1,), (1,)), ((), ())), preferred_element_type=F32)


_NQ = Q_DIM // PROJ_TN
_NK = Q_DIM // PROJ_TN
_NV = ATT_WIDTH // PROJ_TN
_NZ = D_INNER // PROJ_TN
_NX = CONV_DIM // PROJ_TN
_NG = 2 * D_MODEL // PROJ_TN
_OFF_K = _NQ
_OFF_V = _OFF_K + _NK
_OFF_Z = _OFF_V + _NV
_OFF_X = _OFF_Z + _NZ
_OFF_G = _OFF_X + _NX
_N_COL_TILES = _OFF_G + _NG


def _in_proj_kernel(*refs, tm, fuse_conv, n_alias):
    x_ref, nw_ref, w_ref, wdt_ref = refs[:4]
    pos = 4
    if fuse_conv:
        cw_ref, cb_ref, hist0_ref = refs[pos:pos + 3]
        pos += 3
    pos += n_alias
    q_ref, k_ref, kb_ref, v_ref, vb_ref, z_ref, xbc_ref, g_ref, dt_ref = refs[pos:pos + 9]
    pos += 9
    if fuse_conv:
        tail_ref = refs[pos]
        pos += 1
    h_sc = refs[pos]
    i = pl.program_id(0)
    j = pl.program_id(1)

    @pl.when(j == 0)
    def _():
        hb = _rms(x_ref[...], nw_ref[...], RMS_EPS).astype(BF16)
        h_sc[...] = hb
        dt_ref[...] = _dot(hb, wdt_ref[...])

    def tile():
        return _dot(h_sc[...], w_ref[...])

    @pl.when(j < _OFF_K)
    def _():
        q_ref[...] = (tile() * (LOG2E * ATT_HEAD_DIM ** -0.5)).astype(BF16)

    @pl.when((j >= _OFF_K) & (j < _OFF_V))
    def _():
        res = tile()
        k_ref[...] = res
        kb_ref[...] = res.astype(BF16)

    @pl.when((j >= _OFF_V) & (j < _OFF_Z))
    def _():
        res = tile()
        v_ref[...] = res
        vb_ref[...] = res.astype(BF16)

    @pl.when((j >= _OFF_Z) & (j < _OFF_X))
    def _():
        z_ref[...] = tile().astype(z_ref.dtype)

    if not fuse_conv:
        @pl.when((j >= _OFF_X) & (j < _OFF_G))
        def _():
            xbc_ref[...] = tile()

        @pl.when(j >= _OFF_G)
        def _():
            g_ref[...] = tile().astype(g_ref.dtype)
        return

    hist_sc, raw_sc = refs[pos + 1], refs[pos + 2]

    @pl.when((i == 0) & (j == 0))
    def _():
        for c in range(_NX):
            hist_sc[c] = hist0_ref[:, c * PROJ_TN:(c + 1) * PROJ_TN]

    def activate(jc, slot):
        for r0 in range(0, tm, CONV_ROWS):
            before = hist_sc[jc] if r0 == 0 else raw_sc[slot, r0 - CONV_PAD:r0, :]
            xp = jnp.concatenate([before, raw_sc[slot, r0:r0 + CONV_ROWS, :]], axis=0)
            conv = cw_ref[0:1, :] * xp
            for tap in range(1, CONV_WIDTH):
                conv = cw_ref[tap:tap + 1, :] * xp + pltpu.roll(conv, 1, 0)
            conv = conv[CONV_PAD:] + cb_ref[...]
            xbc_ref[r0:r0 + CONV_ROWS, :] = (conv * jax.nn.sigmoid(conv)).astype(xbc_ref.dtype)
        last_rows = raw_sc[slot, tm - CONV_PAD:tm, :]
        hist_sc[jc] = last_rows
        tail_ref[...] = last_rows

    @pl.when(j == _OFF_X)
    def _():
        raw_sc[0] = tile()

    for slot in range(2):
        @pl.when((j > _OFF_X) & (j < _OFF_G) & (lax.rem(j - _OFF_X, 2) == slot))
        def _():
            activate(j - _OFF_X - 1, 1 - slot)
            raw_sc[slot] = tile()

    @pl.when(j == _OFF_G)
    def _():
        activate(_NX - 1, (_NX - 1) % 2)
        g_ref[...] = tile().astype(g_ref.dtype)

    @pl.when(j > _OFF_G)
    def _():
        g_ref[...] = tile().astype(g_ref.dtype)


def _in_proj(x, norm_w, w_main, w_dt, tm, conv=None, stacks=None, layer=0):
    t = x.shape[0]
    fuse_conv = conv is not None

    def col(off, n):
        return lambda i, j: (i, jnp.clip(j - off, 0, n - 1))

    def out_spec(off, n):
        return pl.BlockSpec((tm, PROJ_TN), col(off, n))

    def stack_spec(off, n):
        return pl.BlockSpec((None, tm, PROJ_TN), lambda i, j: (layer,) + col(off, n)(i, j))

    xbc_lag = 1 if fuse_conv else 0

    def xcol(i, j):
        return (0, jnp.clip(j - _OFF_X - xbc_lag, 0, _NX - 1))

    in_specs = [
        pl.BlockSpec((tm, D_MODEL), lambda i, j: (i, 0)),
        pl.BlockSpec((1, D_MODEL), lambda i, j: (0, 0)),
        pl.BlockSpec((None, D_MODEL, PROJ_TN), lambda i, j: (j, 0, 0)),
        pl.BlockSpec((D_MODEL, LANES), lambda i, j: (0, 0)),
    ]
    args = [x, norm_w, w_main, w_dt]
    if fuse_conv:
        in_specs += [pl.BlockSpec((CONV_WIDTH, PROJ_TN), xcol), pl.BlockSpec((1, PROJ_TN), xcol),
                     pl.BlockSpec((CONV_PAD, CONV_DIM), lambda i, j: (0, 0))]
        args += list(conv)
    aliases = {}
    if stacks is not None:
        in_specs += [pl.BlockSpec(memory_space=pl.ANY)] * 2
        aliases = {len(args): 1, len(args) + 1: 3}
        args += list(stacks)
    out_shape = [
        jax.ShapeDtypeStruct((t, Q_DIM), BF16),
        jax.ShapeDtypeStruct((DEPTH, t, Q_DIM), F32),
        jax.ShapeDtypeStruct((t, Q_DIM), BF16),
        jax.ShapeDtypeStruct((DEPTH, t, ATT_WIDTH), F32),
        jax.ShapeDtypeStruct((t, ATT_WIDTH), BF16),
        jax.ShapeDtypeStruct((t, D_INNER), BF16),
        jax.ShapeDtypeStruct((t, CONV_DIM), BF16 if fuse_conv else F32),
        jax.ShapeDtypeStruct((t, 2 * D_MODEL), BF16),
        jax.ShapeDtypeStruct((t, LANES), F32),
    ]
    out_specs = [
        out_spec(0, _NQ), stack_spec(_OFF_K, _NK), out_spec(_OFF_K, _NK),
        stack_spec(_OFF_V, _NV), out_spec(_OFF_V, _NV), out_spec(_OFF_Z, _NZ),
        out_spec(_OFF_X + xbc_lag, _NX), out_spec(_OFF_G, _NG),
        pl.BlockSpec((tm, LANES), lambda i, j: (i, 0)),
    ]
    scratch = [pltpu.VMEM((tm, D_MODEL), BF16)]
    if fuse_conv:
        out_shape.append(jax.ShapeDtypeStruct((t // tm, CONV_PAD, CONV_DIM), F32))
        out_specs.append(pl.BlockSpec((None, CONV_PAD, PROJ_TN), lambda i, j: (i,) + xcol(i, j)))
        scratch += [pltpu.VMEM((_NX, CONV_PAD, PROJ_TN), F32), pltpu.VMEM((2, tm, PROJ_TN), F32)]
    return pl.pallas_call(
        functools.partial(_in_proj_kernel, tm=tm, fuse_conv=fuse_conv, n_alias=len(aliases)),
        grid=(t // tm, _N_COL_TILES),
        in_specs=in_specs,
        out_specs=out_specs,
        out_shape=out_shape,
        scratch_shapes=scratch,
        input_output_aliases=aliases,
        compiler_params=_cparams("arbitrary", "arbitrary"),
        name="in_proj",
    )(*args)


def _stack_q(q):
    lane = lax.broadcasted_iota(jnp.int32, q.shape, 1)
    zero = jnp.zeros_like(q)
    return jnp.concatenate([jnp.where(lane < ATT_HEAD_DIM, q, zero),
                            jnp.where(lane >= ATT_HEAD_DIM, q, zero)], axis=0)


def _diff_epilogue(o, n, lamp_ref, subln_ref, lam_init):
    lp = lamp_ref[...]
    lam = (jnp.exp(jnp.sum(lp[0:1] * lp[1:2], axis=-1, keepdims=True))
           - jnp.exp(jnp.sum(lp[2:3] * lp[3:4], axis=-1, keepdims=True)) + lam_init)
    a = o[:n] - lam * o[n:]
    return _rms(a, subln_ref[...], ATT_SUBLN_EPS) * (1.0 - lam_init)


def _attn_prompt_kernel(lamp_ref, subln_ref, q_ref, k_ref, v_ref, o_ref, m_sc, l_sc, acc_sc, s_sc, *,
                        tq, tk, nq, unroll, lam_init):
    qi = pl.program_id(1)
    qs = _stack_q(q_ref[...])
    m_sc[...] = jnp.full(m_sc.shape, -jnp.inf, F32)
    l_sc[...] = jnp.zeros(l_sc.shape, F32)
    acc_sc[...] = jnp.zeros(acc_sc.shape, F32)
    n_lane_tiles = tk // LANES
    full_blocks = qi * (tq // tk)

    def scores(kblk, masked):
        start = pl.multiple_of(kblk * tk, tk)
        s = _dot_nt(qs, k_ref[pl.ds(start, tk), :])
        if masked:
            r = lax.broadcasted_iota(jnp.int32, s.shape, 0)
            c = lax.broadcasted_iota(jnp.int32, s.shape, 1)
            q_chunk = jnp.where(r >= tq, r - tq, r) // CHUNK
            k_chunk = c // CHUNK + (kblk - full_blocks) * (tk // CHUNK)
            s = jnp.where(k_chunk <= q_chunk, s, -jnp.inf)
        return s

    def absorb(kblk, s, state):
        m_prev, l_prev, acc_prev = state
        vb = v_ref[pl.ds(pl.multiple_of(kblk * tk, tk), tk), :]
        m_new = jnp.maximum(m_prev, jnp.max(s, axis=-1, keepdims=True))
        alpha = jnp.exp2(m_prev - m_new)
        p = jnp.exp2(s - jnp.concatenate([m_new] * n_lane_tiles, axis=1))
        pv = _dot(p.astype(BF16), jnp.concatenate([vb, jnp.ones((tk, LANES), BF16)], axis=1))
        return m_new, alpha * l_prev + pv[:, ATT_V_DIM:], alpha * acc_prev + pv[:, :ATT_V_DIM]

    PLAIN, MASKED = "plain", "masked"

    def pipeline(first, produce):
        state = (m_sc[...], l_sc[...], acc_sc[...])
        s = s_sc[...]
        for i, kind in enumerate(produce):
            s_next = None if kind is None else scores(first + i + 1, kind == MASKED)
            state = absorb(first + i, s, state)
            s = s_next
        if s is not None:
            s_sc[...] = s
        m_sc[...], l_sc[...], acc_sc[...] = state

    diag_blocks = tq // tk

    @pl.when(qi == 0)
    def _():
        s_sc[...] = scores(0, True)
        pipeline(0, [MASKED] * (diag_blocks - 1) + [None])

    @pl.when(qi > 0)
    def _():
        s_sc[...] = scores(0, False)

    def body(t, carry):
        pipeline(unroll * t, [PLAIN] * unroll)
        return carry

    plain_steps = jnp.maximum(full_blocks - 1, 0)
    lax.fori_loop(0, plain_steps // unroll, body, 0)
    for r in sorted({(diag_blocks * q - 1) % unroll for q in range(1, nq)}):
        @pl.when((qi > 0) & (plain_steps % unroll == r))
        def _():
            pipeline(full_blocks - 1 - r, [PLAIN] * r + [MASKED] * diag_blocks + [None])

    o = acc_sc[...] * (1.0 / l_sc[...])
    o_ref[...] = _diff_epilogue(o, tq, lamp_ref, subln_ref, lam_init).astype(o_ref.dtype)


def _attn_prompt(q, kb, vb, lamp, subln, lam_init, tq, tk, unroll):
    t = q.shape[0]
    assert t % tq == 0 and tq % tk == 0 and tk % LANES == 0 and tk % CHUNK == 0
    kern = functools.partial(_attn_prompt_kernel, tq=tq, tk=tk, nq=t // tq, unroll=unroll, lam_init=lam_init)
    return pl.pallas_call(
        kern,
        grid=(ATT_HEADS, t // tq),
        in_specs=[
            pl.BlockSpec((4, ATT_HEAD_DIM), lambda h, i: (0, 0)),
            pl.BlockSpec((1, ATT_V_DIM), lambda h, i: (0, 0)),
            pl.BlockSpec((tq, LANES), lambda h, i: (i, h)),
            pl.BlockSpec((t, LANES), lambda h, i: (0, h)),
            pl.BlockSpec((t, LANES), lambda h, i: (0, h)),
        ],
        out_specs=pl.BlockSpec((tq, LANES), lambda h, i: (i, h)),
        out_shape=jax.ShapeDtypeStruct((t, ATT_WIDTH), BF16),
        scratch_shapes=[pltpu.VMEM((2 * tq, LANES), F32), pltpu.VMEM((2 * tq, LANES), F32),
                        pltpu.VMEM((2 * tq, ATT_V_DIM), F32), pltpu.VMEM((2 * tq, tk), F32)],
        compiler_params=_cparams("arbitrary", "arbitrary"),
        name="attn_prompt",
    )(lamp, subln, q, kb, vb)


def _attn_sample_kernel(lamp_ref, subln_ref, q_ref, kt_ref, vc_ref, kn_ref, vn_ref, o_ref, m_sc, l_sc, acc_sc, *,
                        n, ts, lam_init):
    piece = pl.program_id(1)

    @pl.when(piece == 0)
    def _():
        m_sc[...] = jnp.full(m_sc.shape, -jnp.inf, F32)
        l_sc[...] = jnp.zeros(l_sc.shape, F32)
        acc_sc[...] = jnp.zeros(acc_sc.shape, F32)

    def update(h, s, v):
        width = s.shape[1]
        m_prev = m_sc[h]
        m_new = jnp.maximum(m_prev, jnp.max(s, axis=-1, keepdims=True))
        alpha = jnp.exp2(m_prev - m_new)
        m_wide = m_new[:, :width] if width <= LANES else jnp.concatenate([m_new] * (width // LANES), axis=1)
        p = jnp.exp2(s - m_wide)
        l_sc[h] = alpha * l_sc[h] + jnp.sum(p, axis=-1, keepdims=True)
        acc_sc[h] = alpha * acc_sc[h] + _dot(p.astype(BF16), v)
        m_sc[h] = m_new

    for h in range(ATT_HEADS):
        hs = slice(h * LANES, (h + 1) * LANES)
        qs = _stack_q(q_ref[0, :, hs])
        s = _dot(qs, kt_ref[0, 0, hs, :].astype(BF16))
        update(h, s, vc_ref[0, 0, pl.ds(h, ts, stride=ATT_HEADS), :].astype(BF16))

    @pl.when(piece == pl.num_programs(1) - 1)
    def _():
        for h in range(ATT_HEADS):
            hs = slice(h * LANES, (h + 1) * LANES)
            qs = _stack_q(q_ref[0, :, hs])
            update(h, _dot_nt(qs, kn_ref[0, :, hs]), vn_ref[0, :, hs])
            o = acc_sc[h] * (1.0 / l_sc[h])
            o_ref[0, :, hs] = _diff_epilogue(o, n, lamp_ref, subln_ref, lam_init).astype(o_ref.dtype)


def _attn_sample(q, cache_kt, cache_v, kb, vb, lamp, subln, lam_init, layer):
    b, n, _ = q.shape
    past = cache_kt.shape[3]
    ts = min(past, SAMPLE_TS)
    assert past % CHUNK == 0 and n <= CHUNK and past % ts == 0
    kern = functools.partial(_attn_sample_kernel, n=n, ts=ts, lam_init=lam_init)
    row_spec = pl.BlockSpec((1, n, Q_DIM), lambda bi, s: (bi, 0, 0))
    return pl.pallas_call(
        kern,
        grid=(b, past // ts),
        in_specs=[
            pl.BlockSpec((4, ATT_HEAD_DIM), lambda bi, s: (0, 0)),
            pl.BlockSpec((1, ATT_V_DIM), lambda bi, s: (0, 0)),
            row_spec,
            pl.BlockSpec((1, 1, Q_DIM, ts), lambda bi, s: (layer, bi, 0, s)),
            pl.BlockSpec((1, 1, ts * ATT_HEADS, ATT_V_DIM), lambda bi, s: (layer, bi, s, 0)),
            row_spec,
            row_spec,
        ],
        out_specs=row_spec,
        out_shape=jax.ShapeDtypeStruct((b, n, ATT_WIDTH), BF16),
        scratch_shapes=[pltpu.VMEM((ATT_HEADS, 2 * n, LANES), F32)] * 3,
        compiler_params=_cparams("arbitrary", "arbitrary"),
        name="attn_sample",
    )(lamp, subln, q, cache_kt, cache_v, kb, vb)


def _ssd_kernel(*refs, lc, valid_last, conv_done):
    if conv_done:
        (xbc_ref, z_ref, dt_ref, st0_ref, dtb_ref, alog_ref, dsk_ref, nrm_ref, expand_ref,
         y_ref, sout_ref, st_sc) = refs
    else:
        (xbc_ref, z_ref, dt_ref, st0_ref, dtb_ref, alog_ref, dsk_ref, nrm_ref, expand_ref,
         cbuf_ref, cw_ref, cb_ref, y_ref, sout_ref, cout_ref, st_sc, xpad_sc) = refs
    c = pl.program_id(1)
    nchunks = pl.num_programs(1)

    @pl.when(c == 0)
    def _():
        for g in range(SSM_GROUPS):
            st_sc[g] = st0_ref[0, g].T

    if conv_done:
        act = xbc_ref[0]
    else:
        @pl.when(c == 0)
        def _():
            xpad_sc[0:CONV_PAD, :] = jnp.zeros((CONV_PAD, CONV_DIM), F32)
            xpad_sc[CONV_PAD - (CONV_WIDTH - 1):CONV_PAD, :] = cbuf_ref[0]

        xpad_sc[CONV_PAD:CONV_PAD + lc, :] = xbc_ref[0]
        xp = xpad_sc[...]
        conv = cw_ref[0:1, :] * xp
        for j in range(1, CONV_WIDTH):
            conv = cw_ref[j:j + 1, :] * xp + pltpu.roll(conv, 1, 0)
        conv = conv[CONV_PAD:CONV_PAD + lc] + cb_ref[...]
        act = conv * jax.nn.sigmoid(conv)

        @pl.when(c == nchunks - 1)
        def _():
            end = CONV_PAD + valid_last
            cout_ref[0] = xpad_sc[end - (CONV_WIDTH - 1):end, :]

        xpad_sc[0:CONV_PAD, :] = xpad_sc[lc:lc + CONV_PAD, :]

    dtr = dt_ref[0] + dtb_ref[...]
    dt = jnp.maximum(dtr, 0.0) + jnp.log1p(jnp.exp(-jnp.abs(dtr)))
    if valid_last < lc:
        row = lax.broadcasted_iota(jnp.int32, dt.shape, 0)
        dt = jnp.where(row < valid_last, dt, 0.0)
    a = dt * (-jnp.exp(alog_ref[...]))
    rr = lax.broadcasted_iota(jnp.int32, (lc, lc), 0)
    cc = lax.broadcasted_iota(jnp.int32, (lc, lc), 1)
    causal = rr >= cc
    tri = jnp.where(causal, 1.0, 0.0).astype(BF16)
    a_cs = sum(_dot(tri, part) for part in _split3(a))
    a_last = a_cs[lc - 1:lc, :]
    a_cs_t = a_cs.T

    expand = expand_ref[...]
    stack = jnp.concatenate([dt, dt * jnp.exp(a_last - a_cs), jnp.exp(a_cs)], axis=0)
    ex = _dot(stack.astype(BF16), expand)
    w_dt = ex[0:lc]
    w_state = ex[lc:2 * lc]
    w_off = ex[2 * lc:3 * lc]
    carry = jnp.broadcast_to(jnp.exp(a_last), (2 * SUBLANES, LANES))
    w_carry = sum(_dot(part, expand) for part in _split3(carry))[0:1]

    xs = act[:, :D_INNER].astype(F32)
    xdt = xs * w_dt
    xst = xs * w_state
    lane = lax.broadcasted_iota(jnp.int32, (lc, LANES), 1)
    lo_half = lane < SSM_HEAD_DIM

    for g in range(SSM_GROUPS):
        gs = slice(g * GROUP_WIDTH, (g + 1) * GROUP_WIDTH)
        bg = act[:, D_INNER + g * SSM_STATE:D_INNER + (g + 1) * SSM_STATE]
        cg = act[:, D_INNER + SSM_GROUPS * SSM_STATE + g * SSM_STATE:
                 D_INNER + SSM_GROUPS * SSM_STATE + (g + 1) * SSM_STATE]
        bgb = bg.astype(BF16)
        cgb = cg.astype(BF16)
        cb = _dot_nt(cgb, bgb)
        st_prev = st_sc[g]
        y_off = _dot(cgb, st_prev.astype(BF16)) * w_off[:, gs]
        st_sc[g] = w_carry[:, gs] * st_prev + _dot(bg.astype(F32).T.astype(BF16), xst[:, gs].astype(BF16))

        pairs = []
        for pr in range(SSM_HEADS_PER_GROUP // 2):
            x_pair = xdt[:, g * GROUP_WIDTH + pr * LANES:g * GROUP_WIDTH + (pr + 1) * LANES]
            y_pair = None
            for half in range(2):
                h = g * SSM_HEADS_PER_GROUP + pr * 2 + half
                seg = a_cs[:, h:h + 1] - a_cs_t[h:h + 1, :]
                decay = jnp.exp(jnp.where(causal, seg, -jnp.inf))
                mix = (cb * decay).astype(BF16)
                keep = lo_half if half == 0 else jnp.logical_not(lo_half)
                contrib = _dot(mix, jnp.where(keep, x_pair, 0.0).astype(BF16))
                y_pair = contrib if y_pair is None else y_pair + contrib
            pairs.append(y_pair)
        y_g = jnp.concatenate(pairs, axis=1) + y_off + dsk_ref[:, gs] * xs[:, gs]

        zg = z_ref[0, :, gs].astype(F32)
        yz = y_g * (zg * jax.nn.sigmoid(zg))
        y_ref[0, :, gs] = _rms(yz, nrm_ref[:, gs], SSM_NORM_EPS).astype(y_ref.dtype)

    @pl.when(c == nchunks - 1)
    def _():
        for g in range(SSM_GROUPS):
            sout_ref[0, g] = st_sc[g].T


def _ssd(xbc, z, dt, state0, p, lc, valid_last, conv_buf=None):
    b, l, _ = xbc.shape
    conv_done = conv_buf is None
    assert l % lc == 0 and CONV_WIDTH - 1 <= valid_last <= lc
    assert valid_last == lc or l == lc
    state_spec = pl.BlockSpec((1, SSM_GROUPS, GROUP_WIDTH, SSM_STATE), lambda bi, c: (bi, 0, 0, 0))
    in_specs = [
        pl.BlockSpec((1, lc, CONV_DIM), lambda bi, c: (bi, c, 0)),
        pl.BlockSpec((1, lc, D_INNER), lambda bi, c: (bi, c, 0)),
        pl.BlockSpec((1, lc, LANES), lambda bi, c: (bi, c, 0)),
        state_spec,
        pl.BlockSpec((1, LANES), lambda bi, c: (0, 0)),
        pl.BlockSpec((1, LANES), lambda bi, c: (0, 0)),
        pl.BlockSpec((1, D_INNER), lambda bi, c: (0, 0)),
        pl.BlockSpec((1, D_INNER), lambda bi, c: (0, 0)),
        pl.BlockSpec((LANES, D_INNER), lambda bi, c: (0, 0)),
    ]
    args = [xbc, z, dt, state0, p["dt_bias"], p["a_log"], p["d_skip"], p["ssm_norm"], p["expand"]]
    out_specs = [pl.BlockSpec((1, lc, D_INNER), lambda bi, c: (bi, c, 0)), state_spec]
    out_shape = [jax.ShapeDtypeStruct((b, l, D_INNER), BF16),
                 jax.ShapeDtypeStruct((b, SSM_GROUPS, GROUP_WIDTH, SSM_STATE), F32)]
    scratch = [pltpu.VMEM((SSM_GROUPS, SSM_STATE, GROUP_WIDTH), F32)]
    if not conv_done:
        conv_spec = pl.BlockSpec((1, CONV_WIDTH - 1, CONV_DIM), lambda bi, c: (bi, 0, 0))
        in_specs += [conv_spec, pl.BlockSpec((CONV_WIDTH, CONV_DIM), lambda bi, c: (0, 0)),
                     pl.BlockSpec((1, CONV_DIM), lambda bi, c: (0, 0))]
        args += [conv_buf, p["conv_w"], p["conv_b"]]
        out_specs.append(conv_spec)
        out_shape.append(jax.ShapeDtypeStruct((b, CONV_WIDTH - 1, CONV_DIM), F32))
        scratch.append(pltpu.VMEM((lc + CONV_PAD, CONV_DIM), F32))
    return pl.pallas_call(
        functools.partial(_ssd_kernel, lc=lc, valid_last=valid_last, conv_done=conv_done),
        grid=(b, l // lc),
        in_specs=in_specs,
        out_specs=out_specs,
        out_shape=out_shape,
        scratch_shapes=scratch,
        compiler_params=_cparams("arbitrary", "arbitrary"),
        name="ssd",
    )(*args)


def _merge_kernel(x_ref, att_ref, y_ref, g_ref, wa_ref, ws_ref, wo_ref, o_ref):
    ba = _dot(att_ref[...], wa_ref[...])
    bs = _dot(y_ref[...], ws_ref[...])
    g = jax.nn.sigmoid(g_ref[...].astype(F32))
    merged = g[:, :D_MODEL] * ba + g[:, D_MODEL:] * bs
    o_ref[...] = x_ref[...] + _dot(merged.astype(BF16), wo_ref[...])


def _merge(x, att, y, gates, wa, ws, wo, tm):
    t = x.shape[0]
    return pl.pallas_call(
        _merge_kernel,
        grid=(t // tm,),
        in_specs=[
            pl.BlockSpec((tm, D_MODEL), lambda i: (i, 0)),
            pl.BlockSpec((tm, ATT_WIDTH), lambda i: (i, 0)),
            pl.BlockSpec((tm, D_INNER), lambda i: (i, 0)),
            pl.BlockSpec((tm, 2 * D_MODEL), lambda i: (i, 0)),
            _const_spec((ATT_WIDTH, D_MODEL)),
            _const_spec((D_INNER, D_MODEL)),
            _const_spec((D_MODEL, D_MODEL)),
        ],
        out_specs=pl.BlockSpec((tm, D_MODEL), lambda i: (i, 0)),
        out_shape=jax.ShapeDtypeStruct((t, D_MODEL), F32),
        compiler_params=_cparams("arbitrary"),
        name="merge",
    )(x, att, y, gates, wa, ws, wo)


FFN_SPLIT = 2
FFN_TF = D_FF // FFN_SPLIT


def _ffn_kernel(x_ref, nw_ref, wgu_ref, wd_ref, nf_ref, *out_refs, final_norm):
    x = x_ref[...]
    hb = _rms(x, nw_ref[...], RMS_EPS).astype(BF16)
    acc = x
    for c in range(FFN_SPLIT):
        gt = _dot(hb, wgu_ref[:, c * FFN_TF:(c + 1) * FFN_TF])
        up = _dot(hb, wgu_ref[:, D_FF + c * FFN_TF:D_FF + (c + 1) * FFN_TF])
        act = (gt * jax.nn.sigmoid(gt) * up).astype(BF16)
        acc = acc + _dot(act, wd_ref[c * FFN_TF:(c + 1) * FFN_TF, :])
    out_refs[0][...] = acc
    if final_norm:
        out_refs[1][...] = _rms(acc, nf_ref[...], RMS_EPS)


def _ffn(x, norm_w, wgu, wd, norm_final, final_norm, tm):
    t = x.shape[0]
    row_spec = pl.BlockSpec((tm, D_MODEL), lambda i: (i, 0))
    n_out = 2 if final_norm else 1
    return pl.pallas_call(
        functools.partial(_ffn_kernel, final_norm=final_norm),
        grid=(t // tm,),
        in_specs=[
            row_spec,
            pl.BlockSpec((1, D_MODEL), lambda i: (0, 0)),
            _const_spec((D_MODEL, 2 * D_FF)),
            _const_spec((D_FF, D_MODEL)),
            pl.BlockSpec((1, D_MODEL), lambda i: (0, 0)),
        ],
        out_specs=(row_spec,) * n_out,
        out_shape=(jax.ShapeDtypeStruct((t, D_MODEL), F32),) * n_out,
        compiler_params=_cparams("arbitrary"),
        name="ffn",
    )(x, norm_w, wgu, wd, norm_final)


def _layer_params(l, norm_mix, w_in, lambda_q1, lambda_k1, lambda_q2, lambda_k2, attn_subln, conv_w, conv_b,
                  dt_bias, a_log, d_skip, ssm_norm, w_branch_att, w_branch_ssd, w_out, norm_ffn, w_gate_up,
                  w_down):
    w = w_in[l]
    n_main = Q_DIM + Q_DIM + ATT_WIDTH + D_INNER + CONV_DIM
    w_main = jnp.concatenate([w[:, :n_main], w[:, n_main + SSM_HEADS:]], axis=1)
    w_main = w_main.reshape(D_MODEL, _N_COL_TILES, PROJ_TN).transpose(1, 0, 2)
    w_dt = jnp.pad(w[:, n_main:n_main + SSM_HEADS], ((0, 0), (0, LANES - SSM_HEADS)))
    head_of_channel = jnp.arange(D_INNER) // SSM_HEAD_DIM
    return {
        "norm_mix": norm_mix[l][None], "w_main": w_main, "w_dt": w_dt,
        "lamp": jnp.stack([lambda_q1[l], lambda_k1[l], lambda_q2[l], lambda_k2[l]]),
        "subln": attn_subln[l][None],
        "conv_w": conv_w[l], "conv_b": conv_b[l][None],
        "dt_bias": jnp.pad(dt_bias[l], (0, LANES - SSM_HEADS))[None],
        "a_log": jnp.pad(a_log[l], (0, LANES - SSM_HEADS))[None],
        "d_skip": jnp.repeat(d_skip[l], SSM_HEAD_DIM)[None],
        "ssm_norm": ssm_norm[l][None],
        "expand": (jnp.arange(LANES)[:, None] == head_of_channel[None, :]).astype(BF16),
        "wa": w_branch_att[l].astype(BF16), "ws": w_branch_ssd[l].astype(BF16), "wo": w_out[l].astype(BF16),
        "norm_ffn": norm_ffn[l][None], "wgu": w_gate_up[l].astype(BF16), "wd": w_down[l].astype(BF16),
    }


def _pad_time(x, b, n, lc):
    return jnp.pad(x.reshape(b, n, x.shape[-1]), ((0, 0), (0, lc - n), (0, 0)))


def kernel(x_prompt, x_sample, cache_k, cache_v, state_conv, state_ssm, norm_mix, w_in, lambda_q1, lambda_k1,
           lambda_q2, lambda_k2, attn_subln, conv_w, conv_b, dt_bias, a_log, d_skip, ssm_norm, w_branch_att,
           w_branch_ssd, w_out, norm_ffn, w_gate_up, w_down, norm_final):
    bp, lp, _ = x_prompt.shape
    bs, ls, _ = x_sample.shape
    assert bp == 1
    past = cache_k.shape[2]
    tp, ts = bp * lp, bs * ls
    xp = x_prompt.reshape(tp, D_MODEL)
    xs = x_sample.reshape(ts, D_MODEL)
    cache_k = jnp.transpose(cache_k, (0, 1, 3, 4, 5, 2)).reshape(DEPTH, bs, Q_DIM, past)
    cache_v = cache_v.reshape(DEPTH, bs, past * ATT_HEADS, ATT_V_DIM)
    state_ssm = state_ssm.reshape(DEPTH, bs, SSM_GROUPS, GROUP_WIDTH, SSM_STATE)
    conv0 = jnp.zeros((CONV_PAD, CONV_DIM), F32)
    ssm0 = jnp.zeros((bp, SSM_GROUPS, GROUP_WIDTH, SSM_STATE), F32)
    nf = norm_final[None]

    tm_p = 1024 if tp % 1024 == 0 else 128
    tm_row = 512 if tp % 512 == 0 else 128
    tq = ATTN_TQ if lp % ATTN_TQ == 0 else CHUNK
    tk = ATTN_TK if tq % ATTN_TK == 0 else tq

    w_in = w_in.astype(BF16)
    conv_p, ssm_p, conv_s, ssm_s = [], [], [], []
    kv_p = kv_s = None
    yp = ys = None
    for l in range(DEPTH):
        p = _layer_params(l, norm_mix, w_in, lambda_q1, lambda_k1, lambda_q2, lambda_k2, attn_subln, conv_w,
                          conv_b, dt_bias, a_log, d_skip, ssm_norm, w_branch_att, w_branch_ssd, w_out,
                          norm_ffn, w_gate_up, w_down)
        lam_init = 0.8 - 0.6 * math.exp(-0.3 * l)
        last = l == DEPTH - 1

        q, k_all, kb, v_all, vb, z, act, gates, dt, tail = _in_proj(
            xp, p["norm_mix"], p["w_main"], p["w_dt"], tm_p, conv=(p["conv_w"], p["conv_b"], conv0),
            stacks=kv_p, layer=l)
        kv_p = (k_all, v_all)
        att = _attn_prompt(q, kb, vb, p["lamp"], p["subln"], lam_init, tq, tk, ATTN_UNROLL)
        y, s_new = _ssd(act.reshape(bp, lp, CONV_DIM), z.reshape(bp, lp, D_INNER), dt.reshape(bp, lp, LANES),
                        ssm0, p, SSD_CHUNK, SSD_CHUNK)
        x1 = _merge(xp, att, y.reshape(tp, D_INNER), gates, p["wa"], p["ws"], p["wo"], tm_row)
        res = _ffn(x1, p["norm_ffn"], p["wgu"], p["wd"], nf, last, tm_row)
        xp = res[0]
        if last:
            yp = res[1]
        conv_p.append(tail[-1, CONV_PAD - (CONV_WIDTH - 1):][None])
        ssm_p.append(s_new.reshape(bp, SSM_GROUPS, SSM_HEADS_PER_GROUP, SSM_HEAD_DIM, SSM_STATE))

        q, k_all, kb, v_all, vb, z, xbc, gates, dt = _in_proj(xs, p["norm_mix"], p["w_main"], p["w_dt"], ts,
                                                              stacks=kv_s, layer=l)
        kv_s = (k_all, v_all)
        att = _attn_sample(q.reshape(bs, ls, Q_DIM), cache_k, cache_v, kb.reshape(bs, ls, Q_DIM),
                           vb.reshape(bs, ls, ATT_WIDTH), p["lamp"], p["subln"], lam_init, l)
        y, s_new, c_new = _ssd(_pad_time(xbc, bs, ls, SSD_PAD_ROWS), _pad_time(z, bs, ls, SSD_PAD_ROWS),
                               _pad_time(dt, bs, ls, SSD_PAD_ROWS), state_ssm[l], p, SSD_PAD_ROWS, ls,
                               conv_buf=state_conv[l])
        x1 = _merge(xs, att.reshape(ts, ATT_WIDTH), y[:, :ls].reshape(ts, D_INNER), gates,
                    p["wa"], p["ws"], p["wo"], ts)
        res = _ffn(x1, p["norm_ffn"], p["wgu"], p["wd"], nf, last, ts)
        xs = res[0]
        if last:
            ys = res[1]
        conv_s.append(c_new)
        ssm_s.append(s_new.reshape(bs, SSM_GROUPS, SSM_HEADS_PER_GROUP, SSM_HEAD_DIM, SSM_STATE))

    return (yp.reshape(bp, lp, D_MODEL), ys.reshape(bs, ls, D_MODEL),
            kv_p[0].reshape(DEPTH, bp, lp, ATT_HEADS, 2, ATT_HEAD_DIM),
            kv_p[1].reshape(DEPTH, bp, lp, ATT_HEADS, ATT_V_DIM),
            jnp.stack(conv_p), jnp.stack(ssm_p),
            kv_s[0].reshape(DEPTH, bs, ls, ATT_HEADS, 2, ATT_HEAD_DIM),
            kv_s[1].reshape(DEPTH, bs, ls, ATT_HEADS, ATT_V_DIM),
            jnp.stack(conv_s), jnp.stack(ssm_s))
```

```python
import functools
import math

import jax
import jax.numpy as jnp
from jax import lax
from jax.experimental import pallas as pl
from jax.experimental.pallas import tpu as pltpu

F32 = jnp.float32
BF16 = jnp.bfloat16
LOG2E = math.log2(math.e)

D_MODEL = 1024
DEPTH = 2
CHUNK = 64
ATT_HEADS = 8
ATT_HEAD_DIM = 64
ATT_V_DIM = 128
ATT_WIDTH = 1024
Q_DIM = 1024
ATT_SUBLN_EPS = 1e-5
D_INNER = 2048
SSM_HEAD_DIM = 64
SSM_HEADS = 32
SSM_GROUPS = 8
SSM_HEADS_PER_GROUP = 4
SSM_STATE = 128
CONV_WIDTH = 4
CONV_DIM = 4096
SSM_NORM_EPS = 1e-5
D_FF = 2816
RMS_EPS = 1e-6
GROUP_WIDTH = SSM_HEADS_PER_GROUP * SSM_HEAD_DIM

LANES = 128
SUBLANES = 8
VMEM_LIMIT_BYTES = 56 * 1024 * 1024

PROJ_TN = 512
SSD_CHUNK = 256
SSD_PAD_ROWS = 128
CONV_PAD = 8
CONV_ROWS = 32
CONV_COLS = 512
SAMPLE_TS = 1024
ATTN_TQ = 1024
ATTN_TK = 512
ATTN_UNROLL = 2


def _cparams(*sem):
    return pltpu.CompilerParams(dimension_semantics=sem, vmem_limit_bytes=VMEM_LIMIT_BYTES)


def _const_spec(shape):
    return pl.BlockSpec(shape, lambda *_: (0,) * len(shape), pipeline_mode=pl.Buffered(1))


def _rms(x, w, eps):
    return x * lax.rsqrt(jnp.mean(x * x, axis=-1, keepdims=True) + eps) * w


def _split3(x):
    hi = x.astype(BF16)
    r1 = x - hi.astype(F32)
    mid = r1.astype(BF16)
    lo = (r1 - mid.astype(F32)).astype(BF16)
    return hi, mid, lo


def _dot(a, b):
    return jnp.dot(a, b, preferred_element_type=F32)


def _dot_nt(a, b):
    return lax.dot_general(a, b, (((1,), (1,)), ((), ())), preferred_element_type=F32)


_NZ = D_INNER // PROJ_TN
_NX = CONV_DIM // PROJ_TN
_NG = 2 * D_MODEL // PROJ_TN
_OFF_X = _NZ
_OFF_G = _OFF_X + _NX
_N_COL_TILES = _OFF_G + _NG


def _qkv_kernel(*refs):
    x_ref, nw_ref, w_ref = refs[:3]
    q_ref, k_ref, kb_ref, v_ref, vb_ref, h_sc = refs[-6:]
    j = pl.program_id(1)

    @pl.when(j == 0)
    def _():
        h_sc[...] = _rms(x_ref[...], nw_ref[...], RMS_EPS).astype(BF16)
        q_ref[...] = (_dot(h_sc[...], w_ref[...]) * (LOG2E * ATT_HEAD_DIM ** -0.5)).astype(BF16)

    @pl.when(j == 1)
    def _():
        res = _dot(h_sc[...], w_ref[...])
        k_ref[...] = res
        kb_ref[...] = res.astype(BF16)

    @pl.when(j == 2)
    def _():
        res = _dot(h_sc[...], w_ref[...])
        v_ref[...] = res
        vb_ref[...] = res.astype(BF16)


def _qkv_proj(x, norm_w, w_qkv, tm, stacks=None, layer=0):
    t = x.shape[0]
    row = pl.BlockSpec((tm, Q_DIM), lambda i, j: (i, 0))
    stack = pl.BlockSpec((None, tm, Q_DIM), lambda i, j: (layer, i, 0))
    in_specs = [pl.BlockSpec((tm, D_MODEL), lambda i, j: (i, 0)), pl.BlockSpec((1, D_MODEL), lambda i, j: (0, 0)),
                pl.BlockSpec((D_MODEL, Q_DIM), lambda i, j: (0, j))]
    args = [x, norm_w, w_qkv]
    aliases = {}
    if stacks is not None:
        in_specs += [pl.BlockSpec(memory_space=pl.ANY)] * 2
        aliases = {len(args): 1, len(args) + 1: 3}
        args += list(stacks)
    return pl.pallas_call(
        _qkv_kernel,
        grid=(t // tm, 3),
        in_specs=in_specs,
        out_specs=[row, stack, row, stack, row],
        out_shape=[jax.ShapeDtypeStruct((t, Q_DIM), BF16),
                   jax.ShapeDtypeStruct((DEPTH, t, Q_DIM), F32),
                   jax.ShapeDtypeStruct((t, Q_DIM), BF16),
                   jax.ShapeDtypeStruct((DEPTH, t, ATT_WIDTH), F32),
                   jax.ShapeDtypeStruct((t, ATT_WIDTH), BF16)],
        scratch_shapes=[pltpu.VMEM((tm, D_MODEL), BF16)],
        input_output_aliases=aliases,
        compiler_params=_cparams("arbitrary", "arbitrary"),
        name="qkv_proj",
    )(*args)


def _in_proj_kernel(*refs, tm, fuse_conv):
    x_ref, nw_ref, w_ref, wdt_ref = refs[:4]
    pos = 4
    if fuse_conv:
        cw_ref, cb_ref, hist0_ref = refs[pos:pos + 3]
        pos += 3
    z_ref, xbc_ref, g_ref, dt_ref = refs[pos:pos + 4]
    pos += 4
    if fuse_conv:
        tail_ref = refs[pos]
        pos += 1
    h_sc = refs[pos]
    i = pl.program_id(0)
    j = pl.program_id(1)

    @pl.when(j == 0)
    def _():
        hb = _rms(x_ref[...], nw_ref[...], RMS_EPS).astype(BF16)
        h_sc[...] = hb
        dt_ref[...] = _dot(hb, wdt_ref[...])

    def tile():
        return _dot(h_sc[...], w_ref[...])

    @pl.when(j < _OFF_X)
    def _():
        z_ref[...] = tile().astype(z_ref.dtype)

    if not fuse_conv:
        @pl.when((j >= _OFF_X) & (j < _OFF_G))
        def _():
            xbc_ref[...] = tile()

        @pl.when(j >= _OFF_G)
        def _():
            g_ref[...] = tile().astype(g_ref.dtype)
        return

    hist_sc, raw_sc = refs[pos + 1], refs[pos + 2]

    @pl.when((i == 0) & (j == 0))
    def _():
        for c in range(_NX):
            hist_sc[c] = hist0_ref[:, c * PROJ_TN:(c + 1) * PROJ_TN]

    def activate(jc, slot):
        for c0 in range(0, PROJ_TN, CONV_COLS):
            cs = slice(c0, c0 + CONV_COLS)
            for r0 in range(0, tm, CONV_ROWS):
                before = hist_sc[jc, :, cs] if r0 == 0 else raw_sc[slot, r0 - CONV_PAD:r0, cs]
                xp = jnp.concatenate([before, raw_sc[slot, r0:r0 + CONV_ROWS, cs]], axis=0)
                conv = cw_ref[0:1, cs] * xp
                for tap in range(1, CONV_WIDTH):
                    conv = cw_ref[tap:tap + 1, cs] * xp + pltpu.roll(conv, 1, 0)
                conv = conv[CONV_PAD:] + cb_ref[:, cs]
                xbc_ref[r0:r0 + CONV_ROWS, cs] = (conv * jax.nn.sigmoid(conv)).astype(xbc_ref.dtype)
        last_rows = raw_sc[slot, tm - CONV_PAD:tm, :]
        hist_sc[jc] = last_rows
        tail_ref[...] = last_rows

    @pl.when(j == _OFF_X)
    def _():
        raw_sc[0] = tile()

    for slot in range(2):
        @pl.when((j > _OFF_X) & (j < _OFF_G) & (lax.rem(j - _OFF_X, 2) == slot))
        def _():
            activate(j - _OFF_X - 1, 1 - slot)
            raw_sc[slot] = tile()

    @pl.when(j == _OFF_G)
    def _():
        activate(_NX - 1, (_NX - 1) % 2)
        g_ref[...] = tile().astype(g_ref.dtype)

    @pl.when(j > _OFF_G)
    def _():
        g_ref[...] = tile().astype(g_ref.dtype)


def _in_proj(x, norm_w, w_main, w_dt, tm, conv=None):
    t = x.shape[0]
    fuse_conv = conv is not None

    def out_spec(off, n):
        return pl.BlockSpec((tm, PROJ_TN), lambda i, j: (i, jnp.clip(j - off, 0, n - 1)))

    xbc_lag = 1 if fuse_conv else 0

    def xcol(i, j):
        return (0, jnp.clip(j - _OFF_X - xbc_lag, 0, _NX - 1))

    in_specs = [
        pl.BlockSpec((tm, D_MODEL), lambda i, j: (i, 0)),
        pl.BlockSpec((1, D_MODEL), lambda i, j: (0, 0)),
        pl.BlockSpec((D_MODEL, PROJ_TN), lambda i, j: (0, j)),
        pl.BlockSpec((D_MODEL, LANES), lambda i, j: (0, 0)),
    ]
    args = [x, norm_w, w_main, w_dt]
    if fuse_conv:
        in_specs += [pl.BlockSpec((CONV_WIDTH, PROJ_TN), xcol), pl.BlockSpec((1, PROJ_TN), xcol),
                     pl.BlockSpec((CONV_PAD, CONV_DIM), lambda i, j: (0, 0))]
        args += list(conv)
    out_shape = [
        jax.ShapeDtypeStruct((t, D_INNER), BF16),
        jax.ShapeDtypeStruct((t, CONV_DIM), BF16 if fuse_conv else F32),
        jax.ShapeDtypeStruct((t, 2 * D_MODEL), BF16),
        jax.ShapeDtypeStruct((t, LANES), F32),
    ]
    out_specs = [
        out_spec(0, _NZ), out_spec(_OFF_X + xbc_lag, _NX), out_spec(_OFF_G, _NG),
        pl.BlockSpec((tm, LANES), lambda i, j: (i, 0)),
    ]
    scratch = [pltpu.VMEM((tm, D_MODEL), BF16)]
    if fuse_conv:
        out_shape.append(jax.ShapeDtypeStruct((t // tm, CONV_PAD, CONV_DIM), F32))
        out_specs.append(pl.BlockSpec((None, CONV_PAD, PROJ_TN), lambda i, j: (i,) + xcol(i, j)))
        scratch += [pltpu.VMEM((_NX, CONV_PAD, PROJ_TN), F32), pltpu.VMEM((2, tm, PROJ_TN), F32)]
    return pl.pallas_call(
        functools.partial(_in_proj_kernel, tm=tm, fuse_conv=fuse_conv),
        grid=(t // tm, _N_COL_TILES),
        in_specs=in_specs,
        out_specs=out_specs,
        out_shape=out_shape,
        scratch_shapes=scratch,
        compiler_params=_cparams("arbitrary", "arbitrary"),
        name="in_proj",
    )(*args)


def _stack_q(q):
    lane = lax.broadcasted_iota(jnp.int32, q.shape, 1)
    zero = jnp.zeros_like(q)
    return jnp.concatenate([jnp.where(lane < ATT_HEAD_DIM, q, zero),
                            jnp.where(lane >= ATT_HEAD_DIM, q, zero)], axis=0)


def _diff_epilogue(o, n, lamp_ref, subln_ref, lam_init):
    lp = lamp_ref[...]
    lam = (jnp.exp(jnp.sum(lp[0:1] * lp[1:2], axis=-1, keepdims=True))
           - jnp.exp(jnp.sum(lp[2:3] * lp[3:4], axis=-1, keepdims=True)) + lam_init)
    a = o[:n] - lam * o[n:]
    return _rms(a, subln_ref[...], ATT_SUBLN_EPS) * (1.0 - lam_init)


def _attn_prompt_kernel(lamp_ref, subln_ref, q_ref, k_ref, v_ref, o_ref, m_sc, l_sc, acc_sc, s_sc, *,
                        tq, tk, nq, unroll, lam_init):
    qi = pl.program_id(1)
    qs = _stack_q(q_ref[...])
    m_sc[...] = jnp.full(m_sc.shape, -jnp.inf, F32)
    l_sc[...] = jnp.zeros(l_sc.shape, F32)
    acc_sc[...] = jnp.zeros(acc_sc.shape, F32)
    n_lane_tiles = tk // LANES
    full_blocks = qi * (tq // tk)

    def scores(kblk, masked):
        start = pl.multiple_of(kblk * tk, tk)
        s = _dot_nt(qs, k_ref[pl.ds(start, tk), :])
        if masked:
            r = lax.broadcasted_iota(jnp.int32, s.shape, 0)
            c = lax.broadcasted_iota(jnp.int32, s.shape, 1)
            q_chunk = jnp.where(r >= tq, r - tq, r) // CHUNK
            k_chunk = c // CHUNK + (kblk - full_blocks) * (tk // CHUNK)
            s = jnp.where(k_chunk <= q_chunk, s, -jnp.inf)
        return s

    def absorb(kblk, s, state):
        m_prev, l_prev, acc_prev = state
        vb = v_ref[pl.ds(pl.multiple_of(kblk * tk, tk), tk), :]
        m_new = jnp.maximum(m_prev, jnp.max(s, axis=-1, keepdims=True))
        alpha = jnp.exp2(m_prev - m_new)
        p = jnp.exp2(s - jnp.concatenate([m_new] * n_lane_tiles, axis=1))
        pv = _dot(p.astype(BF16), jnp.concatenate([vb, jnp.ones((tk, LANES), BF16)], axis=1))
        return m_new, alpha * l_prev + pv[:, ATT_V_DIM:], alpha * acc_prev + pv[:, :ATT_V_DIM]

    PLAIN, MASKED = "plain", "masked"

    def pipeline(first, produce):
        state = (m_sc[...], l_sc[...], acc_sc[...])
        s = s_sc[...]
        for i, kind in enumerate(produce):
            s_next = None if kind is None else scores(first + i + 1, kind == MASKED)
            state = absorb(first + i, s, state)
            s = s_next
        if s is not None:
            s_sc[...] = s
        m_sc[...], l_sc[...], acc_sc[...] = state

    diag_blocks = tq // tk

    @pl.when(qi == 0)
    def _():
        s_sc[...] = scores(0, True)
        pipeline(0, [MASKED] * (diag_blocks - 1) + [None])

    @pl.when(qi > 0)
    def _():
        s_sc[...] = scores(0, False)

    def body(t, carry):
        pipeline(unroll * t, [PLAIN] * unroll)
        return carry

    plain_steps = jnp.maximum(full_blocks - 1, 0)
    lax.fori_loop(0, plain_steps // unroll, body, 0)
    for r in sorted({(diag_blocks * q - 1) % unroll for q in range(1, nq)}):
        @pl.when((qi > 0) & (plain_steps % unroll == r))
        def _():
            pipeline(full_blocks - 1 - r, [PLAIN] * r + [MASKED] * diag_blocks + [None])

    o = acc_sc[...] * (1.0 / l_sc[...])
    o_ref[...] = _diff_epilogue(o, tq, lamp_ref, subln_ref, lam_init).astype(o_ref.dtype)


def _attn_prompt(q, kb, vb, lamp, subln, lam_init, tq, tk, unroll):
    t = q.shape[0]
    assert t % tq == 0 and tq % tk == 0 and tk % LANES == 0 and tk % CHUNK == 0
    kern = functools.partial(_attn_prompt_kernel, tq=tq, tk=tk, nq=t // tq, unroll=unroll, lam_init=lam_init)
    return pl.pallas_call(
        kern,
        grid=(ATT_HEADS, t // tq),
        in_specs=[
            pl.BlockSpec((4, ATT_HEAD_DIM), lambda h, i: (0, 0)),
            pl.BlockSpec((1, ATT_V_DIM), lambda h, i: (0, 0)),
            pl.BlockSpec((tq, LANES), lambda h, i: (i, h)),
            pl.BlockSpec((t, LANES), lambda h, i: (0, h)),
            pl.BlockSpec((t, LANES), lambda h, i: (0, h)),
        ],
        out_specs=pl.BlockSpec((tq, LANES), lambda h, i: (i, h)),
        out_shape=jax.ShapeDtypeStruct((t, ATT_WIDTH), BF16),
        scratch_shapes=[pltpu.VMEM((2 * tq, LANES), F32), pltpu.VMEM((2 * tq, LANES), F32),
                        pltpu.VMEM((2 * tq, ATT_V_DIM), F32), pltpu.VMEM((2 * tq, tk), F32)],
        compiler_params=_cparams("arbitrary", "arbitrary"),
        name="attn_prompt",
    )(lamp, subln, q, kb, vb)


def _attn_sample_kernel(lamp_ref, subln_ref, q_ref, kt_ref, vc_ref, kn_ref, vn_ref, o_ref, m_sc, l_sc, acc_sc, *,
                        n, ts, lam_init):
    piece = pl.program_id(1)

    @pl.when(piece == 0)
    def _():
        m_sc[...] = jnp.full(m_sc.shape, -jnp.inf, F32)
        l_sc[...] = jnp.zeros(l_sc.shape, F32)
        acc_sc[...] = jnp.zeros(acc_sc.shape, F32)

    def update(h, s, v):
        width = s.shape[1]
        m_prev = m_sc[h]
        m_new = jnp.maximum(m_prev, jnp.max(s, axis=-1, keepdims=True))
        alpha = jnp.exp2(m_prev - m_new)
        m_wide = m_new[:, :width] if width <= LANES else jnp.concatenate([m_new] * (width // LANES), axis=1)
        p = jnp.exp2(s - m_wide)
        l_sc[h] = alpha * l_sc[h] + jnp.sum(p, axis=-1, keepdims=True)
        acc_sc[h] = alpha * acc_sc[h] + _dot(p.astype(BF16), v)
        m_sc[h] = m_new

    for h in range(ATT_HEADS):
        hs = slice(h * LANES, (h + 1) * LANES)
        qs = _stack_q(q_ref[0, :, hs])
        s = _dot(qs, kt_ref[0, 0, hs, :].astype(BF16))
        update(h, s, vc_ref[0, 0, pl.ds(h, ts, stride=ATT_HEADS), :].astype(BF16))

    @pl.when(piece == pl.num_programs(1) - 1)
    def _():
        for h in range(ATT_HEADS):
            hs = slice(h * LANES, (h + 1) * LANES)
            qs = _stack_q(q_ref[0, :, hs])
            update(h, _dot_nt(qs, kn_ref[0, :, hs]), vn_ref[0, :, hs])
            o = acc_sc[h] * (1.0 / l_sc[h])
            o_ref[0, :, hs] = _diff_epilogue(o, n, lamp_ref, subln_ref, lam_init).astype(o_ref.dtype)


def _attn_sample(q, cache_kt, cache_v, kb, vb, lamp, subln, lam_init, layer):
    b, n, _ = q.shape
    past = cache_kt.shape[3]
    ts = min(past, SAMPLE_TS)
    assert past % CHUNK == 0 and n <= CHUNK and past % ts == 0
    kern = functools.partial(_attn_sample_kernel, n=n, ts=ts, lam_init=lam_init)
    row_spec = pl.BlockSpec((1, n, Q_DIM), lambda bi, s: (bi, 0, 0))
    return pl.pallas_call(
        kern,
        grid=(b, past // ts),
        in_specs=[
            pl.BlockSpec((4, ATT_HEAD_DIM), lambda bi, s: (0, 0)),
            pl.BlockSpec((1, ATT_V_DIM), lambda bi, s: (0, 0)),
            row_spec,
            pl.BlockSpec((1, 1, Q_DIM, ts), lambda bi, s: (layer, bi, 0, s)),
            pl.BlockSpec((1, 1, ts * ATT_HEADS, ATT_V_DIM), lambda bi, s: (layer, bi, s, 0)),
            row_spec,
            row_spec,
        ],
        out_specs=row_spec,
        out_shape=jax.ShapeDtypeStruct((b, n, ATT_WIDTH), BF16),
        scratch_shapes=[pltpu.VMEM((ATT_HEADS, 2 * n, LANES), F32)] * 3,
        compiler_params=_cparams("arbitrary", "arbitrary"),
        name="attn_sample",
    )(lamp, subln, q, cache_kt, cache_v, kb, vb)


def _ssd_kernel(*refs, lc, valid_last, conv_done):
    if conv_done:
        (xbc_ref, z_ref, dt_ref, st0_ref, dtb_ref, alog_ref, dsk_ref, nrm_ref, expand_ref,
         y_ref, sout_ref, st_sc) = refs
    else:
        (xbc_ref, z_ref, dt_ref, st0_ref, dtb_ref, alog_ref, dsk_ref, nrm_ref, expand_ref,
         cbuf_ref, cw_ref, cb_ref, y_ref, sout_ref, cout_ref, st_sc, xpad_sc) = refs
    c = pl.program_id(1)
    nchunks = pl.num_programs(1)

    @pl.when(c == 0)
    def _():
        for g in range(SSM_GROUPS):
            st_sc[g] = st0_ref[0, g].T

    if conv_done:
        act = xbc_ref[0]
    else:
        @pl.when(c == 0)
        def _():
            xpad_sc[0:CONV_PAD, :] = jnp.zeros((CONV_PAD, CONV_DIM), F32)
            xpad_sc[CONV_PAD - (CONV_WIDTH - 1):CONV_PAD, :] = cbuf_ref[0]

        xpad_sc[CONV_PAD:CONV_PAD + lc, :] = xbc_ref[0]
        xp = xpad_sc[...]
        conv = cw_ref[0:1, :] * xp
        for j in range(1, CONV_WIDTH):
            conv = cw_ref[j:j + 1, :] * xp + pltpu.roll(conv, 1, 0)
        conv = conv[CONV_PAD:CONV_PAD + lc] + cb_ref[...]
        act = conv * jax.nn.sigmoid(conv)

        @pl.when(c == nchunks - 1)
        def _():
            end = CONV_PAD + valid_last
            cout_ref[0] = xpad_sc[end - (CONV_WIDTH - 1):end, :]

        xpad_sc[0:CONV_PAD, :] = xpad_sc[lc:lc + CONV_PAD, :]

    dtr = dt_ref[0] + dtb_ref[...]
    dt = jnp.maximum(dtr, 0.0) + jnp.log1p(jnp.exp(-jnp.abs(dtr)))
    if valid_last < lc:
        row = lax.broadcasted_iota(jnp.int32, dt.shape, 0)
        dt = jnp.where(row < valid_last, dt, 0.0)
    a = dt * (-jnp.exp(alog_ref[...]))
    rr = lax.broadcasted_iota(jnp.int32, (lc, lc), 0)
    cc = lax.broadcasted_iota(jnp.int32, (lc, lc), 1)
    causal = rr >= cc
    tri = jnp.where(causal, 1.0, 0.0).astype(BF16)
    a_cs = sum(_dot(tri, part) for part in _split3(a))
    a_last = a_cs[lc - 1:lc, :]
    a_cs_t = a_cs.T

    expand = expand_ref[...]
    stack = jnp.concatenate([dt, dt * jnp.exp(a_last - a_cs), jnp.exp(a_cs)], axis=0)
    ex = _dot(stack.astype(BF16), expand)
    w_dt = ex[0:lc]
    w_state = ex[lc:2 * lc]
    w_off = ex[2 * lc:3 * lc]
    carry = jnp.broadcast_to(jnp.exp(a_last), (2 * SUBLANES, LANES))
    w_carry = sum(_dot(part, expand) for part in _split3(carry))[0:1]

    xs = act[:, :D_INNER].astype(F32)
    xdt = xs * w_dt
    xst = xs * w_state
    lane = lax.broadcasted_iota(jnp.int32, (lc, LANES), 1)
    lo_half = lane < SSM_HEAD_DIM

    for g in range(SSM_GROUPS):
        gs = slice(g * GROUP_WIDTH, (g + 1) * GROUP_WIDTH)
        bg = act[:, D_INNER + g * SSM_STATE:D_INNER + (g + 1) * SSM_STATE]
        cg = act[:, D_INNER + SSM_GROUPS * SSM_STATE + g * SSM_STATE:
                 D_INNER + SSM_GROUPS * SSM_STATE + (g + 1) * SSM_STATE]
        bgb = bg.astype(BF16)
        cgb = cg.astype(BF16)
        cb = _dot_nt(cgb, bgb)
        st_prev = st_sc[g]
        y_off = _dot(cgb, st_prev.astype(BF16)) * w_off[:, gs]
        st_sc[g] = w_carry[:, gs] * st_prev + _dot(bg.astype(F32).T.astype(BF16), xst[:, gs].astype(BF16))

        pairs = []
        for pr in range(SSM_HEADS_PER_GROUP // 2):
            x_pair = xdt[:, g * GROUP_WIDTH + pr * LANES:g * GROUP_WIDTH + (pr + 1) * LANES]
            y_pair = None
            for half in range(2):
                h = g * SSM_HEADS_PER_GROUP + pr * 2 + half
                seg = a_cs[:, h:h + 1] - a_cs_t[h:h + 1, :]
                decay = jnp.exp(jnp.where(causal, seg, -jnp.inf))
                mix = (cb * decay).astype(BF16)
                keep = lo_half if half == 0 else jnp.logical_not(lo_half)
                contrib = _dot(mix, jnp.where(keep, x_pair, 0.0).astype(BF16))
                y_pair = contrib if y_pair is None else y_pair + contrib
            pairs.append(y_pair)
        y_g = jnp.concatenate(pairs, axis=1) + y_off + dsk_ref[:, gs] * xs[:, gs]

        zg = z_ref[0, :, gs].astype(F32)
        yz = y_g * (zg * jax.nn.sigmoid(zg))
        y_ref[0, :, gs] = _rms(yz, nrm_ref[:, gs], SSM_NORM_EPS).astype(y_ref.dtype)

    @pl.when(c == nchunks - 1)
    def _():
        for g in range(SSM_GROUPS):
            sout_ref[0, g] = st_sc[g].T


def _ssd(xbc, z, dt, state0, p, lc, valid_last, conv_buf=None):
    b, l, _ = xbc.shape
    conv_done = conv_buf is None
    assert l % lc == 0 and CONV_WIDTH - 1 <= valid_last <= lc
    assert valid_last == lc or l == lc
    state_spec = pl.BlockSpec((1, SSM_GROUPS, GROUP_WIDTH, SSM_STATE), lambda bi, c: (bi, 0, 0, 0))
    in_specs = [
        pl.BlockSpec((1, lc, CONV_DIM), lambda bi, c: (bi, c, 0)),
        pl.BlockSpec((1, lc, D_INNER), lambda bi, c: (bi, c, 0)),
        pl.BlockSpec((1, lc, LANES), lambda bi, c: (bi, c, 0)),
        state_spec,
        pl.BlockSpec((1, LANES), lambda bi, c: (0, 0)),
        pl.BlockSpec((1, LANES), lambda bi, c: (0, 0)),
        pl.BlockSpec((1, D_INNER), lambda bi, c: (0, 0)),
        pl.BlockSpec((1, D_INNER), lambda bi, c: (0, 0)),
        pl.BlockSpec((LANES, D_INNER), lambda bi, c: (0, 0)),
    ]
    args = [xbc, z, dt, state0, p["dt_bias"], p["a_log"], p["d_skip"], p["ssm_norm"], p["expand"]]
    out_specs = [pl.BlockSpec((1, lc, D_INNER), lambda bi, c: (bi, c, 0)), state_spec]
    out_shape = [jax.ShapeDtypeStruct((b, l, D_INNER), BF16),
                 jax.ShapeDtypeStruct((b, SSM_GROUPS, GROUP_WIDTH, SSM_STATE), F32)]
    scratch = [pltpu.VMEM((SSM_GROUPS, SSM_STATE, GROUP_WIDTH), F32)]
    if not conv_done:
        conv_spec = pl.BlockSpec((1, CONV_WIDTH - 1, CONV_DIM), lambda bi, c: (bi, 0, 0))
        in_specs += [conv_spec, pl.BlockSpec((CONV_WIDTH, CONV_DIM), lambda bi, c: (0, 0)),
                     pl.BlockSpec((1, CONV_DIM), lambda bi, c: (0, 0))]
        args += [conv_buf, p["conv_w"], p["conv_b"]]
        out_specs.append(conv_spec)
        out_shape.append(jax.ShapeDtypeStruct((b, CONV_WIDTH - 1, CONV_DIM), F32))
        scratch.append(pltpu.VMEM((lc + CONV_PAD, CONV_DIM), F32))
    return pl.pallas_call(
        functools.partial(_ssd_kernel, lc=lc, valid_last=valid_last, conv_done=conv_done),
        grid=(b, l // lc),
        in_specs=in_specs,
        out_specs=out_specs,
        out_shape=out_shape,
        scratch_shapes=scratch,
        compiler_params=_cparams("arbitrary", "arbitrary"),
        name="ssd",
    )(*args)


def _merge_kernel(x_ref, att_ref, y_ref, g_ref, wa_ref, ws_ref, wo_ref, o_ref):
    ba = _dot(att_ref[...], wa_ref[...])
    bs = _dot(y_ref[...], ws_ref[...])
    g = jax.nn.sigmoid(g_ref[...].astype(F32))
    merged = g[:, :D_MODEL] * ba + g[:, D_MODEL:] * bs
    o_ref[...] = x_ref[...] + _dot(merged.astype(BF16), wo_ref[...])


def _merge(x, att, y, gates, wa, ws, wo, tm):
    t = x.shape[0]
    return pl.pallas_call(
        _merge_kernel,
        grid=(t // tm,),
        in_specs=[
            pl.BlockSpec((tm, D_MODEL), lambda i: (i, 0)),
            pl.BlockSpec((tm, ATT_WIDTH), lambda i: (i, 0)),
            pl.BlockSpec((tm, D_INNER), lambda i: (i, 0)),
            pl.BlockSpec((tm, 2 * D_MODEL), lambda i: (i, 0)),
            _const_spec((ATT_WIDTH, D_MODEL)),
            _const_spec((D_INNER, D_MODEL)),
            _const_spec((D_MODEL, D_MODEL)),
        ],
        out_specs=pl.BlockSpec((tm, D_MODEL), lambda i: (i, 0)),
        out_shape=jax.ShapeDtypeStruct((t, D_MODEL), F32),
        compiler_params=_cparams("arbitrary"),
        name="merge",
    )(x, att, y, gates, wa, ws, wo)


FFN_SPLIT = 2
FFN_TF = D_FF // FFN_SPLIT


def _ffn_kernel(x_ref, nw_ref, wgu_ref, wd_ref, nf_ref, *out_refs, final_norm):
    x = x_ref[...]
    hb = _rms(x, nw_ref[...], RMS_EPS).astype(BF16)
    acc = x
    for c in range(FFN_SPLIT):
        gt = _dot(hb, wgu_ref[:, c * FFN_TF:(c + 1) * FFN_TF])
        up = _dot(hb, wgu_ref[:, D_FF + c * FFN_TF:D_FF + (c + 1) * FFN_TF])
        act = (gt * jax.nn.sigmoid(gt) * up).astype(BF16)
        acc = acc + _dot(act, wd_ref[c * FFN_TF:(c + 1) * FFN_TF, :])
    out_refs[0][...] = acc
    if final_norm:
        out_refs[1][...] = _rms(acc, nf_ref[...], RMS_EPS)


def _ffn(x, norm_w, wgu, wd, norm_final, final_norm, tm):
    t = x.shape[0]
    row_spec = pl.BlockSpec((tm, D_MODEL), lambda i: (i, 0))
    n_out = 2 if final_norm else 1
    return pl.pallas_call(
        functools.partial(_ffn_kernel, final_norm=final_norm),
        grid=(t // tm,),
        in_specs=[
            row_spec,
            pl.BlockSpec((1, D_MODEL), lambda i: (0, 0)),
            _const_spec((D_MODEL, 2 * D_FF)),
            _const_spec((D_FF, D_MODEL)),
            pl.BlockSpec((1, D_MODEL), lambda i: (0, 0)),
        ],
        out_specs=(row_spec,) * n_out,
        out_shape=(jax.ShapeDtypeStruct((t, D_MODEL), F32),) * n_out,
        compiler_params=_cparams("arbitrary"),
        name="ffn",
    )(x, norm_w, wgu, wd, norm_final)


def _layer_params(l, norm_mix, w_in, lambda_q1, lambda_k1, lambda_q2, lambda_k2, attn_subln, conv_w, conv_b,
                  dt_bias, a_log, d_skip, ssm_norm, w_branch_att, w_branch_ssd, w_out, norm_ffn, w_gate_up,
                  w_down):
    w = w_in[l]
    n_qkv = Q_DIM + Q_DIM + ATT_WIDTH
    n_main = n_qkv + D_INNER + CONV_DIM
    w_main = jnp.concatenate([w[:, n_qkv:n_main], w[:, n_main + SSM_HEADS:]], axis=1)
    w_dt = jnp.pad(w[:, n_main:n_main + SSM_HEADS], ((0, 0), (0, LANES - SSM_HEADS)))
    head_of_channel = jnp.arange(D_INNER) // SSM_HEAD_DIM
    return {
        "norm_mix": norm_mix[l][None], "w_qkv": w[:, :n_qkv], "w_main": w_main, "w_dt": w_dt,
        "lamp": jnp.stack([lambda_q1[l], lambda_k1[l], lambda_q2[l], lambda_k2[l]]),
        "subln": attn_subln[l][None],
        "conv_w": conv_w[l], "conv_b": conv_b[l][None],
        "dt_bias": jnp.pad(dt_bias[l], (0, LANES - SSM_HEADS))[None],
        "a_log": jnp.pad(a_log[l], (0, LANES - SSM_HEADS))[None],
        "d_skip": jnp.repeat(d_skip[l], SSM_HEAD_DIM)[None],
        "ssm_norm": ssm_norm[l][None],
        "expand": (jnp.arange(LANES)[:, None] == head_of_channel[None, :]).astype(BF16),
        "wa": w_branch_att[l].astype(BF16), "ws": w_branch_ssd[l].astype(BF16), "wo": w_out[l].astype(BF16),
        "norm_ffn": norm_ffn[l][None], "wgu": w_gate_up[l].astype(BF16), "wd": w_down[l].astype(BF16),
    }


def _pad_time(x, b, n, lc):
    return jnp.pad(x.reshape(b, n, x.shape[-1]), ((0, 0), (0, lc - n), (0, 0)))


def kernel(x_prompt, x_sample, cache_k, cache_v, state_conv, state_ssm, norm_mix, w_in, lambda_q1, lambda_k1,
           lambda_q2, lambda_k2, attn_subln, conv_w, conv_b, dt_bias, a_log, d_skip, ssm_norm, w_branch_att,
           w_branch_ssd, w_out, norm_ffn, w_gate_up, w_down, norm_final):
    bp, lp, _ = x_prompt.shape
    bs, ls, _ = x_sample.shape
    assert bp == 1
    past = cache_k.shape[2]
    tp, ts = bp * lp, bs * ls
    xp = x_prompt.reshape(tp, D_MODEL)
    xs = x_sample.reshape(ts, D_MODEL)
    cache_k = jnp.transpose(cache_k, (0, 1, 3, 4, 5, 2)).reshape(DEPTH, bs, Q_DIM, past)
    cache_v = cache_v.reshape(DEPTH, bs, past * ATT_HEADS, ATT_V_DIM)
    state_ssm = state_ssm.reshape(DEPTH, bs, SSM_GROUPS, GROUP_WIDTH, SSM_STATE)
    conv0 = jnp.zeros((CONV_PAD, CONV_DIM), F32)
    ssm0 = jnp.zeros((bp, SSM_GROUPS, GROUP_WIDTH, SSM_STATE), F32)
    nf = norm_final[None]

    tm_p = 1024 if tp % 1024 == 0 else 128
    tm_row = 512 if tp % 512 == 0 else 128
    tq = ATTN_TQ if lp % ATTN_TQ == 0 else CHUNK
    tk = ATTN_TK if tq % ATTN_TK == 0 else tq

    w_in = w_in.astype(BF16)
    conv_p, ssm_p, conv_s, ssm_s = [], [], [], []
    kv_p = kv_s = None
    yp = ys = None
    for l in range(DEPTH):
        p = _layer_params(l, norm_mix, w_in, lambda_q1, lambda_k1, lambda_q2, lambda_k2, attn_subln, conv_w,
                          conv_b, dt_bias, a_log, d_skip, ssm_norm, w_branch_att, w_branch_ssd, w_out,
                          norm_ffn, w_gate_up, w_down)
        lam_init = 0.8 - 0.6 * math.exp(-0.3 * l)
        last = l == DEPTH - 1

        q, k_all, kb, v_all, vb = _qkv_proj(xp, p["norm_mix"], p["w_qkv"], tm_p, stacks=kv_p, layer=l)
        kv_p = (k_all, v_all)
        z, act, gates, dt, tail = _in_proj(xp, p["norm_mix"], p["w_main"], p["w_dt"], tm_p,
                                           conv=(p["conv_w"], p["conv_b"], conv0))
        att = _attn_prompt(q, kb, vb, p["lamp"], p["subln"], lam_init, tq, tk, ATTN_UNROLL)
        y, s_new = _ssd(act.reshape(bp, lp, CONV_DIM), z.reshape(bp, lp, D_INNER), dt.reshape(bp, lp, LANES),
                        ssm0, p, SSD_CHUNK, SSD_CHUNK)
        x1 = _merge(xp, att, y.reshape(tp, D_INNER), gates, p["wa"], p["ws"], p["wo"], tm_row)
        res = _ffn(x1, p["norm_ffn"], p["wgu"], p["wd"], nf, last, tm_row)
        xp = res[0]
        if last:
            yp = res[1]
        conv_p.append(tail[-1, CONV_PAD - (CONV_WIDTH - 1):][None])
        ssm_p.append(s_new.reshape(bp, SSM_GROUPS, SSM_HEADS_PER_GROUP, SSM_HEAD_DIM, SSM_STATE))

        q, k_all, kb, v_all, vb = _qkv_proj(xs, p["norm_mix"], p["w_qkv"], ts, stacks=kv_s, layer=l)
        kv_s = (k_all, v_all)
        z, xbc, gates, dt = _in_proj(xs, p["norm_mix"], p["w_main"], p["w_dt"], ts)
        att = _attn_sample(q.reshape(bs, ls, Q_DIM), cache_k, cache_v, kb.reshape(bs, ls, Q_DIM),
                           vb.reshape(bs, ls, ATT_WIDTH), p["lamp"], p["subln"], lam_init, l)
        y, s_new, c_new = _ssd(_pad_time(xbc, bs, ls, SSD_PAD_ROWS), _pad_time(z, bs, ls, SSD_PAD_ROWS),
                               _pad_time(dt, bs, ls, SSD_PAD_ROWS), state_ssm[l], p, SSD_PAD_ROWS, ls,
                               conv_buf=state_conv[l])
        x1 = _merge(xs, att.reshape(ts, ATT_WIDTH), y[:, :ls].reshape(ts, D_INNER), gates,
                    p["wa"], p["ws"], p["wo"], ts)
        res = _ffn(x1, p["norm_ffn"], p["wgu"], p["wd"], nf, last, ts)
        xs = res[0]
        if last:
            ys = res[1]
        conv_s.append(c_new)
        ssm_s.append(s_new.reshape(bs, SSM_GROUPS, SSM_HEADS_PER_GROUP, SSM_HEAD_DIM, SSM_STATE))

    return (yp.reshape(bp, lp, D_MODEL), ys.reshape(bs, ls, D_MODEL),
            kv_p[0].reshape(DEPTH, bp, lp, ATT_HEADS, 2, ATT_HEAD_DIM),
            kv_p[1].reshape(DEPTH, bp, lp, ATT_HEADS, ATT_V_DIM),
            jnp.stack(conv_p), jnp.stack(ssm_p),
            kv_s[0].reshape(DEPTH, bs, ls, ATT_HEADS, 2, ATT_HEAD_DIM),
            kv_s[1].reshape(DEPTH, bs, ls, ATT_HEADS, ATT_V_DIM),
            jnp.stack(conv_s), jnp.stack(ssm_s))
```

```python
import functools
import math

import jax
import jax.numpy as jnp
from jax import lax
from jax.experimental import pallas as pl
from jax.experimental.pallas import tpu as pltpu

F32 = jnp.float32
BF16 = jnp.bfloat16
LOG2E = math.log2(math.e)

D_MODEL = 1024
DEPTH = 2
CHUNK = 64
ATT_HEADS = 8
ATT_HEAD_DIM = 64
ATT_V_DIM = 128
ATT_WIDTH = 1024
Q_DIM = 1024
ATT_SUBLN_EPS = 1e-5
D_INNER = 2048
SSM_HEAD_DIM = 64
SSM_HEADS = 32
SSM_GROUPS = 8
SSM_HEADS_PER_GROUP = 4
SSM_STATE = 128
CONV_WIDTH = 4
CONV_DIM = 4096
SSM_NORM_EPS = 1e-5
D_FF = 2816
RMS_EPS = 1e-6
GROUP_WIDTH = SSM_HEADS_PER_GROUP * SSM_HEAD_DIM

LANES = 128
SUBLANES = 8
VMEM_LIMIT_BYTES = 56 * 1024 * 1024

PROJ_TN = 512
SSD_CHUNK = 256
SSD_PAD_ROWS = 128
CONV_PAD = 8
CONV_ROWS = 32
CONV_COLS = 512
SAMPLE_TS = 1024
ATTN_TQ = 1024
ATTN_UNROLL = 2


def _cparams(*sem):
    return pltpu.CompilerParams(dimension_semantics=sem, vmem_limit_bytes=VMEM_LIMIT_BYTES)


def _const_spec(shape):
    return pl.BlockSpec(shape, lambda *_: (0,) * len(shape), pipeline_mode=pl.Buffered(1))


def _rms(x, w, eps):
    return x * lax.rsqrt(jnp.mean(x * x, axis=-1, keepdims=True) + eps) * w


def _split3(x):
    hi = x.astype(BF16)
    r1 = x - hi.astype(F32)
    mid = r1.astype(BF16)
    lo = (r1 - mid.astype(F32)).astype(BF16)
    return hi, mid, lo


def _dot(a, b):
    return jnp.dot(a, b, preferred_element_type=F32)


def _dot_nt(a, b):
    return lax.dot_general(a, b, (((1,), (1,)), ((), ())), preferred_element_type=F32)


_NZ = D_INNER // PROJ_TN
_NX = CONV_DIM // PROJ_TN
_NG = 2 * D_MODEL // PROJ_TN
_OFF_X = _NZ
_OFF_G = _OFF_X + _NX
_N_COL_TILES = _OFF_G + _NG


def _qkv_kernel(*refs):
    x_ref, nw_ref, w_ref = refs[:3]
    q_ref, k_ref, kb_ref, v_ref, vb_ref, h_sc = refs[-6:]
    j = pl.program_id(1)

    @pl.when(j == 0)
    def _():
        h_sc[...] = _rms(x_ref[...], nw_ref[...], RMS_EPS).astype(BF16)
        q_ref[...] = (_dot(h_sc[...], w_ref[...]) * (LOG2E * ATT_HEAD_DIM ** -0.5)).astype(BF16)

    @pl.when(j == 1)
    def _():
        res = _dot(h_sc[...], w_ref[...])
        k_ref[...] = res
        kb_ref[...] = res.astype(BF16)

    @pl.when(j == 2)
    def _():
        res = _dot(h_sc[...], w_ref[...])
        v_ref[...] = res
        vb_ref[...] = res.astype(BF16)


def _qkv_proj(x, norm_w, w_qkv, tm, stacks=None, layer=0):
    t = x.shape[0]
    row = pl.BlockSpec((tm, Q_DIM), lambda i, j: (i, 0))
    stack = pl.BlockSpec((None, tm, Q_DIM), lambda i, j: (layer, i, 0))
    in_specs = [pl.BlockSpec((tm, D_MODEL), lambda i, j: (i, 0)), pl.BlockSpec((1, D_MODEL), lambda i, j: (0, 0)),
                pl.BlockSpec((D_MODEL, Q_DIM), lambda i, j: (0, j))]
    args = [x, norm_w, w_qkv]
    aliases = {}
    if stacks is not None:
        in_specs += [pl.BlockSpec(memory_space=pl.ANY)] * 2
        aliases = {len(args): 1, len(args) + 1: 3}
        args += list(stacks)
    return pl.pallas_call(
        _qkv_kernel,
        grid=(t // tm, 3),
        in_specs=in_specs,
        out_specs=[row, stack, row, stack, row],
        out_shape=[jax.ShapeDtypeStruct((t, Q_DIM), BF16),
                   jax.ShapeDtypeStruct((DEPTH, t, Q_DIM), F32),
                   jax.ShapeDtypeStruct((t, Q_DIM), BF16),
                   jax.ShapeDtypeStruct((DEPTH, t, ATT_WIDTH), F32),
                   jax.ShapeDtypeStruct((t, ATT_WIDTH), BF16)],
        scratch_shapes=[pltpu.VMEM((tm, D_MODEL), BF16)],
        input_output_aliases=aliases,
        compiler_params=_cparams("arbitrary", "arbitrary"),
        name="qkv_proj",
    )(*args)


def _in_proj_kernel(*refs, tm, fuse_conv):
    x_ref, nw_ref, w_ref, wdt_ref = refs[:4]
    pos = 4
    if fuse_conv:
        cw_ref, cb_ref, hist0_ref = refs[pos:pos + 3]
        pos += 3
    z_ref, xbc_ref, g_ref, dt_ref = refs[pos:pos + 4]
    pos += 4
    if fuse_conv:
        tail_ref = refs[pos]
        pos += 1
    h_sc = refs[pos]
    i = pl.program_id(0)
    j = pl.program_id(1)

    @pl.when(j == 0)
    def _():
        hb = _rms(x_ref[...], nw_ref[...], RMS_EPS).astype(BF16)
        h_sc[...] = hb
        dt_ref[...] = _dot(hb, wdt_ref[...])

    def tile():
        return _dot(h_sc[...], w_ref[...])

    @pl.when(j < _OFF_X)
    def _():
        z_ref[...] = tile().astype(z_ref.dtype)

    if not fuse_conv:
        @pl.when((j >= _OFF_X) & (j < _OFF_G))
        def _():
            xbc_ref[...] = tile()

        @pl.when(j >= _OFF_G)
        def _():
            g_ref[...] = tile().astype(g_ref.dtype)
        return

    hist_sc, raw_sc = refs[pos + 1], refs[pos + 2]

    @pl.when((i == 0) & (j == 0))
    def _():
        for c in range(_NX):
            hist_sc[c] = hist0_ref[:, c * PROJ_TN:(c + 1) * PROJ_TN]

    def activate(jc, slot):
        for c0 in range(0, PROJ_TN, CONV_COLS):
            cs = slice(c0, c0 + CONV_COLS)
            for r0 in range(0, tm, CONV_ROWS):
                before = hist_sc[jc, :, cs] if r0 == 0 else raw_sc[slot, r0 - CONV_PAD:r0, cs]
                xp = jnp.concatenate([before, raw_sc[slot, r0:r0 + CONV_ROWS, cs]], axis=0)
                conv = cw_ref[0:1, cs] * xp
                for tap in range(1, CONV_WIDTH):
                    conv = cw_ref[tap:tap + 1, cs] * xp + pltpu.roll(conv, 1, 0)
                conv = conv[CONV_PAD:] + cb_ref[:, cs]
                xbc_ref[r0:r0 + CONV_ROWS, cs] = (conv * jax.nn.sigmoid(conv)).astype(xbc_ref.dtype)
        last_rows = raw_sc[slot, tm - CONV_PAD:tm, :]
        hist_sc[jc] = last_rows
        tail_ref[...] = last_rows

    @pl.when(j == _OFF_X)
    def _():
        raw_sc[0] = tile()

    for slot in range(2):
        @pl.when((j > _OFF_X) & (j < _OFF_G) & (lax.rem(j - _OFF_X, 2) == slot))
        def _():
            activate(j - _OFF_X - 1, 1 - slot)
            raw_sc[slot] = tile()

    @pl.when(j == _OFF_G)
    def _():
        activate(_NX - 1, (_NX - 1) % 2)
        g_ref[...] = tile().astype(g_ref.dtype)

    @pl.when(j > _OFF_G)
    def _():
        g_ref[...] = tile().astype(g_ref.dtype)


def _in_proj(x, norm_w, w_main, w_dt, tm, conv=None):
    t = x.shape[0]
    fuse_conv = conv is not None

    def out_spec(off, n):
        return pl.BlockSpec((tm, PROJ_TN), lambda i, j: (i, jnp.clip(j - off, 0, n - 1)))

    xbc_lag = 1 if fuse_conv else 0

    def xcol(i, j):
        return (0, jnp.clip(j - _OFF_X - xbc_lag, 0, _NX - 1))

    in_specs = [
        pl.BlockSpec((tm, D_MODEL), lambda i, j: (i, 0)),
        pl.BlockSpec((1, D_MODEL), lambda i, j: (0, 0)),
        pl.BlockSpec((D_MODEL, PROJ_TN), lambda i, j: (0, j)),
        pl.BlockSpec((D_MODEL, LANES), lambda i, j: (0, 0)),
    ]
    args = [x, norm_w, w_main, w_dt]
    if fuse_conv:
        in_specs += [pl.BlockSpec((CONV_WIDTH, PROJ_TN), xcol), pl.BlockSpec((1, PROJ_TN), xcol),
                     pl.BlockSpec((CONV_PAD, CONV_DIM), lambda i, j: (0, 0))]
        args += list(conv)
    out_shape = [
        jax.ShapeDtypeStruct((t, D_INNER), BF16),
        jax.ShapeDtypeStruct((t, CONV_DIM), BF16 if fuse_conv else F32),
        jax.ShapeDtypeStruct((t, 2 * D_MODEL), BF16),
        jax.ShapeDtypeStruct((t, LANES), F32),
    ]
    out_specs = [
        out_spec(0, _NZ), out_spec(_OFF_X + xbc_lag, _NX), out_spec(_OFF_G, _NG),
        pl.BlockSpec((tm, LANES), lambda i, j: (i, 0)),
    ]
    scratch = [pltpu.VMEM((tm, D_MODEL), BF16)]
    if fuse_conv:
        out_shape.append(jax.ShapeDtypeStruct((t // tm, CONV_PAD, CONV_DIM), F32))
        out_specs.append(pl.BlockSpec((None, CONV_PAD, PROJ_TN), lambda i, j: (i,) + xcol(i, j)))
        scratch += [pltpu.VMEM((_NX, CONV_PAD, PROJ_TN), F32), pltpu.VMEM((2, tm, PROJ_TN), F32)]
    return pl.pallas_call(
        functools.partial(_in_proj_kernel, tm=tm, fuse_conv=fuse_conv),
        grid=(t // tm, _N_COL_TILES),
        in_specs=in_specs,
        out_specs=out_specs,
        out_shape=out_shape,
        scratch_shapes=scratch,
        compiler_params=_cparams("arbitrary", "arbitrary"),
        name="in_proj",
    )(*args)


def _stack_q(q):
    lane = lax.broadcasted_iota(jnp.int32, q.shape, 1)
    zero = jnp.zeros_like(q)
    return jnp.concatenate([jnp.where(lane < ATT_HEAD_DIM, q, zero),
                            jnp.where(lane >= ATT_HEAD_DIM, q, zero)], axis=0)


def _diff_epilogue(o, n, lamp_ref, subln_ref, lam_init):
    lp = lamp_ref[...]
    lam = (jnp.exp(jnp.sum(lp[0:1] * lp[1:2], axis=-1, keepdims=True))
           - jnp.exp(jnp.sum(lp[2:3] * lp[3:4], axis=-1, keepdims=True)) + lam_init)
    a = o[:n] - lam * o[n:]
    return _rms(a, subln_ref[...], ATT_SUBLN_EPS) * (1.0 - lam_init)


def _attn_prompt_kernel(lamp_ref, subln_ref, q_ref, k_ref, v_ref, o_ref, m_sc, l_sc, acc_sc, s_sc, *,
                        tq, tk, nq, unroll, lam_init):
    qi = pl.program_id(1)
    half = tq // 2
    qs = jnp.concatenate([_stack_q(q_ref[0:half, :]), _stack_q(q_ref[half:tq, :])], axis=0)
    m_sc[...] = jnp.full(m_sc.shape, -jnp.inf, F32)
    l_sc[...] = jnp.zeros(l_sc.shape, F32)
    acc_sc[...] = jnp.zeros(acc_sc.shape, F32)
    n_lane_tiles = tk // LANES
    full_blocks = qi * (tq // tk)
    PLAIN, MASKED, MASKED_LATE = "plain", "masked", "masked_late"

    def scores(kblk, kind):
        start = pl.multiple_of(kblk * tk, tk)
        rows = qs[tq:] if kind == MASKED_LATE else qs
        s = _dot_nt(rows, k_ref[pl.ds(start, tk), :])
        if kind != PLAIN:
            r = lax.broadcasted_iota(jnp.int32, s.shape, 0)
            c = lax.broadcasted_iota(jnp.int32, s.shape, 1)
            if kind == MASKED_LATE:
                q_row = half + jnp.where(r >= half, r - half, r)
            else:
                q_row = jnp.where(r >= tq, half, 0) + lax.rem(r, half)
            k_row = c + (kblk - full_blocks) * tk
            s = jnp.where(k_row // CHUNK <= q_row // CHUNK, s, -jnp.inf)
        return s

    def absorb(kblk, s, state):
        if s.shape[0] != 2 * tq:
            late = absorb_rows(kblk, s, tuple(a[tq:] for a in state))
            return tuple(jnp.concatenate([a[:tq], b], axis=0) for a, b in zip(state, late))
        return absorb_rows(kblk, s, state)

    def absorb_rows(kblk, s, state):
        m_prev, l_prev, acc_prev = state
        vb = v_ref[pl.ds(pl.multiple_of(kblk * tk, tk), tk), :]
        m_new = jnp.maximum(m_prev, jnp.max(s, axis=-1, keepdims=True))
        alpha = jnp.exp2(m_prev - m_new)
        p = jnp.exp2(s - jnp.concatenate([m_new] * n_lane_tiles, axis=1))
        pv = _dot(p.astype(BF16), jnp.concatenate([vb, jnp.ones((tk, LANES), BF16)], axis=1))
        return m_new, alpha * l_prev + pv[:, ATT_V_DIM:], alpha * acc_prev + pv[:, :ATT_V_DIM]

    def pipeline(first, produce):
        state = (m_sc[...], l_sc[...], acc_sc[...])
        s = s_sc[...]
        for i, kind in enumerate(produce):
            s_next = None if kind is None else scores(first + i + 1, kind)
            state = absorb(first + i, s, state)
            s = s_next
        if s is not None:
            s_sc[...] = s
        m_sc[...], l_sc[...], acc_sc[...] = state

    diag = [MASKED, MASKED_LATE]

    @pl.when(qi == 0)
    def _():
        s_sc[...] = scores(0, MASKED)
        pipeline(0, diag[1:] + [None])

    @pl.when(qi > 0)
    def _():
        s_sc[...] = scores(0, PLAIN)

    def body(t, carry):
        pipeline(unroll * t, [PLAIN] * unroll)
        return carry

    plain_steps = jnp.maximum(full_blocks - 1, 0)
    lax.fori_loop(0, plain_steps // unroll, body, 0)
    for r in sorted({(len(diag) * q - 1) % unroll for q in range(1, nq)}):
        @pl.when((qi > 0) & (plain_steps % unroll == r))
        def _():
            pipeline(full_blocks - 1 - r, [PLAIN] * r + diag + [None])

    o = acc_sc[...] * (1.0 / l_sc[...])
    for g in range(2):
        o_ref[g * half:(g + 1) * half, :] = _diff_epilogue(
            o[g * tq:(g + 1) * tq], half, lamp_ref, subln_ref, lam_init).astype(o_ref.dtype)


def _attn_prompt(q, kb, vb, lamp, subln, lam_init, tq, tk, unroll):
    t = q.shape[0]
    assert t % tq == 0 and tq == 2 * tk and tk % LANES == 0 and tk % CHUNK == 0
    kern = functools.partial(_attn_prompt_kernel, tq=tq, tk=tk, nq=t // tq, unroll=unroll, lam_init=lam_init)
    return pl.pallas_call(
        kern,
        grid=(ATT_HEADS, t // tq),
        in_specs=[
            pl.BlockSpec((4, ATT_HEAD_DIM), lambda h, i: (0, 0)),
            pl.BlockSpec((1, ATT_V_DIM), lambda h, i: (0, 0)),
            pl.BlockSpec((tq, LANES), lambda h, i: (i, h)),
            pl.BlockSpec((t, LANES), lambda h, i: (0, h)),
            pl.BlockSpec((t, LANES), lambda h, i: (0, h)),
        ],
        out_specs=pl.BlockSpec((tq, LANES), lambda h, i: (i, h)),
        out_shape=jax.ShapeDtypeStruct((t, ATT_WIDTH), BF16),
        scratch_shapes=[pltpu.VMEM((2 * tq, LANES), F32), pltpu.VMEM((2 * tq, LANES), F32),
                        pltpu.VMEM((2 * tq, ATT_V_DIM), F32), pltpu.VMEM((2 * tq, tk), F32)],
        compiler_params=_cparams("arbitrary", "arbitrary"),
        name="attn_prompt",
    )(lamp, subln, q, kb, vb)


def _attn_sample_kernel(lamp_ref, subln_ref, q_ref, kt_ref, vc_ref, kn_ref, vn_ref, o_ref, m_sc, l_sc, acc_sc, *,
                        n, ts, lam_init):
    piece = pl.program_id(1)

    @pl.when(piece == 0)
    def _():
        m_sc[...] = jnp.full(m_sc.shape, -jnp.inf, F32)
        l_sc[...] = jnp.zeros(l_sc.shape, F32)
        acc_sc[...] = jnp.zeros(acc_sc.shape, F32)

    def update(h, s, v):
        width = s.shape[1]
        m_prev = m_sc[h]
        m_new = jnp.maximum(m_prev, jnp.max(s, axis=-1, keepdims=True))
        alpha = jnp.exp2(m_prev - m_new)
        m_wide = m_new[:, :width] if width <= LANES else jnp.concatenate([m_new] * (width // LANES), axis=1)
        p = jnp.exp2(s - m_wide)
        l_sc[h] = alpha * l_sc[h] + jnp.sum(p, axis=-1, keepdims=True)
        acc_sc[h] = alpha * acc_sc[h] + _dot(p.astype(BF16), v)
        m_sc[h] = m_new

    for h in range(ATT_HEADS):
        hs = slice(h * LANES, (h + 1) * LANES)
        qs = _stack_q(q_ref[0, :, hs])
        s = _dot(qs, kt_ref[0, 0, hs, :].astype(BF16))
        update(h, s, vc_ref[0, 0, pl.ds(h, ts, stride=ATT_HEADS), :].astype(BF16))

    @pl.when(piece == pl.num_programs(1) - 1)
    def _():
        for h in range(ATT_HEADS):
            hs = slice(h * LANES, (h + 1) * LANES)
            qs = _stack_q(q_ref[0, :, hs])
            update(h, _dot_nt(qs, kn_ref[0, :, hs]), vn_ref[0, :, hs])
            o = acc_sc[h] * (1.0 / l_sc[h])
            o_ref[0, :, hs] = _diff_epilogue(o, n, lamp_ref, subln_ref, lam_init).astype(o_ref.dtype)


def _attn_sample(q, cache_kt, cache_v, kb, vb, lamp, subln, lam_init, layer):
    b, n, _ = q.shape
    past = cache_kt.shape[3]
    ts = min(past, SAMPLE_TS)
    assert past % CHUNK == 0 and n <= CHUNK and past % ts == 0
    kern = functools.partial(_attn_sample_kernel, n=n, ts=ts, lam_init=lam_init)
    row_spec = pl.BlockSpec((1, n, Q_DIM), lambda bi, s: (bi, 0, 0))
    return pl.pallas_call(
        kern,
        grid=(b, past // ts),
        in_specs=[
            pl.BlockSpec((4, ATT_HEAD_DIM), lambda bi, s: (0, 0)),
            pl.BlockSpec((1, ATT_V_DIM), lambda bi, s: (0, 0)),
            row_spec,
            pl.BlockSpec((1, 1, Q_DIM, ts), lambda bi, s: (layer, bi, 0, s)),
            pl.BlockSpec((1, 1, ts * ATT_HEADS, ATT_V_DIM), lambda bi, s: (layer, bi, s, 0)),
            row_spec,
            row_spec,
        ],
        out_specs=row_spec,
        out_shape=jax.ShapeDtypeStruct((b, n, ATT_WIDTH), BF16),
        scratch_shapes=[pltpu.VMEM((ATT_HEADS, 2 * n, LANES), F32)] * 3,
        compiler_params=_cparams("arbitrary", "arbitrary"),
        name="attn_sample",
    )(lamp, subln, q, cache_kt, cache_v, kb, vb)


def _ssd_kernel(*refs, lc, valid_last, conv_done):
    if conv_done:
        (xbc_ref, z_ref, dt_ref, st0_ref, dtb_ref, alog_ref, dsk_ref, nrm_ref, expand_ref,
         y_ref, sout_ref, st_sc) = refs
    else:
        (xbc_ref, z_ref, dt_ref, st0_ref, dtb_ref, alog_ref, dsk_ref, nrm_ref, expand_ref,
         cbuf_ref, cw_ref, cb_ref, y_ref, sout_ref, cout_ref, st_sc, xpad_sc) = refs
    c = pl.program_id(1)
    nchunks = pl.num_programs(1)

    @pl.when(c == 0)
    def _():
        for g in range(SSM_GROUPS):
            st_sc[g] = st0_ref[0, g].T

    if conv_done:
        act = xbc_ref[0]
    else:
        @pl.when(c == 0)
        def _():
            xpad_sc[0:CONV_PAD, :] = jnp.zeros((CONV_PAD, CONV_DIM), F32)
            xpad_sc[CONV_PAD - (CONV_WIDTH - 1):CONV_PAD, :] = cbuf_ref[0]

        xpad_sc[CONV_PAD:CONV_PAD + lc, :] = xbc_ref[0]
        xp = xpad_sc[...]
        conv = cw_ref[0:1, :] * xp
        for j in range(1, CONV_WIDTH):
            conv = cw_ref[j:j + 1, :] * xp + pltpu.roll(conv, 1, 0)
        conv = conv[CONV_PAD:CONV_PAD + lc] + cb_ref[...]
        act = conv * jax.nn.sigmoid(conv)

        @pl.when(c == nchunks - 1)
        def _():
            end = CONV_PAD + valid_last
            cout_ref[0] = xpad_sc[end - (CONV_WIDTH - 1):end, :]

        xpad_sc[0:CONV_PAD, :] = xpad_sc[lc:lc + CONV_PAD, :]

    dtr = dt_ref[0] + dtb_ref[...]
    dt = jnp.maximum(dtr, 0.0) + jnp.log1p(jnp.exp(-jnp.abs(dtr)))
    if valid_last < lc:
        row = lax.broadcasted_iota(jnp.int32, dt.shape, 0)
        dt = jnp.where(row < valid_last, dt, 0.0)
    a = dt * (-jnp.exp(alog_ref[...]))
    rr = lax.broadcasted_iota(jnp.int32, (lc, lc), 0)
    cc = lax.broadcasted_iota(jnp.int32, (lc, lc), 1)
    causal = rr >= cc
    tri = jnp.where(causal, 1.0, 0.0).astype(BF16)
    a_cs = sum(_dot(tri, part) for part in _split3(a))
    a_last = a_cs[lc - 1:lc, :]
    a_cs_t = a_cs.T

    expand = expand_ref[...]
    stack = jnp.concatenate([dt, dt * jnp.exp(a_last - a_cs), jnp.exp(a_cs)], axis=0)
    ex = _dot(stack.astype(BF16), expand)
    w_dt = ex[0:lc]
    w_state = ex[lc:2 * lc]
    w_off = ex[2 * lc:3 * lc]
    carry = jnp.broadcast_to(jnp.exp(a_last), (2 * SUBLANES, LANES))
    w_carry = sum(_dot(part, expand) for part in _split3(carry))[0:1]

    xs = act[:, :D_INNER].astype(F32)
    xdt = xs * w_dt
    xst = xs * w_state
    lane = lax.broadcasted_iota(jnp.int32, (lc, LANES), 1)
    lo_half = lane < SSM_HEAD_DIM

    for g in range(SSM_GROUPS):
        gs = slice(g * GROUP_WIDTH, (g + 1) * GROUP_WIDTH)
        bg = act[:, D_INNER + g * SSM_STATE:D_INNER + (g + 1) * SSM_STATE]
        cg = act[:, D_INNER + SSM_GROUPS * SSM_STATE + g * SSM_STATE:
                 D_INNER + SSM_GROUPS * SSM_STATE + (g + 1) * SSM_STATE]
        bgb = bg.astype(BF16)
        cgb = cg.astype(BF16)
        cb = _dot_nt(cgb, bgb)
        st_prev = st_sc[g]
        y_off = _dot(cgb, st_prev.astype(BF16)) * w_off[:, gs]
        st_sc[g] = w_carry[:, gs] * st_prev + _dot(bg.astype(F32).T.astype(BF16), xst[:, gs].astype(BF16))

        pairs = []
        for pr in range(SSM_HEADS_PER_GROUP // 2):
            x_pair = xdt[:, g * GROUP_WIDTH + pr * LANES:g * GROUP_WIDTH + (pr + 1) * LANES]
            y_pair = None
            for half in range(2):
                h = g * SSM_HEADS_PER_GROUP + pr * 2 + half
                seg = a_cs[:, h:h + 1] - a_cs_t[h:h + 1, :]
                decay = jnp.exp(jnp.where(causal, seg, -jnp.inf))
                mix = (cb * decay).astype(BF16)
                keep = lo_half if half == 0 else jnp.logical_not(lo_half)
                contrib = _dot(mix, jnp.where(keep, x_pair, 0.0).astype(BF16))
                y_pair = contrib if y_pair is None else y_pair + contrib
            pairs.append(y_pair)
        y_g = jnp.concatenate(pairs, axis=1) + y_off + dsk_ref[:, gs] * xs[:, gs]

        zg = z_ref[0, :, gs].astype(F32)
        yz = y_g * (zg * jax.nn.sigmoid(zg))
        y_ref[0, :, gs] = _rms(yz, nrm_ref[:, gs], SSM_NORM_EPS).astype(y_ref.dtype)

    @pl.when(c == nchunks - 1)
    def _():
        for g in range(SSM_GROUPS):
            sout_ref[0, g] = st_sc[g].T


def _ssd(xbc, z, dt, state0, p, lc, valid_last, conv_buf=None):
    b, l, _ = xbc.shape
    conv_done = conv_buf is None
    assert l % lc == 0 and CONV_WIDTH - 1 <= valid_last <= lc
    assert valid_last == lc or l == lc
    state_spec = pl.BlockSpec((1, SSM_GROUPS, GROUP_WIDTH, SSM_STATE), lambda bi, c: (bi, 0, 0, 0))
    in_specs = [
        pl.BlockSpec((1, lc, CONV_DIM), lambda bi, c: (bi, c, 0)),
        pl.BlockSpec((1, lc, D_INNER), lambda bi, c: (bi, c, 0)),
        pl.BlockSpec((1, lc, LANES), lambda bi, c: (bi, c, 0)),
        state_spec,
        pl.BlockSpec((1, LANES), lambda bi, c: (0, 0)),
        pl.BlockSpec((1, LANES), lambda bi, c: (0, 0)),
        pl.BlockSpec((1, D_INNER), lambda bi, c: (0, 0)),
        pl.BlockSpec((1, D_INNER), lambda bi, c: (0, 0)),
        pl.BlockSpec((LANES, D_INNER), lambda bi, c: (0, 0)),
    ]
    args = [xbc, z, dt, state0, p["dt_bias"], p["a_log"], p["d_skip"], p["ssm_norm"], p["expand"]]
    out_specs = [pl.BlockSpec((1, lc, D_INNER), lambda bi, c: (bi, c, 0)), state_spec]
    out_shape = [jax.ShapeDtypeStruct((b, l, D_INNER), BF16),
                 jax.ShapeDtypeStruct((b, SSM_GROUPS, GROUP_WIDTH, SSM_STATE), F32)]
    scratch = [pltpu.VMEM((SSM_GROUPS, SSM_STATE, GROUP_WIDTH), F32)]
    if not conv_done:
        conv_spec = pl.BlockSpec((1, CONV_WIDTH - 1, CONV_DIM), lambda bi, c: (bi, 0, 0))
        in_specs += [conv_spec, pl.BlockSpec((CONV_WIDTH, CONV_DIM), lambda bi, c: (0, 0)),
                     pl.BlockSpec((1, CONV_DIM), lambda bi, c: (0, 0))]
        args += [conv_buf, p["conv_w"], p["conv_b"]]
        out_specs.append(conv_spec)
        out_shape.append(jax.ShapeDtypeStruct((b, CONV_WIDTH - 1, CONV_DIM), F32))
        scratch.append(pltpu.VMEM((lc + CONV_PAD, CONV_DIM), F32))
    return pl.pallas_call(
        functools.partial(_ssd_kernel, lc=lc, valid_last=valid_last, conv_done=conv_done),
        grid=(b, l // lc),
        in_specs=in_specs,
        out_specs=out_specs,
        out_shape=out_shape,
        scratch_shapes=scratch,
        compiler_params=_cparams("arbitrary", "arbitrary"),
        name="ssd",
    )(*args)


def _merge_kernel(x_ref, att_ref, y_ref, g_ref, wa_ref, ws_ref, wo_ref, o_ref):
    ba = _dot(att_ref[...], wa_ref[...])
    bs = _dot(y_ref[...], ws_ref[...])
    g = jax.nn.sigmoid(g_ref[...].astype(F32))
    merged = g[:, :D_MODEL] * ba + g[:, D_MODEL:] * bs
    o_ref[...] = x_ref[...] + _dot(merged.astype(BF16), wo_ref[...])


def _merge(x, att, y, gates, wa, ws, wo, tm):
    t = x.shape[0]
    return pl.pallas_call(
        _merge_kernel,
        grid=(t // tm,),
        in_specs=[
            pl.BlockSpec((tm, D_MODEL), lambda i: (i, 0)),
            pl.BlockSpec((tm, ATT_WIDTH), lambda i: (i, 0)),
            pl.BlockSpec((tm, D_INNER), lambda i: (i, 0)),
            pl.BlockSpec((tm, 2 * D_MODEL), lambda i: (i, 0)),
            _const_spec((ATT_WIDTH, D_MODEL)),
            _const_spec((D_INNER, D_MODEL)),
            _const_spec((D_MODEL, D_MODEL)),
        ],
        out_specs=pl.BlockSpec((tm, D_MODEL), lambda i: (i, 0)),
        out_shape=jax.ShapeDtypeStruct((t, D_MODEL), F32),
        compiler_params=_cparams("arbitrary"),
        name="merge",
    )(x, att, y, gates, wa, ws, wo)


FFN_SPLIT = 2
FFN_TF = D_FF // FFN_SPLIT


def _ffn_kernel(x_ref, nw_ref, wgu_ref, wd_ref, nf_ref, *out_refs, final_norm):
    x = x_ref[...]
    hb = _rms(x, nw_ref[...], RMS_EPS).astype(BF16)
    acc = x
    for c in range(FFN_SPLIT):
        gt = _dot(hb, wgu_ref[:, c * FFN_TF:(c + 1) * FFN_TF])
        up = _dot(hb, wgu_ref[:, D_FF + c * FFN_TF:D_FF + (c + 1) * FFN_TF])
        act = (gt * jax.nn.sigmoid(gt) * up).astype(BF16)
        acc = acc + _dot(act, wd_ref[c * FFN_TF:(c + 1) * FFN_TF, :])
    out_refs[0][...] = acc
    if final_norm:
        out_refs[1][...] = _rms(acc, nf_ref[...], RMS_EPS)


def _ffn(x, norm_w, wgu, wd, norm_final, final_norm, tm):
    t = x.shape[0]
    row_spec = pl.BlockSpec((tm, D_MODEL), lambda i: (i, 0))
    n_out = 2 if final_norm else 1
    return pl.pallas_call(
        functools.partial(_ffn_kernel, final_norm=final_norm),
        grid=(t // tm,),
        in_specs=[
            row_spec,
            pl.BlockSpec((1, D_MODEL), lambda i: (0, 0)),
            _const_spec((D_MODEL, 2 * D_FF)),
            _const_spec((D_FF, D_MODEL)),
            pl.BlockSpec((1, D_MODEL), lambda i: (0, 0)),
        ],
        out_specs=(row_spec,) * n_out,
        out_shape=(jax.ShapeDtypeStruct((t, D_MODEL), F32),) * n_out,
        compiler_params=_cparams("arbitrary"),
        name="ffn",
    )(x, norm_w, wgu, wd, norm_final)


def _layer_params(l, norm_mix, w_in, lambda_q1, lambda_k1, lambda_q2, lambda_k2, attn_subln, conv_w, conv_b,
                  dt_bias, a_log, d_skip, ssm_norm, w_branch_att, w_branch_ssd, w_out, norm_ffn, w_gate_up,
                  w_down):
    w = w_in[l]
    n_qkv = Q_DIM + Q_DIM + ATT_WIDTH
    n_main = n_qkv + D_INNER + CONV_DIM
    w_main = jnp.concatenate([w[:, n_qkv:n_main], w[:, n_main + SSM_HEADS:]], axis=1)
    w_dt = jnp.pad(w[:, n_main:n_main + SSM_HEADS], ((0, 0), (0, LANES - SSM_HEADS)))
    head_of_channel = jnp.arange(D_INNER) // SSM_HEAD_DIM
    return {
        "norm_mix": norm_mix[l][None], "w_qkv": w[:, :n_qkv], "w_main": w_main, "w_dt": w_dt,
        "lamp": jnp.stack([lambda_q1[l], lambda_k1[l], lambda_q2[l], lambda_k2[l]]),
        "subln": attn_subln[l][None],
        "conv_w": conv_w[l], "conv_b": conv_b[l][None],
        "dt_bias": jnp.pad(dt_bias[l], (0, LANES - SSM_HEADS))[None],
        "a_log": jnp.pad(a_log[l], (0, LANES - SSM_HEADS))[None],
        "d_skip": jnp.repeat(d_skip[l], SSM_HEAD_DIM)[None],
        "ssm_norm": ssm_norm[l][None],
        "expand": (jnp.arange(LANES)[:, None] == head_of_channel[None, :]).astype(BF16),
        "wa": w_branch_att[l].astype(BF16), "ws": w_branch_ssd[l].astype(BF16), "wo": w_out[l].astype(BF16),
        "norm_ffn": norm_ffn[l][None], "wgu": w_gate_up[l].astype(BF16), "wd": w_down[l].astype(BF16),
    }


def _pad_time(x, b, n, lc):
    return jnp.pad(x.reshape(b, n, x.shape[-1]), ((0, 0), (0, lc - n), (0, 0)))


def kernel(x_prompt, x_sample, cache_k, cache_v, state_conv, state_ssm, norm_mix, w_in, lambda_q1, lambda_k1,
           lambda_q2, lambda_k2, attn_subln, conv_w, conv_b, dt_bias, a_log, d_skip, ssm_norm, w_branch_att,
           w_branch_ssd, w_out, norm_ffn, w_gate_up, w_down, norm_final):
    bp, lp, _ = x_prompt.shape
    bs, ls, _ = x_sample.shape
    assert bp == 1
    past = cache_k.shape[2]
    tp, ts = bp * lp, bs * ls
    xp = x_prompt.reshape(tp, D_MODEL)
    xs = x_sample.reshape(ts, D_MODEL)
    cache_k = jnp.transpose(cache_k, (0, 1, 3, 4, 5, 2)).reshape(DEPTH, bs, Q_DIM, past)
    cache_v = cache_v.reshape(DEPTH, bs, past * ATT_HEADS, ATT_V_DIM)
    state_ssm = state_ssm.reshape(DEPTH, bs, SSM_GROUPS, GROUP_WIDTH, SSM_STATE)
    conv0 = jnp.zeros((CONV_PAD, CONV_DIM), F32)
    ssm0 = jnp.zeros((bp, SSM_GROUPS, GROUP_WIDTH, SSM_STATE), F32)
    nf = norm_final[None]

    tm_p = 1024 if tp % 1024 == 0 else 128
    tm_row = 512 if tp % 512 == 0 else 128
    tq = ATTN_TQ if lp % ATTN_TQ == 0 else CHUNK
    tk = tq // 2

    w_in = w_in.astype(BF16)
    conv_p, ssm_p, conv_s, ssm_s = [], [], [], []
    kv_p = kv_s = None
    yp = ys = None
    for l in range(DEPTH):
        p = _layer_params(l, norm_mix, w_in, lambda_q1, lambda_k1, lambda_q2, lambda_k2, attn_subln, conv_w,
                          conv_b, dt_bias, a_log, d_skip, ssm_norm, w_branch_att, w_branch_ssd, w_out,
                          norm_ffn, w_gate_up, w_down)
        lam_init = 0.8 - 0.6 * math.exp(-0.3 * l)
        last = l == DEPTH - 1

        q, k_all, kb, v_all, vb = _qkv_proj(xp, p["norm_mix"], p["w_qkv"], tm_p, stacks=kv_p, layer=l)
        kv_p = (k_all, v_all)
        z, act, gates, dt, tail = _in_proj(xp, p["norm_mix"], p["w_main"], p["w_dt"], tm_p,
                                           conv=(p["conv_w"], p["conv_b"], conv0))
        att = _attn_prompt(q, kb, vb, p["lamp"], p["subln"], lam_init, tq, tk, ATTN_UNROLL)
        y, s_new = _ssd(act.reshape(bp, lp, CONV_DIM), z.reshape(bp, lp, D_INNER), dt.reshape(bp, lp, LANES),
                        ssm0, p, SSD_CHUNK, SSD_CHUNK)
        x1 = _merge(xp, att, y.reshape(tp, D_INNER), gates, p["wa"], p["ws"], p["wo"], tm_row)
        res = _ffn(x1, p["norm_ffn"], p["wgu"], p["wd"], nf, last, tm_row)
        xp = res[0]
        if last:
            yp = res[1]
        conv_p.append(tail[-1, CONV_PAD - (CONV_WIDTH - 1):][None])
        ssm_p.append(s_new.reshape(bp, SSM_GROUPS, SSM_HEADS_PER_GROUP, SSM_HEAD_DIM, SSM_STATE))

        q, k_all, kb, v_all, vb = _qkv_proj(xs, p["norm_mix"], p["w_qkv"], ts, stacks=kv_s, layer=l)
        kv_s = (k_all, v_all)
        z, xbc, gates, dt = _in_proj(xs, p["norm_mix"], p["w_main"], p["w_dt"], ts)
        att = _attn_sample(q.reshape(bs, ls, Q_DIM), cache_k, cache_v, kb.reshape(bs, ls, Q_DIM),
                           vb.reshape(bs, ls, ATT_WIDTH), p["lamp"], p["subln"], lam_init, l)
        y, s_new, c_new = _ssd(_pad_time(xbc, bs, ls, SSD_PAD_ROWS), _pad_time(z, bs, ls, SSD_PAD_ROWS),
                               _pad_time(dt, bs, ls, SSD_PAD_ROWS), state_ssm[l], p, SSD_PAD_ROWS, ls,
                               conv_buf=state_conv[l])
        x1 = _merge(xs, att.reshape(ts, ATT_WIDTH), y[:, :ls].reshape(ts, D_INNER), gates,
                    p["wa"], p["ws"], p["wo"], ts)
        res = _ffn(x1, p["norm_ffn"], p["wgu"], p["wd"], nf, last, ts)
        xs = res[0]
        if last:
            ys = res[1]
        conv_s.append(c_new)
        ssm_s.append(s_new.reshape(bs, SSM_GROUPS, SSM_HEADS_PER_GROUP, SSM_HEAD_DIM, SSM_STATE))

    return (yp.reshape(bp, lp, D_MODEL), ys.reshape(bs, ls, D_MODEL),
            kv_p[0].reshape(DEPTH, bp, lp, ATT_HEADS, 2, ATT_HEAD_DIM),
            kv_p[1].reshape(DEPTH, bp, lp, ATT_HEADS, ATT_V_DIM),
            jnp.stack(conv_p), jnp.stack(ssm_p),
            kv_s[0].reshape(DEPTH, bs, ls, ATT_HEADS, 2, ATT_HEAD_DIM),
            kv_s[1].reshape(DEPTH, bs, ls, ATT_HEADS, ATT_V_DIM),
            jnp.stack(conv_s), jnp.stack(ssm_s))
```

```python
import functools
import math

import jax
import jax.numpy as jnp
from jax import lax
from jax.experimental import pallas as pl
from jax.experimental.pallas import tpu as pltpu

F32 = jnp.float32
BF16 = jnp.bfloat16
LOG2E = math.log2(math.e)

D_MODEL = 1024
DEPTH = 2
CHUNK = 64
ATT_HEADS = 8
ATT_HEAD_DIM = 64
ATT_V_DIM = 128
ATT_WIDTH = 1024
Q_DIM = 1024
ATT_SUBLN_EPS = 1e-5
D_INNER = 2048
SSM_HEAD_DIM = 64
SSM_HEADS = 32
SSM_GROUPS = 8
SSM_HEADS_PER_GROUP = 4
SSM_STATE = 128
CONV_WIDTH = 4
CONV_DIM = 4096
SSM_NORM_EPS = 1e-5
D_FF = 2816
RMS_EPS = 1e-6
GROUP_WIDTH = SSM_HEADS_PER_GROUP * SSM_HEAD_DIM

LANES = 128
SUBLANES = 8
VMEM_LIMIT_BYTES = 56 * 1024 * 1024

PROJ_TN = 512
SSD_CHUNK = 256
SSD_PAD_ROWS = 128
CONV_PAD = 8
CONV_ROWS = 32
CONV_COLS = 512
SAMPLE_TS = 1024
ATTN_TQ = 1024
ATTN_UNROLL = 2


def _cparams(*sem):
    return pltpu.CompilerParams(dimension_semantics=sem, vmem_limit_bytes=VMEM_LIMIT_BYTES)


def _const_spec(shape):
    return pl.BlockSpec(shape, lambda *_: (0,) * len(shape), pipeline_mode=pl.Buffered(1))


def _rms(x, w, eps):
    return x * lax.rsqrt(jnp.mean(x * x, axis=-1, keepdims=True) + eps) * w


def _split3(x):
    hi = x.astype(BF16)
    r1 = x - hi.astype(F32)
    mid = r1.astype(BF16)
    lo = (r1 - mid.astype(F32)).astype(BF16)
    return hi, mid, lo


def _dot(a, b):
    return jnp.dot(a, b, preferred_element_type=F32)


def _dot_nt(a, b):
    return lax.dot_general(a, b, (((1,), (1,)), ((), ())), preferred_element_type=F32)


_NZ = D_INNER // PROJ_TN
_NX = CONV_DIM // PROJ_TN
_NG = 2 * D_MODEL // PROJ_TN
_OFF_X = _NZ
_OFF_G = _OFF_X + _NX
_N_COL_TILES = _OFF_G + _NG


def _qkv_kernel(*refs):
    x_ref, nw_ref, w_ref = refs[:3]
    q_ref, k_ref, kb_ref, v_ref, vb_ref, h_sc = refs[-6:]
    j = pl.program_id(1)

    @pl.when(j == 0)
    def _():
        h_sc[...] = _rms(x_ref[...], nw_ref[...], RMS_EPS).astype(BF16)
        q_ref[...] = (_dot(h_sc[...], w_ref[...]) * (LOG2E * ATT_HEAD_DIM ** -0.5)).astype(BF16)

    @pl.when(j == 1)
    def _():
        res = _dot(h_sc[...], w_ref[...])
        k_ref[...] = res
        kb_ref[...] = res.astype(BF16)

    @pl.when(j == 2)
    def _():
        res = _dot(h_sc[...], w_ref[...])
        v_ref[...] = res
        vb_ref[...] = res.astype(BF16)


def _qkv_proj(x, norm_w, w_qkv, tm, stacks=None, layer=0):
    t = x.shape[0]
    row = pl.BlockSpec((tm, Q_DIM), lambda i, j: (i, 0))
    stack = pl.BlockSpec((None, tm, Q_DIM), lambda i, j: (layer, i, 0))
    in_specs = [pl.BlockSpec((tm, D_MODEL), lambda i, j: (i, 0)), pl.BlockSpec((1, D_MODEL), lambda i, j: (0, 0)),
                pl.BlockSpec((D_MODEL, Q_DIM), lambda i, j: (0, j))]
    args = [x, norm_w, w_qkv]
    aliases = {}
    if stacks is not None:
        in_specs += [pl.BlockSpec(memory_space=pl.ANY)] * 2
        aliases = {len(args): 1, len(args) + 1: 3}
        args += list(stacks)
    return pl.pallas_call(
        _qkv_kernel,
        grid=(t // tm, 3),
        in_specs=in_specs,
        out_specs=[row, stack, row, stack, row],
        out_shape=[jax.ShapeDtypeStruct((t, Q_DIM), BF16),
                   jax.ShapeDtypeStruct((DEPTH, t, Q_DIM), F32),
                   jax.ShapeDtypeStruct((t, Q_DIM), BF16),
                   jax.ShapeDtypeStruct((DEPTH, t, ATT_WIDTH), F32),
                   jax.ShapeDtypeStruct((t, ATT_WIDTH), BF16)],
        scratch_shapes=[pltpu.VMEM((tm, D_MODEL), BF16)],
        input_output_aliases=aliases,
        compiler_params=_cparams("arbitrary", "arbitrary"),
        name="qkv_proj",
    )(*args)


def _in_proj_kernel(*refs, tm, fuse_conv):
    x_ref, nw_ref, w_ref, wdt_ref = refs[:4]
    pos = 4
    if fuse_conv:
        cw_ref, cb_ref, hist0_ref = refs[pos:pos + 3]
        pos += 3
    z_ref, xbc_ref, g_ref, dt_ref = refs[pos:pos + 4]
    pos += 4
    if fuse_conv:
        tail_ref = refs[pos]
        pos += 1
    h_sc = refs[pos]
    i = pl.program_id(0)
    j = pl.program_id(1)

    @pl.when(j == 0)
    def _():
        hb = _rms(x_ref[...], nw_ref[...], RMS_EPS).astype(BF16)
        h_sc[...] = hb
        dt_ref[...] = _dot(hb, wdt_ref[...])

    def tile():
        return _dot(h_sc[...], w_ref[...])

    @pl.when(j < _OFF_X)
    def _():
        z_ref[...] = tile().astype(z_ref.dtype)

    if not fuse_conv:
        @pl.when((j >= _OFF_X) & (j < _OFF_G))
        def _():
            xbc_ref[...] = tile()

        @pl.when(j >= _OFF_G)
        def _():
            g_ref[...] = tile().astype(g_ref.dtype)
        return

    hist_sc, raw_sc = refs[pos + 1], refs[pos + 2]

    @pl.when((i == 0) & (j == 0))
    def _():
        for c in range(_NX):
            hist_sc[c] = hist0_ref[:, c * PROJ_TN:(c + 1) * PROJ_TN]

    def activate(jc, slot):
        for c0 in range(0, PROJ_TN, CONV_COLS):
            cs = slice(c0, c0 + CONV_COLS)
            for r0 in range(0, tm, CONV_ROWS):
                before = hist_sc[jc, :, cs] if r0 == 0 else raw_sc[slot, r0 - CONV_PAD:r0, cs]
                xp = jnp.concatenate([before, raw_sc[slot, r0:r0 + CONV_ROWS, cs]], axis=0)
                conv = cw_ref[0:1, cs] * xp
                for tap in range(1, CONV_WIDTH):
                    conv = cw_ref[tap:tap + 1, cs] * xp + pltpu.roll(conv, 1, 0)
                conv = conv[CONV_PAD:] + cb_ref[:, cs]
                xbc_ref[r0:r0 + CONV_ROWS, cs] = (conv * jax.nn.sigmoid(conv)).astype(xbc_ref.dtype)
        last_rows = raw_sc[slot, tm - CONV_PAD:tm, :]
        hist_sc[jc] = last_rows
        tail_ref[...] = last_rows

    @pl.when(j == _OFF_X)
    def _():
        raw_sc[0] = tile()

    for slot in range(2):
        @pl.when((j > _OFF_X) & (j < _OFF_G) & (lax.rem(j - _OFF_X, 2) == slot))
        def _():
            activate(j - _OFF_X - 1, 1 - slot)
            raw_sc[slot] = tile()

    @pl.when(j == _OFF_G)
    def _():
        activate(_NX - 1, (_NX - 1) % 2)
        g_ref[...] = tile().astype(g_ref.dtype)

    @pl.when(j > _OFF_G)
    def _():
        g_ref[...] = tile().astype(g_ref.dtype)


def _in_proj(x, norm_w, w_main, w_dt, tm, conv=None):
    t = x.shape[0]
    fuse_conv = conv is not None

    tiled = fuse_conv

    def out_spec(off, n):
        if tiled:
            return pl.BlockSpec((None, tm, PROJ_TN), lambda i, j: (jnp.clip(j - off, 0, n - 1), i, 0))
        return pl.BlockSpec((tm, PROJ_TN), lambda i, j: (i, jnp.clip(j - off, 0, n - 1)))

    def out_sds(n, dtype):
        return jax.ShapeDtypeStruct((n, t, PROJ_TN) if tiled else (t, n * PROJ_TN), dtype)

    xbc_lag = 1 if fuse_conv else 0

    def xcol(i, j):
        return (0, jnp.clip(j - _OFF_X - xbc_lag, 0, _NX - 1))

    in_specs = [
        pl.BlockSpec((tm, D_MODEL), lambda i, j: (i, 0)),
        pl.BlockSpec((1, D_MODEL), lambda i, j: (0, 0)),
        pl.BlockSpec((D_MODEL, PROJ_TN), lambda i, j: (0, j)),
        pl.BlockSpec((D_MODEL, LANES), lambda i, j: (0, 0)),
    ]
    args = [x, norm_w, w_main, w_dt]
    if fuse_conv:
        in_specs += [pl.BlockSpec((CONV_WIDTH, PROJ_TN), xcol), pl.BlockSpec((1, PROJ_TN), xcol),
                     pl.BlockSpec((CONV_PAD, CONV_DIM), lambda i, j: (0, 0))]
        args += list(conv)
    out_shape = [
        out_sds(_NZ, BF16),
        out_sds(_NX, BF16 if fuse_conv else F32),
        out_sds(_NG, BF16),
        jax.ShapeDtypeStruct((t, LANES), F32),
    ]
    out_specs = [
        out_spec(0, _NZ), out_spec(_OFF_X + xbc_lag, _NX), out_spec(_OFF_G, _NG),
        pl.BlockSpec((tm, LANES), lambda i, j: (i, 0)),
    ]
    scratch = [pltpu.VMEM((tm, D_MODEL), BF16)]
    if fuse_conv:
        out_shape.append(jax.ShapeDtypeStruct((t // tm, CONV_PAD, CONV_DIM), F32))
        out_specs.append(pl.BlockSpec((None, CONV_PAD, PROJ_TN), lambda i, j: (i,) + xcol(i, j)))
        scratch += [pltpu.VMEM((_NX, CONV_PAD, PROJ_TN), F32), pltpu.VMEM((2, tm, PROJ_TN), F32)]
    return pl.pallas_call(
        functools.partial(_in_proj_kernel, tm=tm, fuse_conv=fuse_conv),
        grid=(t // tm, _N_COL_TILES),
        in_specs=in_specs,
        out_specs=out_specs,
        out_shape=out_shape,
        scratch_shapes=scratch,
        compiler_params=_cparams("arbitrary", "arbitrary"),
        name="in_proj",
    )(*args)


def _stack_q(q):
    lane = lax.broadcasted_iota(jnp.int32, q.shape, 1)
    zero = jnp.zeros_like(q)
    return jnp.concatenate([jnp.where(lane < ATT_HEAD_DIM, q, zero),
                            jnp.where(lane >= ATT_HEAD_DIM, q, zero)], axis=0)


def _diff_epilogue(o, n, lamp_ref, subln_ref, lam_init):
    lp = lamp_ref[...]
    lam = (jnp.exp(jnp.sum(lp[0:1] * lp[1:2], axis=-1, keepdims=True))
           - jnp.exp(jnp.sum(lp[2:3] * lp[3:4], axis=-1, keepdims=True)) + lam_init)
    a = o[:n] - lam * o[n:]
    return _rms(a, subln_ref[...], ATT_SUBLN_EPS) * (1.0 - lam_init)


def _attn_prompt_kernel(lamp_ref, subln_ref, q_ref, k_ref, v_ref, o_ref, m_sc, l_sc, acc_sc, s_sc, *,
                        tq, tk, nq, unroll, lam_init):
    qi = pl.program_id(1)
    half = tq // 2
    qs = jnp.concatenate([_stack_q(q_ref[0:half, :]), _stack_q(q_ref[half:tq, :])], axis=0)
    m_sc[...] = jnp.full(m_sc.shape, -jnp.inf, F32)
    l_sc[...] = jnp.zeros(l_sc.shape, F32)
    acc_sc[...] = jnp.zeros(acc_sc.shape, F32)
    n_lane_tiles = tk // LANES
    full_blocks = qi * (tq // tk)
    PLAIN, MASKED, MASKED_LATE = "plain", "masked", "masked_late"

    def scores(kblk, kind):
        start = pl.multiple_of(kblk * tk, tk)
        rows = qs[tq:] if kind == MASKED_LATE else qs
        s = _dot_nt(rows, k_ref[pl.ds(start, tk), :])
        if kind != PLAIN:
            r = lax.broadcasted_iota(jnp.int32, s.shape, 0)
            c = lax.broadcasted_iota(jnp.int32, s.shape, 1)
            if kind == MASKED_LATE:
                q_row = half + jnp.where(r >= half, r - half, r)
            else:
                q_row = jnp.where(r >= tq, half, 0) + lax.rem(r, half)
            k_row = c + (kblk - full_blocks) * tk
            s = jnp.where(k_row // CHUNK <= q_row // CHUNK, s, -jnp.inf)
        return s

    def absorb(kblk, s, state):
        if s.shape[0] != 2 * tq:
            late = absorb_rows(kblk, s, tuple(a[tq:] for a in state))
            return tuple(jnp.concatenate([a[:tq], b], axis=0) for a, b in zip(state, late))
        return absorb_rows(kblk, s, state)

    def absorb_rows(kblk, s, state):
        m_prev, l_prev, acc_prev = state
        vb = v_ref[pl.ds(pl.multiple_of(kblk * tk, tk), tk), :]
        m_new = jnp.maximum(m_prev, jnp.max(s, axis=-1, keepdims=True))
        alpha = jnp.exp2(m_prev - m_new)
        p = jnp.exp2(s - jnp.concatenate([m_new] * n_lane_tiles, axis=1))
        pv = _dot(p.astype(BF16), jnp.concatenate([vb, jnp.ones((tk, LANES), BF16)], axis=1))
        return m_new, alpha * l_prev + pv[:, ATT_V_DIM:], alpha * acc_prev + pv[:, :ATT_V_DIM]

    def pipeline(first, produce):
        state = (m_sc[...], l_sc[...], acc_sc[...])
        s = s_sc[...]
        for i, kind in enumerate(produce):
            s_next = None if kind is None else scores(first + i + 1, kind)
            state = absorb(first + i, s, state)
            s = s_next
        if s is not None:
            s_sc[...] = s
        m_sc[...], l_sc[...], acc_sc[...] = state

    diag = [MASKED, MASKED_LATE]

    @pl.when(qi == 0)
    def _():
        s_sc[...] = scores(0, MASKED)
        pipeline(0, diag[1:] + [None])

    @pl.when(qi > 0)
    def _():
        s_sc[...] = scores(0, PLAIN)

    def body(t, carry):
        pipeline(unroll * t, [PLAIN] * unroll)
        return carry

    plain_steps = jnp.maximum(full_blocks - 1, 0)
    lax.fori_loop(0, plain_steps // unroll, body, 0)
    for r in sorted({(len(diag) * q - 1) % unroll for q in range(1, nq)}):
        @pl.when((qi > 0) & (plain_steps % unroll == r))
        def _():
            pipeline(full_blocks - 1 - r, [PLAIN] * r + diag + [None])

    o = acc_sc[...] * (1.0 / l_sc[...])
    for g in range(2):
        o_ref[g * half:(g + 1) * half, :] = _diff_epilogue(
            o[g * tq:(g + 1) * tq], half, lamp_ref, subln_ref, lam_init).astype(o_ref.dtype)


def _attn_prompt(q, kb, vb, lamp, subln, lam_init, tq, tk, unroll):
    t = q.shape[0]
    assert t % tq == 0 and tq == 2 * tk and tk % LANES == 0 and tk % CHUNK == 0
    kern = functools.partial(_attn_prompt_kernel, tq=tq, tk=tk, nq=t // tq, unroll=unroll, lam_init=lam_init)
    return pl.pallas_call(
        kern,
        grid=(ATT_HEADS, t // tq),
        in_specs=[
            pl.BlockSpec((4, ATT_HEAD_DIM), lambda h, i: (0, 0)),
            pl.BlockSpec((1, ATT_V_DIM), lambda h, i: (0, 0)),
            pl.BlockSpec((tq, LANES), lambda h, i: (i, h)),
            pl.BlockSpec((t, LANES), lambda h, i: (0, h)),
            pl.BlockSpec((t, LANES), lambda h, i: (0, h)),
        ],
        out_specs=pl.BlockSpec((tq, LANES), lambda h, i: (i, h)),
        out_shape=jax.ShapeDtypeStruct((t, ATT_WIDTH), BF16),
        scratch_shapes=[pltpu.VMEM((2 * tq, LANES), F32), pltpu.VMEM((2 * tq, LANES), F32),
                        pltpu.VMEM((2 * tq, ATT_V_DIM), F32), pltpu.VMEM((2 * tq, tk), F32)],
        compiler_params=_cparams("arbitrary", "arbitrary"),
        name="attn_prompt",
    )(lamp, subln, q, kb, vb)


def _attn_sample_kernel(lamp_ref, subln_ref, q_ref, kt_ref, vc_ref, kn_ref, vn_ref, o_ref, m_sc, l_sc, acc_sc, *,
                        n, ts, lam_init):
    piece = pl.program_id(1)

    @pl.when(piece == 0)
    def _():
        m_sc[...] = jnp.full(m_sc.shape, -jnp.inf, F32)
        l_sc[...] = jnp.zeros(l_sc.shape, F32)
        acc_sc[...] = jnp.zeros(acc_sc.shape, F32)

    def update(h, s, v):
        width = s.shape[1]
        m_prev = m_sc[h]
        m_new = jnp.maximum(m_prev, jnp.max(s, axis=-1, keepdims=True))
        alpha = jnp.exp2(m_prev - m_new)
        m_wide = m_new[:, :width] if width <= LANES else jnp.concatenate([m_new] * (width // LANES), axis=1)
        p = jnp.exp2(s - m_wide)
        l_sc[h] = alpha * l_sc[h] + jnp.sum(p, axis=-1, keepdims=True)
        acc_sc[h] = alpha * acc_sc[h] + _dot(p.astype(BF16), v)
        m_sc[h] = m_new

    for h in range(ATT_HEADS):
        hs = slice(h * LANES, (h + 1) * LANES)
        qs = _stack_q(q_ref[0, :, hs])
        s = _dot(qs, kt_ref[0, 0, hs, :].astype(BF16))
        update(h, s, vc_ref[0, 0, pl.ds(h, ts, stride=ATT_HEADS), :].astype(BF16))

    @pl.when(piece == pl.num_programs(1) - 1)
    def _():
        for h in range(ATT_HEADS):
            hs = slice(h * LANES, (h + 1) * LANES)
            qs = _stack_q(q_ref[0, :, hs])
            update(h, _dot_nt(qs, kn_ref[0, :, hs]), vn_ref[0, :, hs])
            o = acc_sc[h] * (1.0 / l_sc[h])
            o_ref[0, :, hs] = _diff_epilogue(o, n, lamp_ref, subln_ref, lam_init).astype(o_ref.dtype)


def _attn_sample(q, cache_kt, cache_v, kb, vb, lamp, subln, lam_init, layer):
    b, n, _ = q.shape
    past = cache_kt.shape[3]
    ts = min(past, SAMPLE_TS)
    assert past % CHUNK == 0 and n <= CHUNK and past % ts == 0
    kern = functools.partial(_attn_sample_kernel, n=n, ts=ts, lam_init=lam_init)
    row_spec = pl.BlockSpec((1, n, Q_DIM), lambda bi, s: (bi, 0, 0))
    return pl.pallas_call(
        kern,
        grid=(b, past // ts),
        in_specs=[
            pl.BlockSpec((4, ATT_HEAD_DIM), lambda bi, s: (0, 0)),
            pl.BlockSpec((1, ATT_V_DIM), lambda bi, s: (0, 0)),
            row_spec,
            pl.BlockSpec((1, 1, Q_DIM, ts), lambda bi, s: (layer, bi, 0, s)),
            pl.BlockSpec((1, 1, ts * ATT_HEADS, ATT_V_DIM), lambda bi, s: (layer, bi, s, 0)),
            row_spec,
            row_spec,
        ],
        out_specs=row_spec,
        out_shape=jax.ShapeDtypeStruct((b, n, ATT_WIDTH), BF16),
        scratch_shapes=[pltpu.VMEM((ATT_HEADS, 2 * n, LANES), F32)] * 3,
        compiler_params=_cparams("arbitrary", "arbitrary"),
        name="attn_sample",
    )(lamp, subln, q, cache_kt, cache_v, kb, vb)


def _ssd_kernel(*refs, lc, valid_last, conv_done):
    if conv_done:
        (xbc_ref, z_ref, dt_ref, st0_ref, dtb_ref, alog_ref, dsk_ref, nrm_ref, expand_ref,
         y_ref, sout_ref, st_sc) = refs
    else:
        (xbc_ref, z_ref, dt_ref, st0_ref, dtb_ref, alog_ref, dsk_ref, nrm_ref, expand_ref,
         cbuf_ref, cw_ref, cb_ref, y_ref, sout_ref, cout_ref, st_sc, xpad_sc) = refs
    c = pl.program_id(1)
    nchunks = pl.num_programs(1)

    @pl.when(c == 0)
    def _():
        for g in range(SSM_GROUPS):
            st_sc[g] = st0_ref[0, g].T

    if conv_done:
        def act_cols(start, width):
            return xbc_ref[start // PROJ_TN, :, start % PROJ_TN:start % PROJ_TN + width]

        def z_cols(start, width):
            return z_ref[start // PROJ_TN, :, start % PROJ_TN:start % PROJ_TN + width]

        x_all = jnp.concatenate([xbc_ref[t] for t in range(D_INNER // PROJ_TN)], axis=1)
    else:
        @pl.when(c == 0)
        def _():
            xpad_sc[0:CONV_PAD, :] = jnp.zeros((CONV_PAD, CONV_DIM), F32)
            xpad_sc[CONV_PAD - (CONV_WIDTH - 1):CONV_PAD, :] = cbuf_ref[0]

        xpad_sc[CONV_PAD:CONV_PAD + lc, :] = xbc_ref[0]
        xp = xpad_sc[...]
        conv = cw_ref[0:1, :] * xp
        for j in range(1, CONV_WIDTH):
            conv = cw_ref[j:j + 1, :] * xp + pltpu.roll(conv, 1, 0)
        conv = conv[CONV_PAD:CONV_PAD + lc] + cb_ref[...]
        act = conv * jax.nn.sigmoid(conv)

        @pl.when(c == nchunks - 1)
        def _():
            end = CONV_PAD + valid_last
            cout_ref[0] = xpad_sc[end - (CONV_WIDTH - 1):end, :]

        xpad_sc[0:CONV_PAD, :] = xpad_sc[lc:lc + CONV_PAD, :]

        def act_cols(start, width):
            return act[:, start:start + width]

        def z_cols(start, width):
            return z_ref[0, :, start:start + width]

        x_all = act[:, :D_INNER]

    dtr = dt_ref[0] + dtb_ref[...]
    dt = jnp.maximum(dtr, 0.0) + jnp.log1p(jnp.exp(-jnp.abs(dtr)))
    if valid_last < lc:
        row = lax.broadcasted_iota(jnp.int32, dt.shape, 0)
        dt = jnp.where(row < valid_last, dt, 0.0)
    a = dt * (-jnp.exp(alog_ref[...]))
    rr = lax.broadcasted_iota(jnp.int32, (lc, lc), 0)
    cc = lax.broadcasted_iota(jnp.int32, (lc, lc), 1)
    causal = rr >= cc
    tri = jnp.where(causal, 1.0, 0.0).astype(BF16)
    a_cs = sum(_dot(tri, part) for part in _split3(a))
    a_last = a_cs[lc - 1:lc, :]
    a_cs_t = a_cs.T

    expand = expand_ref[...]
    stack = jnp.concatenate([dt, dt * jnp.exp(a_last - a_cs), jnp.exp(a_cs)], axis=0)
    ex = _dot(stack.astype(BF16), expand)
    w_dt = ex[0:lc]
    w_state = ex[lc:2 * lc]
    w_off = ex[2 * lc:3 * lc]
    carry = jnp.broadcast_to(jnp.exp(a_last), (2 * SUBLANES, LANES))
    w_carry = sum(_dot(part, expand) for part in _split3(carry))[0:1]

    xs = x_all.astype(F32)
    xdt = xs * w_dt
    xst = xs * w_state
    lane = lax.broadcasted_iota(jnp.int32, (lc, LANES), 1)
    lo_half = lane < SSM_HEAD_DIM

    for g in range(SSM_GROUPS):
        gs = slice(g * GROUP_WIDTH, (g + 1) * GROUP_WIDTH)
        bg = act_cols(D_INNER + g * SSM_STATE, SSM_STATE)
        cg = act_cols(D_INNER + SSM_GROUPS * SSM_STATE + g * SSM_STATE, SSM_STATE)
        bgb = bg.astype(BF16)
        cgb = cg.astype(BF16)
        cb = _dot_nt(cgb, bgb)
        st_prev = st_sc[g]
        y_off = _dot(cgb, st_prev.astype(BF16)) * w_off[:, gs]
        st_sc[g] = w_carry[:, gs] * st_prev + _dot(bg.astype(F32).T.astype(BF16), xst[:, gs].astype(BF16))

        pairs = []
        for pr in range(SSM_HEADS_PER_GROUP // 2):
            x_pair = xdt[:, g * GROUP_WIDTH + pr * LANES:g * GROUP_WIDTH + (pr + 1) * LANES]
            y_pair = None
            for half in range(2):
                h = g * SSM_HEADS_PER_GROUP + pr * 2 + half
                seg = a_cs[:, h:h + 1] - a_cs_t[h:h + 1, :]
                decay = jnp.exp(jnp.where(causal, seg, -jnp.inf))
                mix = (cb * decay).astype(BF16)
                keep = lo_half if half == 0 else jnp.logical_not(lo_half)
                contrib = _dot(mix, jnp.where(keep, x_pair, 0.0).astype(BF16))
                y_pair = contrib if y_pair is None else y_pair + contrib
            pairs.append(y_pair)
        y_g = jnp.concatenate(pairs, axis=1) + y_off + dsk_ref[:, gs] * xs[:, gs]

        zg = z_cols(g * GROUP_WIDTH, GROUP_WIDTH).astype(F32)
        yz = y_g * (zg * jax.nn.sigmoid(zg))
        y_ref[0, :, gs] = _rms(yz, nrm_ref[:, gs], SSM_NORM_EPS).astype(y_ref.dtype)

    @pl.when(c == nchunks - 1)
    def _():
        for g in range(SSM_GROUPS):
            sout_ref[0, g] = st_sc[g].T


def _ssd(xbc, z, dt, state0, p, lc, valid_last, conv_buf=None):
    conv_done = conv_buf is None
    b, l, _ = dt.shape
    assert l % lc == 0 and CONV_WIDTH - 1 <= valid_last <= lc
    assert valid_last == lc or l == lc
    state_spec = pl.BlockSpec((1, SSM_GROUPS, GROUP_WIDTH, SSM_STATE), lambda bi, c: (bi, 0, 0, 0))
    if conv_done:
        assert b == 1 and xbc.shape == (_NX, l, PROJ_TN) and z.shape == (_NZ, l, PROJ_TN)
        wide_specs = [pl.BlockSpec((_NX, lc, PROJ_TN), lambda bi, c: (0, c, 0)),
                      pl.BlockSpec((_NZ, lc, PROJ_TN), lambda bi, c: (0, c, 0))]
    else:
        wide_specs = [pl.BlockSpec((1, lc, CONV_DIM), lambda bi, c: (bi, c, 0)),
                      pl.BlockSpec((1, lc, D_INNER), lambda bi, c: (bi, c, 0))]
    in_specs = wide_specs + [
        pl.BlockSpec((1, lc, LANES), lambda bi, c: (bi, c, 0)),
        state_spec,
        pl.BlockSpec((1, LANES), lambda bi, c: (0, 0)),
        pl.BlockSpec((1, LANES), lambda bi, c: (0, 0)),
        pl.BlockSpec((1, D_INNER), lambda bi, c: (0, 0)),
        pl.BlockSpec((1, D_INNER), lambda bi, c: (0, 0)),
        pl.BlockSpec((LANES, D_INNER), lambda bi, c: (0, 0)),
    ]
    args = [xbc, z, dt, state0, p["dt_bias"], p["a_log"], p["d_skip"], p["ssm_norm"], p["expand"]]
    out_specs = [pl.BlockSpec((1, lc, D_INNER), lambda bi, c: (bi, c, 0)), state_spec]
    out_shape = [jax.ShapeDtypeStruct((b, l, D_INNER), BF16),
                 jax.ShapeDtypeStruct((b, SSM_GROUPS, GROUP_WIDTH, SSM_STATE), F32)]
    scratch = [pltpu.VMEM((SSM_GROUPS, SSM_STATE, GROUP_WIDTH), F32)]
    if not conv_done:
        conv_spec = pl.BlockSpec((1, CONV_WIDTH - 1, CONV_DIM), lambda bi, c: (bi, 0, 0))
        in_specs += [conv_spec, pl.BlockSpec((CONV_WIDTH, CONV_DIM), lambda bi, c: (0, 0)),
                     pl.BlockSpec((1, CONV_DIM), lambda bi, c: (0, 0))]
        args += [conv_buf, p["conv_w"], p["conv_b"]]
        out_specs.append(conv_spec)
        out_shape.append(jax.ShapeDtypeStruct((b, CONV_WIDTH - 1, CONV_DIM), F32))
        scratch.append(pltpu.VMEM((lc + CONV_PAD, CONV_DIM), F32))
    return pl.pallas_call(
        functools.partial(_ssd_kernel, lc=lc, valid_last=valid_last, conv_done=conv_done),
        grid=(b, l // lc),
        in_specs=in_specs,
        out_specs=out_specs,
        out_shape=out_shape,
        scratch_shapes=scratch,
        compiler_params=_cparams("arbitrary", "arbitrary"),
        name="ssd",
    )(*args)


def _merge_kernel(x_ref, att_ref, y_ref, g_ref, wa_ref, ws_ref, wo_ref, o_ref):
    ba = _dot(att_ref[...], wa_ref[...])
    bs = _dot(y_ref[...], ws_ref[...])
    if len(g_ref.shape) == 3:
        logits = jnp.concatenate([g_ref[t] for t in range(g_ref.shape[0])], axis=1)
    else:
        logits = g_ref[...]
    g = jax.nn.sigmoid(logits.astype(F32))
    merged = g[:, :D_MODEL] * ba + g[:, D_MODEL:] * bs
    o_ref[...] = x_ref[...] + _dot(merged.astype(BF16), wo_ref[...])


def _merge(x, att, y, gates, wa, ws, wo, tm):
    t = x.shape[0]
    if gates.ndim == 3:
        gate_spec = pl.BlockSpec((gates.shape[0], tm, gates.shape[2]), lambda i: (0, i, 0))
    else:
        gate_spec = pl.BlockSpec((tm, 2 * D_MODEL), lambda i: (i, 0))
    return pl.pallas_call(
        _merge_kernel,
        grid=(t // tm,),
        in_specs=[
            pl.BlockSpec((tm, D_MODEL), lambda i: (i, 0)),
            pl.BlockSpec((tm, ATT_WIDTH), lambda i: (i, 0)),
            pl.BlockSpec((tm, D_INNER), lambda i: (i, 0)),
            gate_spec,
            _const_spec((ATT_WIDTH, D_MODEL)),
            _const_spec((D_INNER, D_MODEL)),
            _const_spec((D_MODEL, D_MODEL)),
        ],
        out_specs=pl.BlockSpec((tm, D_MODEL), lambda i: (i, 0)),
        out_shape=jax.ShapeDtypeStruct((t, D_MODEL), F32),
        compiler_params=_cparams("arbitrary"),
        name="merge",
    )(x, att, y, gates, wa, ws, wo)


FFN_SPLIT = 2
FFN_TF = D_FF // FFN_SPLIT


def _ffn_kernel(x_ref, nw_ref, wgu_ref, wd_ref, nf_ref, *out_refs, final_norm):
    x = x_ref[...]
    hb = _rms(x, nw_ref[...], RMS_EPS).astype(BF16)
    acc = x
    for c in range(FFN_SPLIT):
        gt = _dot(hb, wgu_ref[:, c * FFN_TF:(c + 1) * FFN_TF])
        up = _dot(hb, wgu_ref[:, D_FF + c * FFN_TF:D_FF + (c + 1) * FFN_TF])
        act = (gt * jax.nn.sigmoid(gt) * up).astype(BF16)
        acc = acc + _dot(act, wd_ref[c * FFN_TF:(c + 1) * FFN_TF, :])
    out_refs[0][...] = acc
    if final_norm:
        out_refs[1][...] = _rms(acc, nf_ref[...], RMS_EPS)


def _ffn(x, norm_w, wgu, wd, norm_final, final_norm, tm):
    t = x.shape[0]
    row_spec = pl.BlockSpec((tm, D_MODEL), lambda i: (i, 0))
    n_out = 2 if final_norm else 1
    return pl.pallas_call(
        functools.partial(_ffn_kernel, final_norm=final_norm),
        grid=(t // tm,),
        in_specs=[
            row_spec,
            pl.BlockSpec((1, D_MODEL), lambda i: (0, 0)),
            _const_spec((D_MODEL, 2 * D_FF)),
            _const_spec((D_FF, D_MODEL)),
            pl.BlockSpec((1, D_MODEL), lambda i: (0, 0)),
        ],
        out_specs=(row_spec,) * n_out,
        out_shape=(jax.ShapeDtypeStruct((t, D_MODEL), F32),) * n_out,
        compiler_params=_cparams("arbitrary"),
        name="ffn",
    )(x, norm_w, wgu, wd, norm_final)


def _layer_params(l, norm_mix, w_in, lambda_q1, lambda_k1, lambda_q2, lambda_k2, attn_subln, conv_w, conv_b,
                  dt_bias, a_log, d_skip, ssm_norm, w_branch_att, w_branch_ssd, w_out, norm_ffn, w_gate_up,
                  w_down):
    w = w_in[l]
    n_qkv = Q_DIM + Q_DIM + ATT_WIDTH
    n_main = n_qkv + D_INNER + CONV_DIM
    w_main = jnp.concatenate([w[:, n_qkv:n_main], w[:, n_main + SSM_HEADS:]], axis=1)
    w_dt = jnp.pad(w[:, n_main:n_main + SSM_HEADS], ((0, 0), (0, LANES - SSM_HEADS)))
    head_of_channel = jnp.arange(D_INNER) // SSM_HEAD_DIM
    return {
        "norm_mix": norm_mix[l][None], "w_qkv": w[:, :n_qkv], "w_main": w_main, "w_dt": w_dt,
        "lamp": jnp.stack([lambda_q1[l], lambda_k1[l], lambda_q2[l], lambda_k2[l]]),
        "subln": attn_subln[l][None],
        "conv_w": conv_w[l], "conv_b": conv_b[l][None],
        "dt_bias": jnp.pad(dt_bias[l], (0, LANES - SSM_HEADS))[None],
        "a_log": jnp.pad(a_log[l], (0, LANES - SSM_HEADS))[None],
        "d_skip": jnp.repeat(d_skip[l], SSM_HEAD_DIM)[None],
        "ssm_norm": ssm_norm[l][None],
        "expand": (jnp.arange(LANES)[:, None] == head_of_channel[None, :]).astype(BF16),
        "wa": w_branch_att[l].astype(BF16), "ws": w_branch_ssd[l].astype(BF16), "wo": w_out[l].astype(BF16),
        "norm_ffn": norm_ffn[l][None], "wgu": w_gate_up[l].astype(BF16), "wd": w_down[l].astype(BF16),
    }


def _pad_time(x, b, n, lc):
    return jnp.pad(x.reshape(b, n, x.shape[-1]), ((0, 0), (0, lc - n), (0, 0)))


def kernel(x_prompt, x_sample, cache_k, cache_v, state_conv, state_ssm, norm_mix, w_in, lambda_q1, lambda_k1,
           lambda_q2, lambda_k2, attn_subln, conv_w, conv_b, dt_bias, a_log, d_skip, ssm_norm, w_branch_att,
           w_branch_ssd, w_out, norm_ffn, w_gate_up, w_down, norm_final):
    bp, lp, _ = x_prompt.shape
    bs, ls, _ = x_sample.shape
    assert bp == 1
    past = cache_k.shape[2]
    tp, ts = bp * lp, bs * ls
    xp = x_prompt.reshape(tp, D_MODEL)
    xs = x_sample.reshape(ts, D_MODEL)
    cache_k = jnp.transpose(cache_k, (0, 1, 3, 4, 5, 2)).reshape(DEPTH, bs, Q_DIM, past)
    cache_v = cache_v.reshape(DEPTH, bs, past * ATT_HEADS, ATT_V_DIM)
    state_ssm = state_ssm.reshape(DEPTH, bs, SSM_GROUPS, GROUP_WIDTH, SSM_STATE)
    conv0 = jnp.zeros((CONV_PAD, CONV_DIM), F32)
    ssm0 = jnp.zeros((bp, SSM_GROUPS, GROUP_WIDTH, SSM_STATE), F32)
    nf = norm_final[None]

    tm_p = 1024 if tp % 1024 == 0 else 128
    tm_row = 512 if tp % 512 == 0 else 128
    tq = ATTN_TQ if lp % ATTN_TQ == 0 else CHUNK
    tk = tq // 2

    w_in = w_in.astype(BF16)
    conv_p, ssm_p, conv_s, ssm_s = [], [], [], []
    kv_p = kv_s = None
    yp = ys = None
    for l in range(DEPTH):
        p = _layer_params(l, norm_mix, w_in, lambda_q1, lambda_k1, lambda_q2, lambda_k2, attn_subln, conv_w,
                          conv_b, dt_bias, a_log, d_skip, ssm_norm, w_branch_att, w_branch_ssd, w_out,
                          norm_ffn, w_gate_up, w_down)
        lam_init = 0.8 - 0.6 * math.exp(-0.3 * l)
        last = l == DEPTH - 1

        q, k_all, kb, v_all, vb = _qkv_proj(xp, p["norm_mix"], p["w_qkv"], tm_p, stacks=kv_p, layer=l)
        kv_p = (k_all, v_all)
        z, act, gates, dt, tail = _in_proj(xp, p["norm_mix"], p["w_main"], p["w_dt"], tm_p,
                                           conv=(p["conv_w"], p["conv_b"], conv0))
        att = _attn_prompt(q, kb, vb, p["lamp"], p["subln"], lam_init, tq, tk, ATTN_UNROLL)
        y, s_new = _ssd(act, z, dt.reshape(bp, lp, LANES), ssm0, p, SSD_CHUNK, SSD_CHUNK)
        x1 = _merge(xp, att, y.reshape(tp, D_INNER), gates, p["wa"], p["ws"], p["wo"], tm_row)
        res = _ffn(x1, p["norm_ffn"], p["wgu"], p["wd"], nf, last, tm_row)
        xp = res[0]
        if last:
            yp = res[1]
        conv_p.append(tail[-1, CONV_PAD - (CONV_WIDTH - 1):][None])
        ssm_p.append(s_new.reshape(bp, SSM_GROUPS, SSM_HEADS_PER_GROUP, SSM_HEAD_DIM, SSM_STATE))

        q, k_all, kb, v_all, vb = _qkv_proj(xs, p["norm_mix"], p["w_qkv"], ts, stacks=kv_s, layer=l)
        kv_s = (k_all, v_all)
        z, xbc, gates, dt = _in_proj(xs, p["norm_mix"], p["w_main"], p["w_dt"], ts)
        att = _attn_sample(q.reshape(bs, ls, Q_DIM), cache_k, cache_v, kb.reshape(bs, ls, Q_DIM),
                           vb.reshape(bs, ls, ATT_WIDTH), p["lamp"], p["subln"], lam_init, l)
        y, s_new, c_new = _ssd(_pad_time(xbc, bs, ls, SSD_PAD_ROWS), _pad_time(z, bs, ls, SSD_PAD_ROWS),
                               _pad_time(dt, bs, ls, SSD_PAD_ROWS), state_ssm[l], p, SSD_PAD_ROWS, ls,
                               conv_buf=state_conv[l])
        x1 = _merge(xs, att.reshape(ts, ATT_WIDTH), y[:, :ls].reshape(ts, D_INNER), gates,
                    p["wa"], p["ws"], p["wo"], ts)
        res = _ffn(x1, p["norm_ffn"], p["wgu"], p["wd"], nf, last, ts)
        xs = res[0]
        if last:
            ys = res[1]
        conv_s.append(c_new)
        ssm_s.append(s_new.reshape(bs, SSM_GROUPS, SSM_HEADS_PER_GROUP, SSM_HEAD_DIM, SSM_STATE))

    return (yp.reshape(bp, lp, D_MODEL), ys.reshape(bs, ls, D_MODEL),
            kv_p[0].reshape(DEPTH, bp, lp, ATT_HEADS, 2, ATT_HEAD_DIM),
            kv_p[1].reshape(DEPTH, bp, lp, ATT_HEADS, ATT_V_DIM),
            jnp.stack(conv_p), jnp.stack(ssm_p),
            kv_s[0].reshape(DEPTH, bs, ls, ATT_HEADS, 2, ATT_HEAD_DIM),
            kv_s[1].reshape(DEPTH, bs, ls, ATT_HEADS, ATT_V_DIM),
            jnp.stack(conv_s), jnp.stack(ssm_s))
```

```python
import functools
import math

import jax
import jax.numpy as jnp
from jax import lax
from jax.experimental import pallas as pl
from jax.experimental.pallas import tpu as pltpu

F32 = jnp.float32
BF16 = jnp.bfloat16
LOG2E = math.log2(math.e)

D_MODEL = 1024
DEPTH = 2
CHUNK = 64
ATT_HEADS = 8
ATT_HEAD_DIM = 64
ATT_V_DIM = 128
ATT_WIDTH = 1024
Q_DIM = 1024
ATT_SUBLN_EPS = 1e-5
D_INNER = 2048
SSM_HEAD_DIM = 64
SSM_HEADS = 32
SSM_GROUPS = 8
SSM_HEADS_PER_GROUP = 4
SSM_STATE = 128
CONV_WIDTH = 4
CONV_DIM = 4096
SSM_NORM_EPS = 1e-5
D_FF = 2816
RMS_EPS = 1e-6
GROUP_WIDTH = SSM_HEADS_PER_GROUP * SSM_HEAD_DIM

LANES = 128
SUBLANES = 8
VMEM_LIMIT_BYTES = 56 * 1024 * 1024

PROJ_TN = 512
SSD_CHUNK = 256
SSD_PAD_ROWS = 128
CONV_PAD = 8
CONV_ROWS = 32
CONV_COLS = 512
SAMPLE_TS = 1024
ATTN_TQ = 1024
ATTN_UNROLL = 2


def _cparams(*sem):
    return pltpu.CompilerParams(dimension_semantics=sem, vmem_limit_bytes=VMEM_LIMIT_BYTES)


def _const_spec(shape):
    return pl.BlockSpec(shape, lambda *_: (0,) * len(shape), pipeline_mode=pl.Buffered(1))


def _rms(x, w, eps):
    return x * lax.rsqrt(jnp.mean(x * x, axis=-1, keepdims=True) + eps) * w


def _split3(x):
    hi = x.astype(BF16)
    r1 = x - hi.astype(F32)
    mid = r1.astype(BF16)
    lo = (r1 - mid.astype(F32)).astype(BF16)
    return hi, mid, lo


def _dot(a, b):
    return jnp.dot(a, b, preferred_element_type=F32)


def _dot_nt(a, b):
    return lax.dot_general(a, b, (((1,), (1,)), ((), ())), preferred_element_type=F32)


_NZ = D_INNER // PROJ_TN
_NX = CONV_DIM // PROJ_TN
_NG = 2 * D_MODEL // PROJ_TN
_OFF_X = _NZ
_OFF_G = _OFF_X + _NX
_N_COL_TILES = _OFF_G + _NG
_W_TILE0 = (Q_DIM + Q_DIM + ATT_WIDTH) // PROJ_TN


def _qkv_kernel(*refs):
    x_ref, nw_ref, w_ref = refs[:3]
    q_ref, k_ref, kb_ref, v_ref, vb_ref, h_sc = refs[-6:]
    j = pl.program_id(1)

    @pl.when(j == 0)
    def _():
        h_sc[...] = _rms(x_ref[...], nw_ref[...], RMS_EPS).astype(BF16)
        q_ref[...] = (_dot(h_sc[...], w_ref[...]) * (LOG2E * ATT_HEAD_DIM ** -0.5)).astype(BF16)

    @pl.when(j == 1)
    def _():
        res = _dot(h_sc[...], w_ref[...])
        k_ref[...] = res
        kb_ref[...] = res.astype(BF16)

    @pl.when(j == 2)
    def _():
        res = _dot(h_sc[...], w_ref[...])
        v_ref[...] = res
        vb_ref[...] = res.astype(BF16)


def _qkv_proj(x, norm_w, w_all, tm, stacks=None, layer=0):
    t = x.shape[0]
    row = pl.BlockSpec((tm, Q_DIM), lambda i, j: (i, 0))
    stack = pl.BlockSpec((None, tm, Q_DIM), lambda i, j: (layer, i, 0))
    in_specs = [pl.BlockSpec((tm, D_MODEL), lambda i, j: (i, 0)), pl.BlockSpec((1, D_MODEL), lambda i, j: (0, 0)),
                pl.BlockSpec((None, D_MODEL, Q_DIM), lambda i, j: (layer, 0, j))]
    args = [x, norm_w, w_all]
    aliases = {}
    if stacks is not None:
        in_specs += [pl.BlockSpec(memory_space=pl.ANY)] * 2
        aliases = {len(args): 1, len(args) + 1: 3}
        args += list(stacks)
    return pl.pallas_call(
        _qkv_kernel,
        grid=(t // tm, 3),
        in_specs=in_specs,
        out_specs=[row, stack, row, stack, row],
        out_shape=[jax.ShapeDtypeStruct((t, Q_DIM), BF16),
                   jax.ShapeDtypeStruct((DEPTH, t, Q_DIM), F32),
                   jax.ShapeDtypeStruct((t, Q_DIM), BF16),
                   jax.ShapeDtypeStruct((DEPTH, t, ATT_WIDTH), F32),
                   jax.ShapeDtypeStruct((t, ATT_WIDTH), BF16)],
        scratch_shapes=[pltpu.VMEM((tm, D_MODEL), BF16)],
        input_output_aliases=aliases,
        compiler_params=_cparams("arbitrary", "arbitrary"),
        name="qkv_proj",
    )(*args)


def _in_proj_kernel(*refs, tm, fuse_conv):
    x_ref, nw_ref, w_ref, wg_ref, wdt_ref = refs[:5]
    pos = 5
    if fuse_conv:
        cw_ref, cb_ref, hist0_ref = refs[pos:pos + 3]
        pos += 3
    z_ref, xbc_ref, g_ref, dt_ref = refs[pos:pos + 4]
    pos += 4
    if fuse_conv:
        tail_ref = refs[pos]
        pos += 1
    h_sc = refs[pos]
    i = pl.program_id(0)
    j = pl.program_id(1)

    @pl.when(j == 0)
    def _():
        hb = _rms(x_ref[...], nw_ref[...], RMS_EPS).astype(BF16)
        h_sc[...] = hb
        dt_ref[...] = _dot(hb, wdt_ref[...])

    def tile(weight_ref=w_ref):
        return _dot(h_sc[...], weight_ref[...])

    @pl.when(j < _OFF_X)
    def _():
        z_ref[...] = tile().astype(z_ref.dtype)

    if not fuse_conv:
        @pl.when((j >= _OFF_X) & (j < _OFF_G))
        def _():
            xbc_ref[...] = tile()

        @pl.when(j >= _OFF_G)
        def _():
            g_ref[...] = tile(wg_ref).astype(g_ref.dtype)
        return

    hist_sc, raw_sc = refs[pos + 1], refs[pos + 2]

    @pl.when((i == 0) & (j == 0))
    def _():
        for c in range(_NX):
            hist_sc[c] = hist0_ref[:, c * PROJ_TN:(c + 1) * PROJ_TN]

    def activate(jc, slot):
        for c0 in range(0, PROJ_TN, CONV_COLS):
            cs = slice(c0, c0 + CONV_COLS)
            for r0 in range(0, tm, CONV_ROWS):
                before = hist_sc[jc, :, cs] if r0 == 0 else raw_sc[slot, r0 - CONV_PAD:r0, cs]
                xp = jnp.concatenate([before, raw_sc[slot, r0:r0 + CONV_ROWS, cs]], axis=0)
                conv = cw_ref[0:1, cs] * xp
                for tap in range(1, CONV_WIDTH):
                    conv = cw_ref[tap:tap + 1, cs] * xp + pltpu.roll(conv, 1, 0)
                conv = conv[CONV_PAD:] + cb_ref[:, cs]
                xbc_ref[r0:r0 + CONV_ROWS, cs] = (conv * jax.nn.sigmoid(conv)).astype(xbc_ref.dtype)
        last_rows = raw_sc[slot, tm - CONV_PAD:tm, :]
        hist_sc[jc] = last_rows
        tail_ref[...] = last_rows

    @pl.when(j == _OFF_X)
    def _():
        raw_sc[0] = tile()

    for slot in range(2):
        @pl.when((j > _OFF_X) & (j < _OFF_G) & (lax.rem(j - _OFF_X, 2) == slot))
        def _():
            activate(j - _OFF_X - 1, 1 - slot)
            raw_sc[slot] = tile()

    @pl.when(j == _OFF_G)
    def _():
        activate(_NX - 1, (_NX - 1) % 2)
        g_ref[...] = tile(wg_ref).astype(g_ref.dtype)

    @pl.when(j > _OFF_G)
    def _():
        g_ref[...] = tile(wg_ref).astype(g_ref.dtype)


def _in_proj(x, norm_w, w_all, w_gates, w_dt, tm, layer, conv=None):
    t = x.shape[0]
    fuse_conv = conv is not None

    def out_spec(off, n):
        return pl.BlockSpec((tm, PROJ_TN), lambda i, j: (i, jnp.clip(j - off, 0, n - 1)))

    xbc_lag = 1 if fuse_conv else 0

    def xcol(i, j):
        return (0, jnp.clip(j - _OFF_X - xbc_lag, 0, _NX - 1))

    in_specs = [
        pl.BlockSpec((tm, D_MODEL), lambda i, j: (i, 0)),
        pl.BlockSpec((1, D_MODEL), lambda i, j: (0, 0)),
        pl.BlockSpec((None, D_MODEL, PROJ_TN),
                     lambda i, j: (layer, 0, _W_TILE0 + jnp.clip(j, 0, _OFF_G - 1))),
        pl.BlockSpec((None, D_MODEL, PROJ_TN), lambda i, j: (layer, 0, jnp.clip(j - _OFF_G, 0, _NG - 1))),
        pl.BlockSpec((D_MODEL, LANES), lambda i, j: (0, 0)),
    ]
    args = [x, norm_w, w_all, w_gates, w_dt]
    if fuse_conv:
        in_specs += [pl.BlockSpec((CONV_WIDTH, PROJ_TN), xcol), pl.BlockSpec((1, PROJ_TN), xcol),
                     pl.BlockSpec((CONV_PAD, CONV_DIM), lambda i, j: (0, 0))]
        args += list(conv)
    out_shape = [
        jax.ShapeDtypeStruct((t, D_INNER), BF16),
        jax.ShapeDtypeStruct((t, CONV_DIM), BF16 if fuse_conv else F32),
        jax.ShapeDtypeStruct((t, 2 * D_MODEL), BF16),
        jax.ShapeDtypeStruct((t, LANES), F32),
    ]
    out_specs = [
        out_spec(0, _NZ), out_spec(_OFF_X + xbc_lag, _NX), out_spec(_OFF_G, _NG),
        pl.BlockSpec((tm, LANES), lambda i, j: (i, 0)),
    ]
    scratch = [pltpu.VMEM((tm, D_MODEL), BF16)]
    if fuse_conv:
        out_shape.append(jax.ShapeDtypeStruct((t // tm, CONV_PAD, CONV_DIM), F32))
        out_specs.append(pl.BlockSpec((None, CONV_PAD, PROJ_TN), lambda i, j: (i,) + xcol(i, j)))
        scratch += [pltpu.VMEM((_NX, CONV_PAD, PROJ_TN), F32), pltpu.VMEM((2, tm, PROJ_TN), F32)]
    return pl.pallas_call(
        functools.partial(_in_proj_kernel, tm=tm, fuse_conv=fuse_conv),
        grid=(t // tm, _N_COL_TILES),
        in_specs=in_specs,
        out_specs=out_specs,
        out_shape=out_shape,
        scratch_shapes=scratch,
        compiler_params=_cparams("arbitrary", "arbitrary"),
        name="in_proj",
    )(*args)


def _stack_q(q):
    lane = lax.broadcasted_iota(jnp.int32, q.shape, 1)
    zero = jnp.zeros_like(q)
    return jnp.concatenate([jnp.where(lane < ATT_HEAD_DIM, q, zero),
                            jnp.where(lane >= ATT_HEAD_DIM, q, zero)], axis=0)


def _diff_epilogue(o, n, lamp_ref, subln_ref, lam_init):
    lp = lamp_ref[...]
    lam = (jnp.exp(jnp.sum(lp[0:1] * lp[1:2], axis=-1, keepdims=True))
           - jnp.exp(jnp.sum(lp[2:3] * lp[3:4], axis=-1, keepdims=True)) + lam_init)
    a = o[:n] - lam * o[n:]
    return _rms(a, subln_ref[...], ATT_SUBLN_EPS) * (1.0 - lam_init)


def _attn_prompt_kernel(lamp_ref, subln_ref, q_ref, k_ref, v_ref, o_ref, m_sc, l_sc, acc_sc, s_sc, *,
                        tq, tk, nq, unroll, lam_init):
    qi = pl.program_id(1)
    half = tq // 2
    qs = jnp.concatenate([_stack_q(q_ref[0:half, :]), _stack_q(q_ref[half:tq, :])], axis=0)
    m_sc[...] = jnp.full(m_sc.shape, -jnp.inf, F32)
    l_sc[...] = jnp.zeros(l_sc.shape, F32)
    acc_sc[...] = jnp.zeros(acc_sc.shape, F32)
    n_lane_tiles = tk // LANES
    full_blocks = qi * (tq // tk)
    PLAIN, MASKED, MASKED_LATE = "plain", "masked", "masked_late"

    def scores(kblk, kind):
        start = pl.multiple_of(kblk * tk, tk)
        rows = qs[tq:] if kind == MASKED_LATE else qs
        s = _dot_nt(rows, k_ref[pl.ds(start, tk), :])
        if kind != PLAIN:
            r = lax.broadcasted_iota(jnp.int32, s.shape, 0)
            c = lax.broadcasted_iota(jnp.int32, s.shape, 1)
            if kind == MASKED_LATE:
                q_row = half + jnp.where(r >= half, r - half, r)
            else:
                q_row = jnp.where(r >= tq, half, 0) + lax.rem(r, half)
            k_row = c + (kblk - full_blocks) * tk
            s = jnp.where(k_row // CHUNK <= q_row // CHUNK, s, -jnp.inf)
        return s

    def absorb(kblk, s, state):
        if s.shape[0] != 2 * tq:
            late = absorb_rows(kblk, s, tuple(a[tq:] for a in state))
            return tuple(jnp.concatenate([a[:tq], b], axis=0) for a, b in zip(state, late))
        return absorb_rows(kblk, s, state)

    def absorb_rows(kblk, s, state):
        m_prev, l_prev, acc_prev = state
        vb = v_ref[pl.ds(pl.multiple_of(kblk * tk, tk), tk), :]
        m_new = jnp.maximum(m_prev, jnp.max(s, axis=-1, keepdims=True))
        alpha = jnp.exp2(m_prev - m_new)
        p = jnp.exp2(s - jnp.concatenate([m_new] * n_lane_tiles, axis=1))
        pv = _dot(p.astype(BF16), jnp.concatenate([vb, jnp.ones((tk, LANES), BF16)], axis=1))
        return m_new, alpha * l_prev + pv[:, ATT_V_DIM:], alpha * acc_prev + pv[:, :ATT_V_DIM]

    def pipeline(first, produce):
        state = (m_sc[...], l_sc[...], acc_sc[...])
        s = s_sc[...]
        for i, kind in enumerate(produce):
            s_next = None if kind is None else scores(first + i + 1, kind)
            state = absorb(first + i, s, state)
            s = s_next
        if s is not None:
            s_sc[...] = s
        m_sc[...], l_sc[...], acc_sc[...] = state

    diag = [MASKED, MASKED_LATE]

    @pl.when(qi == 0)
    def _():
        s_sc[...] = scores(0, MASKED)
        pipeline(0, diag[1:] + [None])

    @pl.when(qi > 0)
    def _():
        s_sc[...] = scores(0, PLAIN)

    def body(t, carry):
        pipeline(unroll * t, [PLAIN] * unroll)
        return carry

    plain_steps = jnp.maximum(full_blocks - 1, 0)
    lax.fori_loop(0, plain_steps // unroll, body, 0)
    for r in sorted({(len(diag) * q - 1) % unroll for q in range(1, nq)}):
        @pl.when((qi > 0) & (plain_steps % unroll == r))
        def _():
            pipeline(full_blocks - 1 - r, [PLAIN] * r + diag + [None])

    o = acc_sc[...] * (1.0 / l_sc[...])
    for g in range(2):
        o_ref[g * half:(g + 1) * half, :] = _diff_epilogue(
            o[g * tq:(g + 1) * tq], half, lamp_ref, subln_ref, lam_init).astype(o_ref.dtype)


def _attn_prompt(q, kb, vb, lamp, subln, lam_init, tq, tk, unroll):
    t = q.shape[0]
    assert t % tq == 0 and tq == 2 * tk and tk % LANES == 0 and tk % CHUNK == 0
    kern = functools.partial(_attn_prompt_kernel, tq=tq, tk=tk, nq=t // tq, unroll=unroll, lam_init=lam_init)
    return pl.pallas_call(
        kern,
        grid=(ATT_HEADS, t // tq),
        in_specs=[
            pl.BlockSpec((4, ATT_HEAD_DIM), lambda h, i: (0, 0)),
            pl.BlockSpec((1, ATT_V_DIM), lambda h, i: (0, 0)),
            pl.BlockSpec((tq, LANES), lambda h, i: (i, h)),
            pl.BlockSpec((t, LANES), lambda h, i: (0, h)),
            pl.BlockSpec((t, LANES), lambda h, i: (0, h)),
        ],
        out_specs=pl.BlockSpec((tq, LANES), lambda h, i: (i, h)),
        out_shape=jax.ShapeDtypeStruct((t, ATT_WIDTH), BF16),
        scratch_shapes=[pltpu.VMEM((2 * tq, LANES), F32), pltpu.VMEM((2 * tq, LANES), F32),
                        pltpu.VMEM((2 * tq, ATT_V_DIM), F32), pltpu.VMEM((2 * tq, tk), F32)],
        compiler_params=_cparams("arbitrary", "arbitrary"),
        name="attn_prompt",
    )(lamp, subln, q, kb, vb)


def _attn_sample_kernel(lamp_ref, subln_ref, q_ref, kt_ref, vc_ref, kn_ref, vn_ref, o_ref, m_sc, l_sc, acc_sc, *,
                        n, ts, lam_init):
    piece = pl.program_id(1)

    @pl.when(piece == 0)
    def _():
        m_sc[...] = jnp.full(m_sc.shape, -jnp.inf, F32)
        l_sc[...] = jnp.zeros(l_sc.shape, F32)
        acc_sc[...] = jnp.zeros(acc_sc.shape, F32)

    def update(h, s, v):
        width = s.shape[1]
        m_prev = m_sc[h]
        m_new = jnp.maximum(m_prev, jnp.max(s, axis=-1, keepdims=True))
        alpha = jnp.exp2(m_prev - m_new)
        m_wide = m_new[:, :width] if width <= LANES else jnp.concatenate([m_new] * (width // LANES), axis=1)
        p = jnp.exp2(s - m_wide)
        l_sc[h] = alpha * l_sc[h] + jnp.sum(p, axis=-1, keepdims=True)
        acc_sc[h] = alpha * acc_sc[h] + _dot(p.astype(BF16), v)
        m_sc[h] = m_new

    for h in range(ATT_HEADS):
        hs = slice(h * LANES, (h + 1) * LANES)
        qs = _stack_q(q_ref[0, :, hs])
        s = _dot(qs, kt_ref[0, 0, hs, :].astype(BF16))
        update(h, s, vc_ref[0, 0, pl.ds(h, ts, stride=ATT_HEADS), :].astype(BF16))

    @pl.when(piece == pl.num_programs(1) - 1)
    def _():
        for h in range(ATT_HEADS):
            hs = slice(h * LANES, (h + 1) * LANES)
            qs = _stack_q(q_ref[0, :, hs])
            update(h, _dot_nt(qs, kn_ref[0, :, hs]), vn_ref[0, :, hs])
            o = acc_sc[h] * (1.0 / l_sc[h])
            o_ref[0, :, hs] = _diff_epilogue(o, n, lamp_ref, subln_ref, lam_init).astype(o_ref.dtype)


def _attn_sample(q, cache_kt, cache_v, kb, vb, lamp, subln, lam_init, layer):
    b, n, _ = q.shape
    past = cache_kt.shape[3]
    ts = min(past, SAMPLE_TS)
    assert past % CHUNK == 0 and n <= CHUNK and past % ts == 0
    kern = functools.partial(_attn_sample_kernel, n=n, ts=ts, lam_init=lam_init)
    row_spec = pl.BlockSpec((1, n, Q_DIM), lambda bi, s: (bi, 0, 0))
    return pl.pallas_call(
        kern,
        grid=(b, past // ts),
        in_specs=[
            pl.BlockSpec((4, ATT_HEAD_DIM), lambda bi, s: (0, 0)),
            pl.BlockSpec((1, ATT_V_DIM), lambda bi, s: (0, 0)),
            row_spec,
            pl.BlockSpec((1, 1, Q_DIM, ts), lambda bi, s: (layer, bi, 0, s)),
            pl.BlockSpec((1, 1, ts * ATT_HEADS, ATT_V_DIM), lambda bi, s: (layer, bi, s, 0)),
            row_spec,
            row_spec,
        ],
        out_specs=row_spec,
        out_shape=jax.ShapeDtypeStruct((b, n, ATT_WIDTH), BF16),
        scratch_shapes=[pltpu.VMEM((ATT_HEADS, 2 * n, LANES), F32)] * 3,
        compiler_params=_cparams("arbitrary", "arbitrary"),
        name="attn_sample",
    )(lamp, subln, q, cache_kt, cache_v, kb, vb)


def _ssd_kernel(*refs, lc, valid_last, conv_done):
    if conv_done:
        (xbc_ref, z_ref, dt_ref, st0_ref, dtb_ref, alog_ref, dsk_ref, nrm_ref, expand_ref,
         y_ref, sout_ref, st_sc) = refs
    else:
        (xbc_ref, z_ref, dt_ref, st0_ref, dtb_ref, alog_ref, dsk_ref, nrm_ref, expand_ref,
         cbuf_ref, cw_ref, cb_ref, y_ref, sout_ref, cout_ref, st_sc, xpad_sc) = refs
    c = pl.program_id(1)
    nchunks = pl.num_programs(1)

    @pl.when(c == 0)
    def _():
        for g in range(SSM_GROUPS):
            st_sc[g] = st0_ref[0, g].T

    if conv_done:
        act = xbc_ref[0]
    else:
        @pl.when(c == 0)
        def _():
            xpad_sc[0:CONV_PAD, :] = jnp.zeros((CONV_PAD, CONV_DIM), F32)
            xpad_sc[CONV_PAD - (CONV_WIDTH - 1):CONV_PAD, :] = cbuf_ref[0]

        xpad_sc[CONV_PAD:CONV_PAD + lc, :] = xbc_ref[0]
        xp = xpad_sc[...]
        conv = cw_ref[0:1, :] * xp
        for j in range(1, CONV_WIDTH):
            conv = cw_ref[j:j + 1, :] * xp + pltpu.roll(conv, 1, 0)
        conv = conv[CONV_PAD:CONV_PAD + lc] + cb_ref[...]
        act = conv * jax.nn.sigmoid(conv)

        @pl.when(c == nchunks - 1)
        def _():
            end = CONV_PAD + valid_last
            cout_ref[0] = xpad_sc[end - (CONV_WIDTH - 1):end, :]

        xpad_sc[0:CONV_PAD, :] = xpad_sc[lc:lc + CONV_PAD, :]

    dtr = dt_ref[0] + dtb_ref[...]
    dt = jnp.maximum(dtr, 0.0) + jnp.log1p(jnp.exp(-jnp.abs(dtr)))
    if valid_last < lc:
        row = lax.broadcasted_iota(jnp.int32, dt.shape, 0)
        dt = jnp.where(row < valid_last, dt, 0.0)
    a = dt * (-jnp.exp(alog_ref[...]))
    rr = lax.broadcasted_iota(jnp.int32, (lc, lc), 0)
    cc = lax.broadcasted_iota(jnp.int32, (lc, lc), 1)
    causal = rr >= cc
    tri = jnp.where(causal, 1.0, 0.0).astype(BF16)
    a_cs = sum(_dot(tri, part) for part in _split3(a))
    a_last = a_cs[lc - 1:lc, :]
    a_cs_t = a_cs.T

    expand = expand_ref[...]
    stack = jnp.concatenate([dt, dt * jnp.exp(a_last - a_cs), jnp.exp(a_cs)], axis=0)
    ex = _dot(stack.astype(BF16), expand)
    w_dt = ex[0:lc]
    w_state = ex[lc:2 * lc]
    w_off = ex[2 * lc:3 * lc]
    carry = jnp.broadcast_to(jnp.exp(a_last), (2 * SUBLANES, LANES))
    w_carry = sum(_dot(part, expand) for part in _split3(carry))[0:1]

    xs = act[:, :D_INNER].astype(F32)
    xdt = xs * w_dt
    xst = xs * w_state
    lane = lax.broadcasted_iota(jnp.int32, (lc, LANES), 1)
    lo_half = lane < SSM_HEAD_DIM

    for g in range(SSM_GROUPS):
        gs = slice(g * GROUP_WIDTH, (g + 1) * GROUP_WIDTH)
        bg = act[:, D_INNER + g * SSM_STATE:D_INNER + (g + 1) * SSM_STATE]
        cg = act[:, D_INNER + SSM_GROUPS * SSM_STATE + g * SSM_STATE:
                 D_INNER + SSM_GROUPS * SSM_STATE + (g + 1) * SSM_STATE]
        bgb = bg.astype(BF16)
        cgb = cg.astype(BF16)
        cb = _dot_nt(cgb, bgb)
        st_prev = st_sc[g]
        y_off = _dot(cgb, st_prev.astype(BF16)) * w_off[:, gs]
        st_sc[g] = w_carry[:, gs] * st_prev + _dot(bg.astype(F32).T.astype(BF16), xst[:, gs].astype(BF16))

        pairs = []
        for pr in range(SSM_HEADS_PER_GROUP // 2):
            x_pair = xdt[:, g * GROUP_WIDTH + pr * LANES:g * GROUP_WIDTH + (pr + 1) * LANES]
            y_pair = None
            for half in range(2):
                h = g * SSM_HEADS_PER_GROUP + pr * 2 + half
                seg = a_cs[:, h:h + 1] - a_cs_t[h:h + 1, :]
                decay = jnp.exp(jnp.where(causal, seg, -jnp.inf))
                mix = (cb * decay).astype(BF16)
                keep = lo_half if half == 0 else jnp.logical_not(lo_half)
                contrib = _dot(mix, jnp.where(keep, x_pair, 0.0).astype(BF16))
                y_pair = contrib if y_pair is None else y_pair + contrib
            pairs.append(y_pair)
        y_g = jnp.concatenate(pairs, axis=1) + y_off + dsk_ref[:, gs] * xs[:, gs]

        zg = z_ref[0, :, gs].astype(F32)
        yz = y_g * (zg * jax.nn.sigmoid(zg))
        y_ref[0, :, gs] = _rms(yz, nrm_ref[:, gs], SSM_NORM_EPS).astype(y_ref.dtype)

    @pl.when(c == nchunks - 1)
    def _():
        for g in range(SSM_GROUPS):
            sout_ref[0, g] = st_sc[g].T


def _ssd(xbc, z, dt, state0, p, lc, valid_last, conv_buf=None):
    b, l, _ = xbc.shape
    conv_done = conv_buf is None
    assert l % lc == 0 and CONV_WIDTH - 1 <= valid_last <= lc
    assert valid_last == lc or l == lc
    state_spec = pl.BlockSpec((1, SSM_GROUPS, GROUP_WIDTH, SSM_STATE), lambda bi, c: (bi, 0, 0, 0))
    in_specs = [
        pl.BlockSpec((1, lc, CONV_DIM), lambda bi, c: (bi, c, 0)),
        pl.BlockSpec((1, lc, D_INNER), lambda bi, c: (bi, c, 0)),
        pl.BlockSpec((1, lc, LANES), lambda bi, c: (bi, c, 0)),
        state_spec,
        pl.BlockSpec((1, LANES), lambda bi, c: (0, 0)),
        pl.BlockSpec((1, LANES), lambda bi, c: (0, 0)),
        pl.BlockSpec((1, D_INNER), lambda bi, c: (0, 0)),
        pl.BlockSpec((1, D_INNER), lambda bi, c: (0, 0)),
        pl.BlockSpec((LANES, D_INNER), lambda bi, c: (0, 0)),
    ]
    args = [xbc, z, dt, state0, p["dt_bias"], p["a_log"], p["d_skip"], p["ssm_norm"], p["expand"]]
    out_specs = [pl.BlockSpec((1, lc, D_INNER), lambda bi, c: (bi, c, 0)), state_spec]
    out_shape = [jax.ShapeDtypeStruct((b, l, D_INNER), BF16),
                 jax.ShapeDtypeStruct((b, SSM_GROUPS, GROUP_WIDTH, SSM_STATE), F32)]
    scratch = [pltpu.VMEM((SSM_GROUPS, SSM_STATE, GROUP_WIDTH), F32)]
    if not conv_done:
        conv_spec = pl.BlockSpec((1, CONV_WIDTH - 1, CONV_DIM), lambda bi, c: (bi, 0, 0))
        in_specs += [conv_spec, pl.BlockSpec((CONV_WIDTH, CONV_DIM), lambda bi, c: (0, 0)),
                     pl.BlockSpec((1, CONV_DIM), lambda bi, c: (0, 0))]
        args += [conv_buf, p["conv_w"], p["conv_b"]]
        out_specs.append(conv_spec)
        out_shape.append(jax.ShapeDtypeStruct((b, CONV_WIDTH - 1, CONV_DIM), F32))
        scratch.append(pltpu.VMEM((lc + CONV_PAD, CONV_DIM), F32))
    return pl.pallas_call(
        functools.partial(_ssd_kernel, lc=lc, valid_last=valid_last, conv_done=conv_done),
        grid=(b, l // lc),
        in_specs=in_specs,
        out_specs=out_specs,
        out_shape=out_shape,
        scratch_shapes=scratch,
        compiler_params=_cparams("arbitrary", "arbitrary"),
        name="ssd",
    )(*args)


FFN_SPLIT = 2
FFN_TF = D_FF // FFN_SPLIT


def _mix_ffn_kernel(x_ref, att_ref, y_ref, g_ref, wa_ref, ws_ref, wo_ref, nw_ref, wgu_ref, wd_ref, nf_ref,
                    *out_refs, final_norm):
    ba = _dot(att_ref[...], wa_ref[...])
    bs = _dot(y_ref[...], ws_ref[...])
    g = jax.nn.sigmoid(g_ref[...].astype(F32))
    merged = g[:, :D_MODEL] * ba + g[:, D_MODEL:] * bs
    x = x_ref[...] + _dot(merged.astype(BF16), wo_ref[...])
    hb = _rms(x, nw_ref[...], RMS_EPS).astype(BF16)
    acc = x
    for c in range(FFN_SPLIT):
        gt = _dot(hb, wgu_ref[:, c * FFN_TF:(c + 1) * FFN_TF])
        up = _dot(hb, wgu_ref[:, D_FF + c * FFN_TF:D_FF + (c + 1) * FFN_TF])
        act = (gt * jax.nn.sigmoid(gt) * up).astype(BF16)
        acc = acc + _dot(act, wd_ref[c * FFN_TF:(c + 1) * FFN_TF, :])
    out_refs[0][...] = acc
    if final_norm:
        out_refs[1][...] = _rms(acc, nf_ref[...], RMS_EPS)


def _mix_ffn(x, att, y, gates, p, norm_final, final_norm, tm):
    t = x.shape[0]
    row_spec = pl.BlockSpec((tm, D_MODEL), lambda i: (i, 0))
    vec_spec = pl.BlockSpec((1, D_MODEL), lambda i: (0, 0))
    n_out = 2 if final_norm else 1
    return pl.pallas_call(
        functools.partial(_mix_ffn_kernel, final_norm=final_norm),
        grid=(t // tm,),
        in_specs=[
            row_spec,
            pl.BlockSpec((tm, ATT_WIDTH), lambda i: (i, 0)),
            pl.BlockSpec((tm, D_INNER), lambda i: (i, 0)),
            pl.BlockSpec((tm, 2 * D_MODEL), lambda i: (i, 0)),
            _const_spec((ATT_WIDTH, D_MODEL)),
            _const_spec((D_INNER, D_MODEL)),
            _const_spec((D_MODEL, D_MODEL)),
            vec_spec,
            _const_spec((D_MODEL, 2 * D_FF)),
            _const_spec((D_FF, D_MODEL)),
            vec_spec,
        ],
        out_specs=(row_spec,) * n_out,
        out_shape=(jax.ShapeDtypeStruct((t, D_MODEL), F32),) * n_out,
        compiler_params=_cparams("arbitrary"),
        name="mix_ffn",
    )(x, att, y, gates, p["wa"], p["ws"], p["wo"], p["norm_ffn"], p["wgu"], p["wd"], norm_final)


def _layer_params(l, norm_mix, w_in, lambda_q1, lambda_k1, lambda_q2, lambda_k2, attn_subln, conv_w, conv_b,
                  dt_bias, a_log, d_skip, ssm_norm, w_branch_att, w_branch_ssd, w_out, norm_ffn, w_gate_up,
                  w_down):
    n_main = Q_DIM + Q_DIM + ATT_WIDTH + D_INNER + CONV_DIM
    w_dt = jnp.pad(w_in[l, :, n_main:n_main + SSM_HEADS], ((0, 0), (0, LANES - SSM_HEADS)))
    head_of_channel = jnp.arange(D_INNER) // SSM_HEAD_DIM
    return {
        "norm_mix": norm_mix[l][None], "w_dt": w_dt,
        "lamp": jnp.stack([lambda_q1[l], lambda_k1[l], lambda_q2[l], lambda_k2[l]]),
        "subln": attn_subln[l][None],
        "conv_w": conv_w[l], "conv_b": conv_b[l][None],
        "dt_bias": jnp.pad(dt_bias[l], (0, LANES - SSM_HEADS))[None],
        "a_log": jnp.pad(a_log[l], (0, LANES - SSM_HEADS))[None],
        "d_skip": jnp.repeat(d_skip[l], SSM_HEAD_DIM)[None],
        "ssm_norm": ssm_norm[l][None],
        "expand": (jnp.arange(LANES)[:, None] == head_of_channel[None, :]).astype(BF16),
        "wa": w_branch_att[l].astype(BF16), "ws": w_branch_ssd[l].astype(BF16), "wo": w_out[l].astype(BF16),
        "norm_ffn": norm_ffn[l][None], "wgu": w_gate_up[l].astype(BF16), "wd": w_down[l].astype(BF16),
    }


def _pad_time(x, b, n, lc):
    return jnp.pad(x.reshape(b, n, x.shape[-1]), ((0, 0), (0, lc - n), (0, 0)))


def kernel(x_prompt, x_sample, cache_k, cache_v, state_conv, state_ssm, norm_mix, w_in, lambda_q1, lambda_k1,
           lambda_q2, lambda_k2, attn_subln, conv_w, conv_b, dt_bias, a_log, d_skip, ssm_norm, w_branch_att,
           w_branch_ssd, w_out, norm_ffn, w_gate_up, w_down, norm_final):
    bp, lp, _ = x_prompt.shape
    bs, ls, _ = x_sample.shape
    assert bp == 1
    past = cache_k.shape[2]
    tp, ts = bp * lp, bs * ls
    xp = x_prompt.reshape(tp, D_MODEL)
    xs = x_sample.reshape(ts, D_MODEL)
    cache_k = jnp.transpose(cache_k, (0, 1, 3, 4, 5, 2)).reshape(DEPTH, bs, Q_DIM, past)
    cache_v = cache_v.reshape(DEPTH, bs, past * ATT_HEADS, ATT_V_DIM)
    state_ssm = state_ssm.reshape(DEPTH, bs, SSM_GROUPS, GROUP_WIDTH, SSM_STATE)
    conv0 = jnp.zeros((CONV_PAD, CONV_DIM), F32)
    ssm0 = jnp.zeros((bp, SSM_GROUPS, GROUP_WIDTH, SSM_STATE), F32)
    nf = norm_final[None]

    tm_p = 1024 if tp % 1024 == 0 else 128
    tm_row = 512 if tp % 512 == 0 else 128
    tq = ATTN_TQ if lp % ATTN_TQ == 0 else CHUNK
    tk = tq // 2

    w_in = w_in.astype(BF16)
    w_gates = w_in[:, :, w_in.shape[2] - 2 * D_MODEL:]
    conv_p, ssm_p, conv_s, ssm_s = [], [], [], []
    kv_p = kv_s = None
    yp = ys = None
    for l in range(DEPTH):
        p = _layer_params(l, norm_mix, w_in, lambda_q1, lambda_k1, lambda_q2, lambda_k2, attn_subln, conv_w,
                          conv_b, dt_bias, a_log, d_skip, ssm_norm, w_branch_att, w_branch_ssd, w_out,
                          norm_ffn, w_gate_up, w_down)
        lam_init = 0.8 - 0.6 * math.exp(-0.3 * l)
        last = l == DEPTH - 1

        q, k_all, kb, v_all, vb = _qkv_proj(xp, p["norm_mix"], w_in, tm_p, stacks=kv_p, layer=l)
        kv_p = (k_all, v_all)
        z, act, gates, dt, tail = _in_proj(xp, p["norm_mix"], w_in, w_gates, p["w_dt"], tm_p, l,
                                           conv=(p["conv_w"], p["conv_b"], conv0))
        att = _attn_prompt(q, kb, vb, p["lamp"], p["subln"], lam_init, tq, tk, ATTN_UNROLL)
        y, s_new = _ssd(act.reshape(bp, lp, CONV_DIM), z.reshape(bp, lp, D_INNER), dt.reshape(bp, lp, LANES),
                        ssm0, p, SSD_CHUNK, SSD_CHUNK)
        res = _mix_ffn(xp, att, y.reshape(tp, D_INNER), gates, p, nf, last, tm_row)
        xp = res[0]
        if last:
            yp = res[1]
        conv_p.append(tail[-1, CONV_PAD - (CONV_WIDTH - 1):][None])
        ssm_p.append(s_new.reshape(bp, SSM_GROUPS, SSM_HEADS_PER_GROUP, SSM_HEAD_DIM, SSM_STATE))

        q, k_all, kb, v_all, vb = _qkv_proj(xs, p["norm_mix"], w_in, ts, stacks=kv_s, layer=l)
        kv_s = (k_all, v_all)
        z, xbc, gates, dt = _in_proj(xs, p["norm_mix"], w_in, w_gates, p["w_dt"], ts, l)
        att = _attn_sample(q.reshape(bs, ls, Q_DIM), cache_k, cache_v, kb.reshape(bs, ls, Q_DIM),
                           vb.reshape(bs, ls, ATT_WIDTH), p["lamp"], p["subln"], lam_init, l)
        y, s_new, c_new = _ssd(_pad_time(xbc, bs, ls, SSD_PAD_ROWS), _pad_time(z, bs, ls, SSD_PAD_ROWS),
                               _pad_time(dt, bs, ls, SSD_PAD_ROWS), state_ssm[l], p, SSD_PAD_ROWS, ls,
                               conv_buf=state_conv[l])
        res = _mix_ffn(xs, att.reshape(ts, ATT_WIDTH), y[:, :ls].reshape(ts, D_INNER), gates, p, nf, last, ts)
        xs = res[0]
        if last:
            ys = res[1]
        conv_s.append(c_new)
        ssm_s.append(s_new.reshape(bs, SSM_GROUPS, SSM_HEADS_PER_GROUP, SSM_HEAD_DIM, SSM_STATE))

    return (yp.reshape(bp, lp, D_MODEL), ys.reshape(bs, ls, D_MODEL),
            kv_p[0].reshape(DEPTH, bp, lp, ATT_HEADS, 2, ATT_HEAD_DIM),
            kv_p[1].reshape(DEPTH, bp, lp, ATT_HEADS, ATT_V_DIM),
            jnp.stack(conv_p), jnp.stack(ssm_p),
            kv_s[0].reshape(DEPTH, bs, ls, ATT_HEADS, 2, ATT_HEAD_DIM),
            kv_s[1].reshape(DEPTH, bs, ls, ATT_HEADS, ATT_V_DIM),
            jnp.stack(conv_s), jnp.stack(ssm_s))
```

```python
import functools
import math

import jax
import jax.numpy as jnp
from jax import lax
from jax.experimental import pallas as pl
from jax.experimental.pallas import tpu as pltpu

F32 = jnp.float32
BF16 = jnp.bfloat16
LOG2E = math.log2(math.e)

D_MODEL = 1024
DEPTH = 2
CHUNK = 64
ATT_HEADS = 8
ATT_HEAD_DIM = 64
ATT_V_DIM = 128
ATT_WIDTH = 1024
Q_DIM = 1024
ATT_SUBLN_EPS = 1e-5
D_INNER = 2048
SSM_HEAD_DIM = 64
SSM_HEADS = 32
SSM_GROUPS = 8
SSM_HEADS_PER_GROUP = 4
SSM_STATE = 128
CONV_WIDTH = 4
CONV_DIM = 4096
SSM_NORM_EPS = 1e-5
D_FF = 2816
RMS_EPS = 1e-6
GROUP_WIDTH = SSM_HEADS_PER_GROUP * SSM_HEAD_DIM

LANES = 128
SUBLANES = 8
VMEM_LIMIT_BYTES = 56 * 1024 * 1024

PROJ_TN = 512
SSD_CHUNK = 256
SSD_PAD_ROWS = 128
CONV_PAD = 8
CONV_ROWS = 32
CONV_COLS = 512
SAMPLE_TS = 1024
ATTN_TQ = 1024
ATTN_UNROLL = 2


def _cparams(*sem):
    return pltpu.CompilerParams(dimension_semantics=sem, vmem_limit_bytes=VMEM_LIMIT_BYTES)


def _const_spec(shape):
    return pl.BlockSpec(shape, lambda *_: (0,) * len(shape), pipeline_mode=pl.Buffered(1))


def _rms(x, w, eps):
    return x * lax.rsqrt(jnp.mean(x * x, axis=-1, keepdims=True) + eps) * w


def _split3(x):
    hi = x.astype(BF16)
    r1 = x - hi.astype(F32)
    mid = r1.astype(BF16)
    lo = (r1 - mid.astype(F32)).astype(BF16)
    return hi, mid, lo


def _dot(a, b):
    return jnp.dot(a, b, preferred_element_type=F32)


def _dot_nt(a, b):
    return lax.dot_general(a, b, (((1,), (1,)), ((), ())), preferred_element_type=F32)


_NZ = D_INNER // PROJ_TN
_NX = CONV_DIM // PROJ_TN
_NG = 2 * D_MODEL // PROJ_TN
_OFF_X = _NZ
_OFF_G = _OFF_X + _NX
_N_COL_TILES = _OFF_G + _NG
_W_TILE0 = (Q_DIM + Q_DIM + ATT_WIDTH) // PROJ_TN


def _qkv_kernel(*refs):
    x_ref, nw_ref, w_ref = refs[:3]
    q_ref, k_ref, kb_ref, v_ref, vb_ref, h_sc = refs[-6:]
    j = pl.program_id(1)

    @pl.when(j == 0)
    def _():
        h_sc[...] = _rms(x_ref[...], nw_ref[...], RMS_EPS).astype(BF16)
        q_ref[...] = (_dot(h_sc[...], w_ref[...]) * (LOG2E * ATT_HEAD_DIM ** -0.5)).astype(BF16)

    @pl.when(j == 1)
    def _():
        res = _dot(h_sc[...], w_ref[...])
        k_ref[...] = res
        kb_ref[...] = res.astype(BF16)

    @pl.when(j == 2)
    def _():
        res = _dot(h_sc[...], w_ref[...])
        v_ref[...] = res
        vb_ref[...] = res.astype(BF16)


def _qkv_proj(x, norm_w, w_all, tm, stacks=None, layer=0):
    t = x.shape[0]
    row = pl.BlockSpec((tm, Q_DIM), lambda i, j: (i, 0))
    stack = pl.BlockSpec((None, tm, Q_DIM), lambda i, j: (layer, i, 0))
    in_specs = [pl.BlockSpec((tm, D_MODEL), lambda i, j: (i, 0)), pl.BlockSpec((1, D_MODEL), lambda i, j: (0, 0)),
                pl.BlockSpec((None, D_MODEL, Q_DIM), lambda i, j: (layer, 0, j))]
    args = [x, norm_w, w_all]
    aliases = {}
    if stacks is not None:
        in_specs += [pl.BlockSpec(memory_space=pl.ANY)] * 2
        aliases = {len(args): 1, len(args) + 1: 3}
        args += list(stacks)
    return pl.pallas_call(
        _qkv_kernel,
        grid=(t // tm, 3),
        in_specs=in_specs,
        out_specs=[row, stack, row, stack, row],
        out_shape=[jax.ShapeDtypeStruct((t, Q_DIM), BF16),
                   jax.ShapeDtypeStruct((DEPTH, t, Q_DIM), F32),
                   jax.ShapeDtypeStruct((t, Q_DIM), BF16),
                   jax.ShapeDtypeStruct((DEPTH, t, ATT_WIDTH), F32),
                   jax.ShapeDtypeStruct((t, ATT_WIDTH), BF16)],
        scratch_shapes=[pltpu.VMEM((tm, D_MODEL), BF16)],
        input_output_aliases=aliases,
        compiler_params=_cparams("arbitrary", "arbitrary"),
        name="qkv_proj",
    )(*args)


def _in_proj_kernel(*refs, tm, fuse_conv):
    x_ref, nw_ref, w_ref, wg_ref, wdt_ref = refs[:5]
    pos = 5
    if fuse_conv:
        cw_ref, cb_ref, hist0_ref = refs[pos:pos + 3]
        pos += 3
    z_ref, xbc_ref, g_ref, dt_ref = refs[pos:pos + 4]
    pos += 4
    if fuse_conv:
        tail_ref = refs[pos]
        pos += 1
    h_sc = refs[pos]
    i = pl.program_id(0)
    j = pl.program_id(1)

    @pl.when(j == 0)
    def _():
        hb = _rms(x_ref[...], nw_ref[...], RMS_EPS).astype(BF16)
        h_sc[...] = hb
        dt_ref[...] = _dot(hb, wdt_ref[...])

    def tile(weight_ref=w_ref):
        return _dot(h_sc[...], weight_ref[...])

    @pl.when(j < _OFF_X)
    def _():
        z_ref[...] = tile().astype(z_ref.dtype)

    if not fuse_conv:
        @pl.when((j >= _OFF_X) & (j < _OFF_G))
        def _():
            xbc_ref[...] = tile()

        @pl.when(j >= _OFF_G)
        def _():
            g_ref[...] = tile(wg_ref).astype(g_ref.dtype)
        return

    hist_sc, raw_sc = refs[pos + 1], refs[pos + 2]

    @pl.when((i == 0) & (j == 0))
    def _():
        for c in range(_NX):
            hist_sc[c] = hist0_ref[:, c * PROJ_TN:(c + 1) * PROJ_TN]

    def activate(jc, slot):
        for c0 in range(0, PROJ_TN, CONV_COLS):
            cs = slice(c0, c0 + CONV_COLS)
            for r0 in range(0, tm, CONV_ROWS):
                before = hist_sc[jc, :, cs] if r0 == 0 else raw_sc[slot, r0 - CONV_PAD:r0, cs]
                xp = jnp.concatenate([before, raw_sc[slot, r0:r0 + CONV_ROWS, cs]], axis=0)
                conv = cw_ref[0:1, cs] * xp
                for tap in range(1, CONV_WIDTH):
                    conv = cw_ref[tap:tap + 1, cs] * xp + pltpu.roll(conv, 1, 0)
                conv = conv[CONV_PAD:] + cb_ref[:, cs]
                xbc_ref[r0:r0 + CONV_ROWS, cs] = (conv * jax.nn.sigmoid(conv)).astype(xbc_ref.dtype)
        last_rows = raw_sc[slot, tm - CONV_PAD:tm, :]
        hist_sc[jc] = last_rows
        tail_ref[...] = last_rows

    @pl.when(j == _OFF_X)
    def _():
        raw_sc[0] = tile()

    for slot in range(2):
        @pl.when((j > _OFF_X) & (j < _OFF_G) & (lax.rem(j - _OFF_X, 2) == slot))
        def _():
            activate(j - _OFF_X - 1, 1 - slot)
            raw_sc[slot] = tile()

    @pl.when(j == _OFF_G)
    def _():
        activate(_NX - 1, (_NX - 1) % 2)
        g_ref[...] = tile(wg_ref).astype(g_ref.dtype)

    @pl.when(j > _OFF_G)
    def _():
        g_ref[...] = tile(wg_ref).astype(g_ref.dtype)


def _in_proj(x, norm_w, w_all, w_gates, w_dt, tm, layer, conv=None):
    t = x.shape[0]
    fuse_conv = conv is not None

    def out_spec(off, n):
        return pl.BlockSpec((tm, PROJ_TN), lambda i, j: (i, jnp.clip(j - off, 0, n - 1)))

    xbc_lag = 1 if fuse_conv else 0

    def xcol(i, j):
        return (0, jnp.clip(j - _OFF_X - xbc_lag, 0, _NX - 1))

    in_specs = [
        pl.BlockSpec((tm, D_MODEL), lambda i, j: (i, 0)),
        pl.BlockSpec((1, D_MODEL), lambda i, j: (0, 0)),
        pl.BlockSpec((None, D_MODEL, PROJ_TN),
                     lambda i, j: (layer, 0, _W_TILE0 + jnp.clip(j, 0, _OFF_G - 1))),
        pl.BlockSpec((None, D_MODEL, PROJ_TN), lambda i, j: (layer, 0, jnp.clip(j - _OFF_G, 0, _NG - 1))),
        pl.BlockSpec((D_MODEL, LANES), lambda i, j: (0, 0)),
    ]
    args = [x, norm_w, w_all, w_gates, w_dt]
    if fuse_conv:
        in_specs += [pl.BlockSpec((CONV_WIDTH, PROJ_TN), xcol), pl.BlockSpec((1, PROJ_TN), xcol),
                     pl.BlockSpec((CONV_PAD, CONV_DIM), lambda i, j: (0, 0))]
        args += list(conv)
    out_shape = [
        jax.ShapeDtypeStruct((t, D_INNER), BF16),
        jax.ShapeDtypeStruct((t, CONV_DIM), BF16 if fuse_conv else F32),
        jax.ShapeDtypeStruct((t, 2 * D_MODEL), BF16),
        jax.ShapeDtypeStruct((t, LANES), F32),
    ]
    out_specs = [
        out_spec(0, _NZ), out_spec(_OFF_X + xbc_lag, _NX), out_spec(_OFF_G, _NG),
        pl.BlockSpec((tm, LANES), lambda i, j: (i, 0)),
    ]
    scratch = [pltpu.VMEM((tm, D_MODEL), BF16)]
    if fuse_conv:
        out_shape.append(jax.ShapeDtypeStruct((t // tm, CONV_PAD, CONV_DIM), F32))
        out_specs.append(pl.BlockSpec((None, CONV_PAD, PROJ_TN), lambda i, j: (i,) + xcol(i, j)))
        scratch += [pltpu.VMEM((_NX, CONV_PAD, PROJ_TN), F32), pltpu.VMEM((2, tm, PROJ_TN), F32)]
    return pl.pallas_call(
        functools.partial(_in_proj_kernel, tm=tm, fuse_conv=fuse_conv),
        grid=(t // tm, _N_COL_TILES),
        in_specs=in_specs,
        out_specs=out_specs,
        out_shape=out_shape,
        scratch_shapes=scratch,
        compiler_params=_cparams("arbitrary", "arbitrary"),
        name="in_proj",
    )(*args)


def _stack_q(q):
    lane = lax.broadcasted_iota(jnp.int32, q.shape, 1)
    zero = jnp.zeros_like(q)
    return jnp.concatenate([jnp.where(lane < ATT_HEAD_DIM, q, zero),
                            jnp.where(lane >= ATT_HEAD_DIM, q, zero)], axis=0)


def _diff_epilogue(o, n, lamp_ref, subln_ref, lam_init):
    lp = lamp_ref[...]
    lam = (jnp.exp(jnp.sum(lp[0:1] * lp[1:2], axis=-1, keepdims=True))
           - jnp.exp(jnp.sum(lp[2:3] * lp[3:4], axis=-1, keepdims=True)) + lam_init)
    a = o[:n] - lam * o[n:]
    return _rms(a, subln_ref[...], ATT_SUBLN_EPS) * (1.0 - lam_init)


def _attn_prompt_kernel(lamp_ref, subln_ref, q_ref, k_ref, v_ref, o_ref, m_sc, l_sc, acc_sc, s_sc, *,
                        tq, tk, nq, unroll, lam_init):
    qi = pl.program_id(1)
    half = tq // 2
    qs = jnp.concatenate([_stack_q(q_ref[0:half, :]), _stack_q(q_ref[half:tq, :])], axis=0)
    m_sc[...] = jnp.full(m_sc.shape, -jnp.inf, F32)
    l_sc[...] = jnp.zeros(l_sc.shape, F32)
    acc_sc[...] = jnp.zeros(acc_sc.shape, F32)
    n_lane_tiles = tk // LANES
    full_blocks = qi * (tq // tk)
    PLAIN, MASKED, MASKED_LATE = "plain", "masked", "masked_late"

    def scores(kblk, kind):
        start = pl.multiple_of(kblk * tk, tk)
        rows = qs[tq:] if kind == MASKED_LATE else qs
        s = _dot_nt(rows, k_ref[pl.ds(start, tk), :])
        if kind != PLAIN:
            r = lax.broadcasted_iota(jnp.int32, s.shape, 0)
            c = lax.broadcasted_iota(jnp.int32, s.shape, 1)
            if kind == MASKED_LATE:
                q_row = half + jnp.where(r >= half, r - half, r)
            else:
                q_row = jnp.where(r >= tq, half, 0) + lax.rem(r, half)
            k_row = c + (kblk - full_blocks) * tk
            s = jnp.where(k_row // CHUNK <= q_row // CHUNK, s, -jnp.inf)
        return s

    def absorb(kblk, s, state):
        if s.shape[0] != 2 * tq:
            late = absorb_rows(kblk, s, tuple(a[tq:] for a in state))
            return tuple(jnp.concatenate([a[:tq], b], axis=0) for a, b in zip(state, late))
        return absorb_rows(kblk, s, state)

    def absorb_rows(kblk, s, state):
        m_prev, l_prev, acc_prev = state
        vb = v_ref[pl.ds(pl.multiple_of(kblk * tk, tk), tk), :]
        m_new = jnp.maximum(m_prev, jnp.max(s, axis=-1, keepdims=True))
        alpha = jnp.exp2(m_prev - m_new)
        p = jnp.exp2(s - jnp.concatenate([m_new] * n_lane_tiles, axis=1))
        pv = _dot(p.astype(BF16), jnp.concatenate([vb, jnp.ones((tk, LANES), BF16)], axis=1))
        return m_new, alpha * l_prev + pv[:, ATT_V_DIM:], alpha * acc_prev + pv[:, :ATT_V_DIM]

    def pipeline(first, produce):
        state = (m_sc[...], l_sc[...], acc_sc[...])
        s = s_sc[...]
        for i, kind in enumerate(produce):
            s_next = None if kind is None else scores(first + i + 1, kind)
            state = absorb(first + i, s, state)
            s = s_next
        if s is not None:
            s_sc[...] = s
        m_sc[...], l_sc[...], acc_sc[...] = state

    diag = [MASKED, MASKED_LATE]

    @pl.when(qi == 0)
    def _():
        s_sc[...] = scores(0, MASKED)
        pipeline(0, diag[1:] + [None])

    @pl.when(qi > 0)
    def _():
        s_sc[...] = scores(0, PLAIN)

    def body(t, carry):
        pipeline(unroll * t, [PLAIN] * unroll)
        return carry

    plain_steps = jnp.maximum(full_blocks - 1, 0)
    lax.fori_loop(0, plain_steps // unroll, body, 0)
    for r in sorted({(len(diag) * q - 1) % unroll for q in range(1, nq)}):
        @pl.when((qi > 0) & (plain_steps % unroll == r))
        def _():
            pipeline(full_blocks - 1 - r, [PLAIN] * r + diag + [None])

    o = acc_sc[...] * (1.0 / l_sc[...])
    for g in range(2):
        o_ref[g * half:(g + 1) * half, :] = _diff_epilogue(
            o[g * tq:(g + 1) * tq], half, lamp_ref, subln_ref, lam_init).astype(o_ref.dtype)


def _attn_prompt(q, kb, vb, lamp, subln, lam_init, tq, tk, unroll):
    t = q.shape[0]
    assert t % tq == 0 and tq == 2 * tk and tk % LANES == 0 and tk % CHUNK == 0
    kern = functools.partial(_attn_prompt_kernel, tq=tq, tk=tk, nq=t // tq, unroll=unroll, lam_init=lam_init)
    return pl.pallas_call(
        kern,
        grid=(ATT_HEADS, t // tq),
        in_specs=[
            pl.BlockSpec((4, ATT_HEAD_DIM), lambda h, i: (0, 0)),
            pl.BlockSpec((1, ATT_V_DIM), lambda h, i: (0, 0)),
            pl.BlockSpec((tq, LANES), lambda h, i: (i, h)),
            pl.BlockSpec((t, LANES), lambda h, i: (0, h)),
            pl.BlockSpec((t, LANES), lambda h, i: (0, h)),
        ],
        out_specs=pl.BlockSpec((tq, LANES), lambda h, i: (i, h)),
        out_shape=jax.ShapeDtypeStruct((t, ATT_WIDTH), BF16),
        scratch_shapes=[pltpu.VMEM((2 * tq, LANES), F32), pltpu.VMEM((2 * tq, LANES), F32),
                        pltpu.VMEM((2 * tq, ATT_V_DIM), F32), pltpu.VMEM((2 * tq, tk), F32)],
        compiler_params=_cparams("arbitrary", "arbitrary"),
        name="attn_prompt",
    )(lamp, subln, q, kb, vb)


def _attn_sample_kernel(lamp_ref, subln_ref, q_ref, kt_ref, vc_ref, kn_ref, vn_ref, o_ref, m_sc, l_sc, acc_sc, *,
                        n, ts, lam_init):
    piece = pl.program_id(1)

    @pl.when(piece == 0)
    def _():
        m_sc[...] = jnp.full(m_sc.shape, -jnp.inf, F32)
        l_sc[...] = jnp.zeros(l_sc.shape, F32)
        acc_sc[...] = jnp.zeros(acc_sc.shape, F32)

    rows = 2 * n

    def head_cols(h):
        return slice(h * LANES, (h + 1) * LANES)

    def update(scores_of, values_of):
        s = jnp.concatenate([scores_of(_stack_q(q_ref[0, :, head_cols(h)]), h) for h in range(ATT_HEADS)], axis=0)
        width = s.shape[1]
        m_prev = m_sc[...]
        m_new = jnp.maximum(m_prev, jnp.max(s, axis=-1, keepdims=True))
        alpha = jnp.exp2(m_prev - m_new)
        m_wide = m_new[:, :width] if width <= LANES else jnp.concatenate([m_new] * (width // LANES), axis=1)
        p = jnp.exp2(s - m_wide)
        l_sc[...] = alpha * l_sc[...] + jnp.sum(p, axis=-1, keepdims=True)
        pb = p.astype(BF16)
        pv = jnp.concatenate([_dot(pb[h * rows:(h + 1) * rows], values_of(h)) for h in range(ATT_HEADS)], axis=0)
        acc_sc[...] = alpha * acc_sc[...] + pv
        m_sc[...] = m_new

    update(lambda qs, h: _dot(qs, kt_ref[0, 0, head_cols(h), :].astype(BF16)),
           lambda h: vc_ref[0, 0, pl.ds(h, ts, stride=ATT_HEADS), :].astype(BF16))

    @pl.when(piece == pl.num_programs(1) - 1)
    def _():
        update(lambda qs, h: _dot_nt(qs, kn_ref[0, :, head_cols(h)]), lambda h: vn_ref[0, :, head_cols(h)])
        o = acc_sc[...] * (1.0 / l_sc[...])
        for h in range(ATT_HEADS):
            o_ref[0, :, head_cols(h)] = _diff_epilogue(
                o[h * rows:(h + 1) * rows], n, lamp_ref, subln_ref, lam_init).astype(o_ref.dtype)


def _attn_sample(q, cache_kt, cache_v, kb, vb, lamp, subln, lam_init, layer):
    b, n, _ = q.shape
    past = cache_kt.shape[3]
    ts = min(past, SAMPLE_TS)
    assert past % CHUNK == 0 and n <= CHUNK and past % ts == 0
    kern = functools.partial(_attn_sample_kernel, n=n, ts=ts, lam_init=lam_init)
    row_spec = pl.BlockSpec((1, n, Q_DIM), lambda bi, s: (bi, 0, 0))
    return pl.pallas_call(
        kern,
        grid=(b, past // ts),
        in_specs=[
            pl.BlockSpec((4, ATT_HEAD_DIM), lambda bi, s: (0, 0)),
            pl.BlockSpec((1, ATT_V_DIM), lambda bi, s: (0, 0)),
            row_spec,
            pl.BlockSpec((1, 1, Q_DIM, ts), lambda bi, s: (layer, bi, 0, s)),
            pl.BlockSpec((1, 1, ts * ATT_HEADS, ATT_V_DIM), lambda bi, s: (layer, bi, s, 0)),
            row_spec,
            row_spec,
        ],
        out_specs=row_spec,
        out_shape=jax.ShapeDtypeStruct((b, n, ATT_WIDTH), BF16),
        scratch_shapes=[pltpu.VMEM((ATT_HEADS * 2 * n, LANES), F32)] * 3,
        compiler_params=_cparams("arbitrary", "arbitrary"),
        name="attn_sample",
    )(lamp, subln, q, cache_kt, cache_v, kb, vb)


def _ssd_kernel(*refs, lc, valid_last, conv_done):
    if conv_done:
        (xbc_ref, z_ref, dt_ref, st0_ref, dtb_ref, alog_ref, dsk_ref, nrm_ref, expand_ref,
         y_ref, sout_ref, st_sc) = refs
    else:
        (xbc_ref, z_ref, dt_ref, st0_ref, dtb_ref, alog_ref, dsk_ref, nrm_ref, expand_ref,
         cbuf_ref, cw_ref, cb_ref, y_ref, sout_ref, cout_ref, st_sc, xpad_sc) = refs
    c = pl.program_id(1)
    nchunks = pl.num_programs(1)

    @pl.when(c == 0)
    def _():
        for g in range(SSM_GROUPS):
            st_sc[g] = st0_ref[0, g].T

    if conv_done:
        act = xbc_ref[0]
    else:
        @pl.when(c == 0)
        def _():
            xpad_sc[0:CONV_PAD, :] = jnp.zeros((CONV_PAD, CONV_DIM), F32)
            xpad_sc[CONV_PAD - (CONV_WIDTH - 1):CONV_PAD, :] = cbuf_ref[0]

        xpad_sc[CONV_PAD:CONV_PAD + lc, :] = xbc_ref[0]
        xp = xpad_sc[...]
        conv = cw_ref[0:1, :] * xp
        for j in range(1, CONV_WIDTH):
            conv = cw_ref[j:j + 1, :] * xp + pltpu.roll(conv, 1, 0)
        conv = conv[CONV_PAD:CONV_PAD + lc] + cb_ref[...]
        act = conv * jax.nn.sigmoid(conv)

        @pl.when(c == nchunks - 1)
        def _():
            end = CONV_PAD + valid_last
            cout_ref[0] = xpad_sc[end - (CONV_WIDTH - 1):end, :]

        xpad_sc[0:CONV_PAD, :] = xpad_sc[lc:lc + CONV_PAD, :]

    dtr = dt_ref[0] + dtb_ref[...]
    dt = jnp.maximum(dtr, 0.0) + jnp.log1p(jnp.exp(-jnp.abs(dtr)))
    if valid_last < lc:
        row = lax.broadcasted_iota(jnp.int32, dt.shape, 0)
        dt = jnp.where(row < valid_last, dt, 0.0)
    a = dt * (-jnp.exp(alog_ref[...]))
    rr = lax.broadcasted_iota(jnp.int32, (lc, lc), 0)
    cc = lax.broadcasted_iota(jnp.int32, (lc, lc), 1)
    causal = rr >= cc
    tri = jnp.where(causal, 1.0, 0.0).astype(BF16)
    a_cs = sum(_dot(tri, part) for part in _split3(a))
    a_last = a_cs[lc - 1:lc, :]
    a_cs_t = a_cs.T

    expand = expand_ref[...]
    stack = jnp.concatenate([dt, dt * jnp.exp(a_last - a_cs), jnp.exp(a_cs)], axis=0)
    ex = _dot(stack.astype(BF16), expand)
    w_dt = ex[0:lc]
    w_state = ex[lc:2 * lc]
    w_off = ex[2 * lc:3 * lc]
    carry = jnp.broadcast_to(jnp.exp(a_last), (2 * SUBLANES, LANES))
    w_carry = sum(_dot(part, expand) for part in _split3(carry))[0:1]

    xs = act[:, :D_INNER].astype(F32)
    xdt = xs * w_dt
    xst = xs * w_state
    lane = lax.broadcasted_iota(jnp.int32, (lc, LANES), 1)
    lo_half = lane < SSM_HEAD_DIM

    for g in range(SSM_GROUPS):
        gs = slice(g * GROUP_WIDTH, (g + 1) * GROUP_WIDTH)
        bg = act[:, D_INNER + g * SSM_STATE:D_INNER + (g + 1) * SSM_STATE]
        cg = act[:, D_INNER + SSM_GROUPS * SSM_STATE + g * SSM_STATE:
                 D_INNER + SSM_GROUPS * SSM_STATE + (g + 1) * SSM_STATE]
        bgb = bg.astype(BF16)
        cgb = cg.astype(BF16)
        cb = _dot_nt(cgb, bgb)
        st_prev = st_sc[g]
        y_off = _dot(cgb, st_prev.astype(BF16)) * w_off[:, gs]
        st_sc[g] = w_carry[:, gs] * st_prev + _dot(bg.astype(F32).T.astype(BF16), xst[:, gs].astype(BF16))

        pairs = []
        for pr in range(SSM_HEADS_PER_GROUP // 2):
            x_pair = xdt[:, g * GROUP_WIDTH + pr * LANES:g * GROUP_WIDTH + (pr + 1) * LANES]
            y_pair = None
            for half in range(2):
                h = g * SSM_HEADS_PER_GROUP + pr * 2 + half
                seg = a_cs[:, h:h + 1] - a_cs_t[h:h + 1, :]
                decay = jnp.exp(jnp.where(causal, seg, -jnp.inf))
                mix = (cb * decay).astype(BF16)
                keep = lo_half if half == 0 else jnp.logical_not(lo_half)
                contrib = _dot(mix, jnp.where(keep, x_pair, 0.0).astype(BF16))
                y_pair = contrib if y_pair is None else y_pair + contrib
            pairs.append(y_pair)
        y_g = jnp.concatenate(pairs, axis=1) + y_off + dsk_ref[:, gs] * xs[:, gs]

        zg = z_ref[0, :, gs].astype(F32)
        yz = y_g * (zg * jax.nn.sigmoid(zg))
        y_ref[0, :, gs] = _rms(yz, nrm_ref[:, gs], SSM_NORM_EPS).astype(y_ref.dtype)

    @pl.when(c == nchunks - 1)
    def _():
        for g in range(SSM_GROUPS):
            sout_ref[0, g] = st_sc[g].T


def _ssd(xbc, z, dt, state0, p, lc, valid_last, conv_buf=None):
    b, l, _ = xbc.shape
    conv_done = conv_buf is None
    assert l % lc == 0 and CONV_WIDTH - 1 <= valid_last <= lc
    assert valid_last == lc or l == lc
    state_spec = pl.BlockSpec((1, SSM_GROUPS, GROUP_WIDTH, SSM_STATE), lambda bi, c: (bi, 0, 0, 0))
    in_specs = [
        pl.BlockSpec((1, lc, CONV_DIM), lambda bi, c: (bi, c, 0)),
        pl.BlockSpec((1, lc, D_INNER), lambda bi, c: (bi, c, 0)),
        pl.BlockSpec((1, lc, LANES), lambda bi, c: (bi, c, 0)),
        state_spec,
        pl.BlockSpec((1, LANES), lambda bi, c: (0, 0)),
        pl.BlockSpec((1, LANES), lambda bi, c: (0, 0)),
        pl.BlockSpec((1, D_INNER), lambda bi, c: (0, 0)),
        pl.BlockSpec((1, D_INNER), lambda bi, c: (0, 0)),
        pl.BlockSpec((LANES, D_INNER), lambda bi, c: (0, 0)),
    ]
    args = [xbc, z, dt, state0, p["dt_bias"], p["a_log"], p["d_skip"], p["ssm_norm"], p["expand"]]
    out_specs = [pl.BlockSpec((1, lc, D_INNER), lambda bi, c: (bi, c, 0)), state_spec]
    out_shape = [jax.ShapeDtypeStruct((b, l, D_INNER), BF16),
                 jax.ShapeDtypeStruct((b, SSM_GROUPS, GROUP_WIDTH, SSM_STATE), F32)]
    scratch = [pltpu.VMEM((SSM_GROUPS, SSM_STATE, GROUP_WIDTH), F32)]
    if not conv_done:
        conv_spec = pl.BlockSpec((1, CONV_WIDTH - 1, CONV_DIM), lambda bi, c: (bi, 0, 0))
        in_specs += [conv_spec, pl.BlockSpec((CONV_WIDTH, CONV_DIM), lambda bi, c: (0, 0)),
                     pl.BlockSpec((1, CONV_DIM), lambda bi, c: (0, 0))]
        args += [conv_buf, p["conv_w"], p["conv_b"]]
        out_specs.append(conv_spec)
        out_shape.append(jax.ShapeDtypeStruct((b, CONV_WIDTH - 1, CONV_DIM), F32))
        scratch.append(pltpu.VMEM((lc + CONV_PAD, CONV_DIM), F32))
    return pl.pallas_call(
        functools.partial(_ssd_kernel, lc=lc, valid_last=valid_last, conv_done=conv_done),
        grid=(b, l // lc),
        in_specs=in_specs,
        out_specs=out_specs,
        out_shape=out_shape,
        scratch_shapes=scratch,
        compiler_params=_cparams("arbitrary", "arbitrary"),
        name="ssd",
    )(*args)


FFN_SPLIT = 2
FFN_TF = D_FF // FFN_SPLIT


def _mix_ffn_kernel(x_ref, att_ref, y_ref, g_ref, wa_ref, ws_ref, wo_ref, nw_ref, wgu_ref, wd_ref, nf_ref,
                    *out_refs, final_norm):
    ba = _dot(att_ref[...], wa_ref[...])
    bs = _dot(y_ref[...], ws_ref[...])
    g = jax.nn.sigmoid(g_ref[...].astype(F32))
    merged = g[:, :D_MODEL] * ba + g[:, D_MODEL:] * bs
    x = x_ref[...] + _dot(merged.astype(BF16), wo_ref[...])
    hb = _rms(x, nw_ref[...], RMS_EPS).astype(BF16)
    acc = x
    for c in range(FFN_SPLIT):
        gt = _dot(hb, wgu_ref[:, c * FFN_TF:(c + 1) * FFN_TF])
        up = _dot(hb, wgu_ref[:, D_FF + c * FFN_TF:D_FF + (c + 1) * FFN_TF])
        act = (gt * jax.nn.sigmoid(gt) * up).astype(BF16)
        acc = acc + _dot(act, wd_ref[c * FFN_TF:(c + 1) * FFN_TF, :])
    out_refs[0][...] = acc
    if final_norm:
        out_refs[1][...] = _rms(acc, nf_ref[...], RMS_EPS)


def _mix_ffn(x, att, y, gates, p, norm_final, final_norm, tm):
    t = x.shape[0]
    row_spec = pl.BlockSpec((tm, D_MODEL), lambda i: (i, 0))
    vec_spec = pl.BlockSpec((1, D_MODEL), lambda i: (0, 0))
    n_out = 2 if final_norm else 1
    return pl.pallas_call(
        functools.partial(_mix_ffn_kernel, final_norm=final_norm),
        grid=(t // tm,),
        in_specs=[
            row_spec,
            pl.BlockSpec((tm, ATT_WIDTH), lambda i: (i, 0)),
            pl.BlockSpec((tm, D_INNER), lambda i: (i, 0)),
            pl.BlockSpec((tm, 2 * D_MODEL), lambda i: (i, 0)),
            _const_spec((ATT_WIDTH, D_MODEL)),
            _const_spec((D_INNER, D_MODEL)),
            _const_spec((D_MODEL, D_MODEL)),
            vec_spec,
            _const_spec((D_MODEL, 2 * D_FF)),
            _const_spec((D_FF, D_MODEL)),
            vec_spec,
        ],
        out_specs=(row_spec,) * n_out,
        out_shape=(jax.ShapeDtypeStruct((t, D_MODEL), F32),) * n_out,
        compiler_params=_cparams("arbitrary"),
        name="mix_ffn",
    )(x, att, y, gates, p["wa"], p["ws"], p["wo"], p["norm_ffn"], p["wgu"], p["wd"], norm_final)


def _layer_params(l, norm_mix, w_in, lambda_q1, lambda_k1, lambda_q2, lambda_k2, attn_subln, conv_w, conv_b,
                  dt_bias, a_log, d_skip, ssm_norm, w_branch_att, w_branch_ssd, w_out, norm_ffn, w_gate_up,
                  w_down):
    n_main = Q_DIM + Q_DIM + ATT_WIDTH + D_INNER + CONV_DIM
    w_dt = jnp.pad(w_in[l, :, n_main:n_main + SSM_HEADS], ((0, 0), (0, LANES - SSM_HEADS)))
    head_of_channel = jnp.arange(D_INNER) // SSM_HEAD_DIM
    return {
        "norm_mix": norm_mix[l][None], "w_dt": w_dt,
        "lamp": jnp.stack([lambda_q1[l], lambda_k1[l], lambda_q2[l], lambda_k2[l]]),
        "subln": attn_subln[l][None],
        "conv_w": conv_w[l], "conv_b": conv_b[l][None],
        "dt_bias": jnp.pad(dt_bias[l], (0, LANES - SSM_HEADS))[None],
        "a_log": jnp.pad(a_log[l], (0, LANES - SSM_HEADS))[None],
        "d_skip": jnp.repeat(d_skip[l], SSM_HEAD_DIM)[None],
        "ssm_norm": ssm_norm[l][None],
        "expand": (jnp.arange(LANES)[:, None] == head_of_channel[None, :]).astype(BF16),
        "wa": w_branch_att[l].astype(BF16), "ws": w_branch_ssd[l].astype(BF16), "wo": w_out[l].astype(BF16),
        "norm_ffn": norm_ffn[l][None], "wgu": w_gate_up[l].astype(BF16), "wd": w_down[l].astype(BF16),
    }


def _pad_time(x, b, n, lc):
    return jnp.pad(x.reshape(b, n, x.shape[-1]), ((0, 0), (0, lc - n), (0, 0)))


def kernel(x_prompt, x_sample, cache_k, cache_v, state_conv, state_ssm, norm_mix, w_in, lambda_q1, lambda_k1,
           lambda_q2, lambda_k2, attn_subln, conv_w, conv_b, dt_bias, a_log, d_skip, ssm_norm, w_branch_att,
           w_branch_ssd, w_out, norm_ffn, w_gate_up, w_down, norm_final):
    bp, lp, _ = x_prompt.shape
    bs, ls, _ = x_sample.shape
    assert bp == 1
    past = cache_k.shape[2]
    tp, ts = bp * lp, bs * ls
    xp = x_prompt.reshape(tp, D_MODEL)
    xs = x_sample.reshape(ts, D_MODEL)
    cache_k = jnp.transpose(cache_k, (0, 1, 3, 4, 5, 2)).reshape(DEPTH, bs, Q_DIM, past)
    cache_v = cache_v.reshape(DEPTH, bs, past * ATT_HEADS, ATT_V_DIM)
    state_ssm = state_ssm.reshape(DEPTH, bs, SSM_GROUPS, GROUP_WIDTH, SSM_STATE)
    conv0 = jnp.zeros((CONV_PAD, CONV_DIM), F32)
    ssm0 = jnp.zeros((bp, SSM_GROUPS, GROUP_WIDTH, SSM_STATE), F32)
    nf = norm_final[None]

    tm_p = 1024 if tp % 1024 == 0 else 128
    tm_row = 512 if tp % 512 == 0 else 128
    tq = ATTN_TQ if lp % ATTN_TQ == 0 else CHUNK
    tk = tq // 2

    w_in = w_in.astype(BF16)
    w_gates = w_in[:, :, w_in.shape[2] - 2 * D_MODEL:]
    conv_p, ssm_p, conv_s, ssm_s = [], [], [], []
    kv_p = kv_s = None
    yp = ys = None
    for l in range(DEPTH):
        p = _layer_params(l, norm_mix, w_in, lambda_q1, lambda_k1, lambda_q2, lambda_k2, attn_subln, conv_w,
                          conv_b, dt_bias, a_log, d_skip, ssm_norm, w_branch_att, w_branch_ssd, w_out,
                          norm_ffn, w_gate_up, w_down)
        lam_init = 0.8 - 0.6 * math.exp(-0.3 * l)
        last = l == DEPTH - 1

        q, k_all, kb, v_all, vb = _qkv_proj(xp, p["norm_mix"], w_in, tm_p, stacks=kv_p, layer=l)
        kv_p = (k_all, v_all)
        z, act, gates, dt, tail = _in_proj(xp, p["norm_mix"], w_in, w_gates, p["w_dt"], tm_p, l,
                                           conv=(p["conv_w"], p["conv_b"], conv0))
        att = _attn_prompt(q, kb, vb, p["lamp"], p["subln"], lam_init, tq, tk, ATTN_UNROLL)
        y, s_new = _ssd(act.reshape(bp, lp, CONV_DIM), z.reshape(bp, lp, D_INNER), dt.reshape(bp, lp, LANES),
                        ssm0, p, SSD_CHUNK, SSD_CHUNK)
        res = _mix_ffn(xp, att, y.reshape(tp, D_INNER), gates, p, nf, last, tm_row)
        xp = res[0]
        if last:
            yp = res[1]
        conv_p.append(tail[-1, CONV_PAD - (CONV_WIDTH - 1):][None])
        ssm_p.append(s_new.reshape(bp, SSM_GROUPS, SSM_HEADS_PER_GROUP, SSM_HEAD_DIM, SSM_STATE))

        q, k_all, kb, v_all, vb = _qkv_proj(xs, p["norm_mix"], w_in, ts, stacks=kv_s, layer=l)
        kv_s = (k_all, v_all)
        z, xbc, gates, dt = _in_proj(xs, p["norm_mix"], w_in, w_gates, p["w_dt"], ts, l)
        att = _attn_sample(q.reshape(bs, ls, Q_DIM), cache_k, cache_v, kb.reshape(bs, ls, Q_DIM),
                           vb.reshape(bs, ls, ATT_WIDTH), p["lamp"], p["subln"], lam_init, l)
        y, s_new, c_new = _ssd(_pad_time(xbc, bs, ls, SSD_PAD_ROWS), _pad_time(z, bs, ls, SSD_PAD_ROWS),
                               _pad_time(dt, bs, ls, SSD_PAD_ROWS), state_ssm[l], p, SSD_PAD_ROWS, ls,
                               conv_buf=state_conv[l])
        res = _mix_ffn(xs, att.reshape(ts, ATT_WIDTH), y[:, :ls].reshape(ts, D_INNER), gates, p, nf, last, ts)
        xs = res[0]
        if last:
            ys = res[1]
        conv_s.append(c_new)
        ssm_s.append(s_new.reshape(bs, SSM_GROUPS, SSM_HEADS_PER_GROUP, SSM_HEAD_DIM, SSM_STATE))

    return (yp.reshape(bp, lp, D_MODEL), ys.reshape(bs, ls, D_MODEL),
            kv_p[0].reshape(DEPTH, bp, lp, ATT_HEADS, 2, ATT_HEAD_DIM),
            kv_p[1].reshape(DEPTH, bp, lp, ATT_HEADS, ATT_V_DIM),
            jnp.stack(conv_p), jnp.stack(ssm_p),
            kv_s[0].reshape(DEPTH, bs, ls, ATT_HEADS, 2, ATT_HEAD_DIM),
            kv_s[1].reshape(DEPTH, bs, ls, ATT_HEADS, ATT_V_DIM),
            jnp.stack(conv_s), jnp.stack(ssm_s))
```

```python
import functools
import math

import jax
import jax.numpy as jnp
from jax import lax
from jax.experimental import pallas as pl
from jax.experimental.pallas import tpu as pltpu

F32 = jnp.float32
BF16 = jnp.bfloat16
LOG2E = math.log2(math.e)

D_MODEL = 1024
DEPTH = 2
CHUNK = 64
ATT_HEADS = 8
ATT_HEAD_DIM = 64
ATT_V_DIM = 128
ATT_WIDTH = 1024
Q_DIM = 1024
ATT_SUBLN_EPS = 1e-5
D_INNER = 2048
SSM_HEAD_DIM = 64
SSM_HEADS = 32
SSM_GROUPS = 8
SSM_HEADS_PER_GROUP = 4
SSM_STATE = 128
CONV_WIDTH = 4
CONV_DIM = 4096
SSM_NORM_EPS = 1e-5
D_FF = 2816
RMS_EPS = 1e-6
GROUP_WIDTH = SSM_HEADS_PER_GROUP * SSM_HEAD_DIM

LANES = 128
SUBLANES = 8
VMEM_LIMIT_BYTES = 56 * 1024 * 1024

PROJ_TN = 512
SSD_CHUNK = 256
SSD_PAD_ROWS = 128
CONV_PAD = 8
CONV_ROWS = 32
CONV_COLS = 512
SAMPLE_TS = 1024
ATTN_TQ = 1024
ATTN_UNROLL = 2


def _cparams(*sem):
    return pltpu.CompilerParams(dimension_semantics=sem, vmem_limit_bytes=VMEM_LIMIT_BYTES)


def _const_spec(shape):
    return pl.BlockSpec(shape, lambda *_: (0,) * len(shape), pipeline_mode=pl.Buffered(1))


def _rms(x, w, eps):
    return x * lax.rsqrt(jnp.mean(x * x, axis=-1, keepdims=True) + eps) * w


def _split3(x):
    hi = x.astype(BF16)
    r1 = x - hi.astype(F32)
    mid = r1.astype(BF16)
    lo = (r1 - mid.astype(F32)).astype(BF16)
    return hi, mid, lo


def _dot(a, b):
    return jnp.dot(a, b, preferred_element_type=F32)


def _dot_nt(a, b):
    return lax.dot_general(a, b, (((1,), (1,)), ((), ())), preferred_element_type=F32)


_NZ = D_INNER // PROJ_TN
_NX = CONV_DIM // PROJ_TN
_NG = 2 * D_MODEL // PROJ_TN
_OFF_X = _NZ
_OFF_G = _OFF_X + _NX
_N_COL_TILES = _OFF_G + _NG


def _qkv_kernel(*refs):
    x_ref, nw_ref, w_ref = refs[:3]
    q_ref, k_ref, kb_ref, v_ref, vb_ref, h_sc = refs[-6:]
    j = pl.program_id(1)

    @pl.when(j == 0)
    def _():
        h_sc[...] = _rms(x_ref[...], nw_ref[...], RMS_EPS).astype(BF16)
        q_ref[...] = (_dot(h_sc[...], w_ref[0]) * (LOG2E * ATT_HEAD_DIM ** -0.5)).astype(BF16)

    @pl.when(j == 1)
    def _():
        res = _dot(h_sc[...], w_ref[1])
        k_ref[...] = res
        kb_ref[...] = res.astype(BF16)

    @pl.when(j == 2)
    def _():
        res = _dot(h_sc[...], w_ref[2])
        v_ref[...] = res
        vb_ref[...] = res.astype(BF16)


def _qkv_proj(x, norm_w, w_qkv, tm, stacks=None, layer=0):
    t = x.shape[0]
    row = pl.BlockSpec((tm, Q_DIM), lambda i, j: (i, 0))
    stack = pl.BlockSpec((None, tm, Q_DIM), lambda i, j: (layer, i, 0))
    in_specs = [pl.BlockSpec((tm, D_MODEL), lambda i, j: (i, 0)), pl.BlockSpec((1, D_MODEL), lambda i, j: (0, 0)),
                _const_spec((3, D_MODEL, Q_DIM))]
    args = [x, norm_w, w_qkv]
    aliases = {}
    if stacks is not None:
        in_specs += [pl.BlockSpec(memory_space=pl.ANY)] * 2
        aliases = {len(args): 1, len(args) + 1: 3}
        args += list(stacks)
    return pl.pallas_call(
        _qkv_kernel,
        grid=(t // tm, 3),
        in_specs=in_specs,
        out_specs=[row, stack, row, stack, row],
        out_shape=[jax.ShapeDtypeStruct((t, Q_DIM), BF16),
                   jax.ShapeDtypeStruct((DEPTH, t, Q_DIM), F32),
                   jax.ShapeDtypeStruct((t, Q_DIM), BF16),
                   jax.ShapeDtypeStruct((DEPTH, t, ATT_WIDTH), F32),
                   jax.ShapeDtypeStruct((t, ATT_WIDTH), BF16)],
        scratch_shapes=[pltpu.VMEM((tm, D_MODEL), BF16)],
        input_output_aliases=aliases,
        compiler_params=_cparams("arbitrary", "arbitrary"),
        name="qkv_proj",
    )(*args)


def _in_proj_kernel(*refs, tm, fuse_conv):
    x_ref, nw_ref, w_ref, wdt_ref = refs[:4]
    pos = 4
    if fuse_conv:
        cw_ref, cb_ref, hist0_ref = refs[pos:pos + 3]
        pos += 3
    z_ref, xbc_ref, g_ref, dt_ref = refs[pos:pos + 4]
    pos += 4
    if fuse_conv:
        tail_ref = refs[pos]
        pos += 1
    h_sc = refs[pos]
    i = pl.program_id(0)
    j = pl.program_id(1)

    @pl.when(j == 0)
    def _():
        hb = _rms(x_ref[...], nw_ref[...], RMS_EPS).astype(BF16)
        h_sc[...] = hb
        dt_ref[...] = _dot(hb, wdt_ref[...])

    def tile():
        return _dot(h_sc[...], w_ref[j])

    @pl.when(j < _OFF_X)
    def _():
        z_ref[...] = tile().astype(z_ref.dtype)

    if not fuse_conv:
        @pl.when((j >= _OFF_X) & (j < _OFF_G))
        def _():
            xbc_ref[...] = tile()

        @pl.when(j >= _OFF_G)
        def _():
            g_ref[...] = tile().astype(g_ref.dtype)
        return

    hist_sc, raw_sc = refs[pos + 1], refs[pos + 2]

    @pl.when((i == 0) & (j == 0))
    def _():
        for c in range(_NX):
            hist_sc[c] = hist0_ref[:, c * PROJ_TN:(c + 1) * PROJ_TN]

    def activate(jc, slot):
        for c0 in range(0, PROJ_TN, CONV_COLS):
            cs = slice(c0, c0 + CONV_COLS)
            for r0 in range(0, tm, CONV_ROWS):
                before = hist_sc[jc, :, cs] if r0 == 0 else raw_sc[slot, r0 - CONV_PAD:r0, cs]
                xp = jnp.concatenate([before, raw_sc[slot, r0:r0 + CONV_ROWS, cs]], axis=0)
                conv = cw_ref[0:1, cs] * xp
                for tap in range(1, CONV_WIDTH):
                    conv = cw_ref[tap:tap + 1, cs] * xp + pltpu.roll(conv, 1, 0)
                conv = conv[CONV_PAD:] + cb_ref[:, cs]
                xbc_ref[r0:r0 + CONV_ROWS, cs] = (conv * jax.nn.sigmoid(conv)).astype(xbc_ref.dtype)
        last_rows = raw_sc[slot, tm - CONV_PAD:tm, :]
        hist_sc[jc] = last_rows
        tail_ref[...] = last_rows

    @pl.when(j == _OFF_X)
    def _():
        raw_sc[0] = tile()

    for slot in range(2):
        @pl.when((j > _OFF_X) & (j < _OFF_G) & (lax.rem(j - _OFF_X, 2) == slot))
        def _():
            activate(j - _OFF_X - 1, 1 - slot)
            raw_sc[slot] = tile()

    @pl.when(j == _OFF_G)
    def _():
        activate(_NX - 1, (_NX - 1) % 2)
        g_ref[...] = tile().astype(g_ref.dtype)

    @pl.when(j > _OFF_G)
    def _():
        g_ref[...] = tile().astype(g_ref.dtype)


def _in_proj(x, norm_w, w_tiles, w_dt, tm, conv=None):
    t = x.shape[0]
    fuse_conv = conv is not None

    def out_spec(off, n):
        return pl.BlockSpec((tm, PROJ_TN), lambda i, j: (i, jnp.clip(j - off, 0, n - 1)))

    xbc_lag = 1 if fuse_conv else 0

    def xcol(i, j):
        return (0, jnp.clip(j - _OFF_X - xbc_lag, 0, _NX - 1))

    in_specs = [
        pl.BlockSpec((tm, D_MODEL), lambda i, j: (i, 0)),
        pl.BlockSpec((1, D_MODEL), lambda i, j: (0, 0)),
        _const_spec((_N_COL_TILES, D_MODEL, PROJ_TN)),
        pl.BlockSpec((D_MODEL, LANES), lambda i, j: (0, 0)),
    ]
    args = [x, norm_w, w_tiles, w_dt]
    if fuse_conv:
        in_specs += [pl.BlockSpec((CONV_WIDTH, PROJ_TN), xcol), pl.BlockSpec((1, PROJ_TN), xcol),
                     pl.BlockSpec((CONV_PAD, CONV_DIM), lambda i, j: (0, 0))]
        args += list(conv)
    out_shape = [
        jax.ShapeDtypeStruct((t, D_INNER), BF16),
        jax.ShapeDtypeStruct((t, CONV_DIM), BF16 if fuse_conv else F32),
        jax.ShapeDtypeStruct((t, 2 * D_MODEL), BF16),
        jax.ShapeDtypeStruct((t, LANES), F32),
    ]
    out_specs = [
        out_spec(0, _NZ), out_spec(_OFF_X + xbc_lag, _NX), out_spec(_OFF_G, _NG),
        pl.BlockSpec((tm, LANES), lambda i, j: (i, 0)),
    ]
    scratch = [pltpu.VMEM((tm, D_MODEL), BF16)]
    if fuse_conv:
        out_shape.append(jax.ShapeDtypeStruct((t // tm, CONV_PAD, CONV_DIM), F32))
        out_specs.append(pl.BlockSpec((None, CONV_PAD, PROJ_TN), lambda i, j: (i,) + xcol(i, j)))
        scratch += [pltpu.VMEM((_NX, CONV_PAD, PROJ_TN), F32), pltpu.VMEM((2, tm, PROJ_TN), F32)]
    return pl.pallas_call(
        functools.partial(_in_proj_kernel, tm=tm, fuse_conv=fuse_conv),
        grid=(t // tm, _N_COL_TILES),
        in_specs=in_specs,
        out_specs=out_specs,
        out_shape=out_shape,
        scratch_shapes=scratch,
        compiler_params=_cparams("arbitrary", "arbitrary"),
        name="in_proj",
    )(*args)


def _stack_q(q):
    lane = lax.broadcasted_iota(jnp.int32, q.shape, 1)
    zero = jnp.zeros_like(q)
    return jnp.concatenate([jnp.where(lane < ATT_HEAD_DIM, q, zero),
                            jnp.where(lane >= ATT_HEAD_DIM, q, zero)], axis=0)


def _diff_epilogue(o, n, lamp_ref, subln_ref, lam_init):
    lp = lamp_ref[...]
    lam = (jnp.exp(jnp.sum(lp[0:1] * lp[1:2], axis=-1, keepdims=True))
           - jnp.exp(jnp.sum(lp[2:3] * lp[3:4], axis=-1, keepdims=True)) + lam_init)
    a = o[:n] - lam * o[n:]
    return _rms(a, subln_ref[...], ATT_SUBLN_EPS) * (1.0 - lam_init)


def _attn_prompt_kernel(lamp_ref, subln_ref, q_ref, k_ref, v_ref, o_ref, m_sc, l_sc, acc_sc, s_sc, *,
                        tq, tk, nq, unroll, lam_init):
    qi = pl.program_id(1)
    half = tq // 2
    qs = jnp.concatenate([_stack_q(q_ref[0:half, :]), _stack_q(q_ref[half:tq, :])], axis=0)
    m_sc[...] = jnp.full(m_sc.shape, -jnp.inf, F32)
    l_sc[...] = jnp.zeros(l_sc.shape, F32)
    acc_sc[...] = jnp.zeros(acc_sc.shape, F32)
    n_lane_tiles = tk // LANES
    full_blocks = qi * (tq // tk)
    PLAIN, MASKED, MASKED_LATE = "plain", "masked", "masked_late"

    def scores(kblk, kind):
        start = pl.multiple_of(kblk * tk, tk)
        rows = qs[tq:] if kind == MASKED_LATE else qs
        s = _dot_nt(rows, k_ref[pl.ds(start, tk), :])
        if kind != PLAIN:
            r = lax.broadcasted_iota(jnp.int32, s.shape, 0)
            c = lax.broadcasted_iota(jnp.int32, s.shape, 1)
            if kind == MASKED_LATE:
                q_row = half + jnp.where(r >= half, r - half, r)
            else:
                q_row = jnp.where(r >= tq, half, 0) + lax.rem(r, half)
            k_row = c + (kblk - full_blocks) * tk
            s = jnp.where(k_row // CHUNK <= q_row // CHUNK, s, -jnp.inf)
        return s

    def absorb(kblk, s, state):
        if s.shape[0] != 2 * tq:
            late = absorb_rows(kblk, s, tuple(a[tq:] for a in state))
            return tuple(jnp.concatenate([a[:tq], b], axis=0) for a, b in zip(state, late))
        return absorb_rows(kblk, s, state)

    def absorb_rows(kblk, s, state):
        m_prev, l_prev, acc_prev = state
        vb = v_ref[pl.ds(pl.multiple_of(kblk * tk, tk), tk), :]
        m_new = jnp.maximum(m_prev, jnp.max(s, axis=-1, keepdims=True))
        alpha = jnp.exp2(m_prev - m_new)
        p = jnp.exp2(s - jnp.concatenate([m_new] * n_lane_tiles, axis=1))
        pv = _dot(p.astype(BF16), jnp.concatenate([vb, jnp.ones((tk, LANES), BF16)], axis=1))
        return m_new, alpha * l_prev + pv[:, ATT_V_DIM:], alpha * acc_prev + pv[:, :ATT_V_DIM]

    def pipeline(first, produce):
        state = (m_sc[...], l_sc[...], acc_sc[...])
        s = s_sc[...]
        for i, kind in enumerate(produce):
            s_next = None if kind is None else scores(first + i + 1, kind)
            state = absorb(first + i, s, state)
            s = s_next
        if s is not None:
            s_sc[...] = s
        m_sc[...], l_sc[...], acc_sc[...] = state

    diag = [MASKED, MASKED_LATE]

    @pl.when(qi == 0)
    def _():
        s_sc[...] = scores(0, MASKED)
        pipeline(0, diag[1:] + [None])

    @pl.when(qi > 0)
    def _():
        s_sc[...] = scores(0, PLAIN)

    def body(t, carry):
        pipeline(unroll * t, [PLAIN] * unroll)
        return carry

    plain_steps = jnp.maximum(full_blocks - 1, 0)
    lax.fori_loop(0, plain_steps // unroll, body, 0)
    for r in sorted({(len(diag) * q - 1) % unroll for q in range(1, nq)}):
        @pl.when((qi > 0) & (plain_steps % unroll == r))
        def _():
            pipeline(full_blocks - 1 - r, [PLAIN] * r + diag + [None])

    o = acc_sc[...] * (1.0 / l_sc[...])
    for g in range(2):
        o_ref[g * half:(g + 1) * half, :] = _diff_epilogue(
            o[g * tq:(g + 1) * tq], half, lamp_ref, subln_ref, lam_init).astype(o_ref.dtype)


def _attn_prompt(q, kb, vb, lamp, subln, lam_init, tq, tk, unroll):
    t = q.shape[0]
    assert t % tq == 0 and tq == 2 * tk and tk % LANES == 0 and tk % CHUNK == 0
    kern = functools.partial(_attn_prompt_kernel, tq=tq, tk=tk, nq=t // tq, unroll=unroll, lam_init=lam_init)
    return pl.pallas_call(
        kern,
        grid=(ATT_HEADS, t // tq),
        in_specs=[
            pl.BlockSpec((4, ATT_HEAD_DIM), lambda h, i: (0, 0)),
            pl.BlockSpec((1, ATT_V_DIM), lambda h, i: (0, 0)),
            pl.BlockSpec((tq, LANES), lambda h, i: (i, h)),
            pl.BlockSpec((t, LANES), lambda h, i: (0, h)),
            pl.BlockSpec((t, LANES), lambda h, i: (0, h)),
        ],
        out_specs=pl.BlockSpec((tq, LANES), lambda h, i: (i, h)),
        out_shape=jax.ShapeDtypeStruct((t, ATT_WIDTH), BF16),
        scratch_shapes=[pltpu.VMEM((2 * tq, LANES), F32), pltpu.VMEM((2 * tq, LANES), F32),
                        pltpu.VMEM((2 * tq, ATT_V_DIM), F32), pltpu.VMEM((2 * tq, tk), F32)],
        compiler_params=_cparams("arbitrary", "arbitrary"),
        name="attn_prompt",
    )(lamp, subln, q, kb, vb)


def _attn_sample_kernel(lamp_ref, subln_ref, q_ref, kt_ref, vc_ref, kn_ref, vn_ref, o_ref, m_sc, l_sc, acc_sc, *,
                        n, ts, lam_init):
    piece = pl.program_id(1)

    @pl.when(piece == 0)
    def _():
        m_sc[...] = jnp.full(m_sc.shape, -jnp.inf, F32)
        l_sc[...] = jnp.zeros(l_sc.shape, F32)
        acc_sc[...] = jnp.zeros(acc_sc.shape, F32)

    rows = 2 * n

    def head_cols(h):
        return slice(h * LANES, (h + 1) * LANES)

    def update(scores_of, values_of):
        s = jnp.concatenate([scores_of(_stack_q(q_ref[0, :, head_cols(h)]), h) for h in range(ATT_HEADS)], axis=0)
        width = s.shape[1]
        m_prev = m_sc[...]
        m_new = jnp.maximum(m_prev, jnp.max(s, axis=-1, keepdims=True))
        alpha = jnp.exp2(m_prev - m_new)
        m_wide = m_new[:, :width] if width <= LANES else jnp.concatenate([m_new] * (width // LANES), axis=1)
        p = jnp.exp2(s - m_wide)
        l_sc[...] = alpha * l_sc[...] + jnp.sum(p, axis=-1, keepdims=True)
        pb = p.astype(BF16)
        pv = jnp.concatenate([_dot(pb[h * rows:(h + 1) * rows], values_of(h)) for h in range(ATT_HEADS)], axis=0)
        acc_sc[...] = alpha * acc_sc[...] + pv
        m_sc[...] = m_new

    update(lambda qs, h: _dot(qs, kt_ref[0, 0, head_cols(h), :].astype(BF16)),
           lambda h: vc_ref[0, 0, pl.ds(h, ts, stride=ATT_HEADS), :].astype(BF16))

    @pl.when(piece == pl.num_programs(1) - 1)
    def _():
        update(lambda qs, h: _dot_nt(qs, kn_ref[0, :, head_cols(h)]), lambda h: vn_ref[0, :, head_cols(h)])
        o = acc_sc[...] * (1.0 / l_sc[...])
        for h in range(ATT_HEADS):
            o_ref[0, :, head_cols(h)] = _diff_epilogue(
                o[h * rows:(h + 1) * rows], n, lamp_ref, subln_ref, lam_init).astype(o_ref.dtype)


def _attn_sample(q, cache_kt, cache_v, kb, vb, lamp, subln, lam_init, layer):
    b, n, _ = q.shape
    past = cache_kt.shape[3]
    ts = min(past, SAMPLE_TS)
    assert past % CHUNK == 0 and n <= CHUNK and past % ts == 0
    kern = functools.partial(_attn_sample_kernel, n=n, ts=ts, lam_init=lam_init)
    row_spec = pl.BlockSpec((1, n, Q_DIM), lambda bi, s: (bi, 0, 0))
    return pl.pallas_call(
        kern,
        grid=(b, past // ts),
        in_specs=[
            pl.BlockSpec((4, ATT_HEAD_DIM), lambda bi, s: (0, 0)),
            pl.BlockSpec((1, ATT_V_DIM), lambda bi, s: (0, 0)),
            row_spec,
            pl.BlockSpec((1, 1, Q_DIM, ts), lambda bi, s: (layer, bi, 0, s)),
            pl.BlockSpec((1, 1, ts * ATT_HEADS, ATT_V_DIM), lambda bi, s: (layer, bi, s, 0)),
            row_spec,
            row_spec,
        ],
        out_specs=row_spec,
        out_shape=jax.ShapeDtypeStruct((b, n, ATT_WIDTH), BF16),
        scratch_shapes=[pltpu.VMEM((ATT_HEADS * 2 * n, LANES), F32)] * 3,
        compiler_params=_cparams("arbitrary", "arbitrary"),
        name="attn_sample",
    )(lamp, subln, q, cache_kt, cache_v, kb, vb)


def _ssd_kernel(*refs, lc, valid_last, conv_done):
    if conv_done:
        (xbc_ref, z_ref, dt_ref, st0_ref, dtb_ref, alog_ref, dsk_ref, nrm_ref, expand_ref,
         y_ref, sout_ref, st_sc) = refs
    else:
        (xbc_ref, z_ref, dt_ref, st0_ref, dtb_ref, alog_ref, dsk_ref, nrm_ref, expand_ref,
         cbuf_ref, cw_ref, cb_ref, y_ref, sout_ref, cout_ref, st_sc, xpad_sc) = refs
    c = pl.program_id(1)
    nchunks = pl.num_programs(1)

    @pl.when(c == 0)
    def _():
        for g in range(SSM_GROUPS):
            st_sc[g] = st0_ref[0, g].T

    if conv_done:
        act = xbc_ref[0]
    else:
        @pl.when(c == 0)
        def _():
            xpad_sc[0:CONV_PAD, :] = jnp.zeros((CONV_PAD, CONV_DIM), F32)
            xpad_sc[CONV_PAD - (CONV_WIDTH - 1):CONV_PAD, :] = cbuf_ref[0]

        xpad_sc[CONV_PAD:CONV_PAD + lc, :] = xbc_ref[0]
        xp = xpad_sc[...]
        conv = cw_ref[0:1, :] * xp
        for j in range(1, CONV_WIDTH):
            conv = cw_ref[j:j + 1, :] * xp + pltpu.roll(conv, 1, 0)
        conv = conv[CONV_PAD:CONV_PAD + lc] + cb_ref[...]
        act = conv * jax.nn.sigmoid(conv)

        @pl.when(c == nchunks - 1)
        def _():
            end = CONV_PAD + valid_last
            cout_ref[0] = xpad_sc[end - (CONV_WIDTH - 1):end, :]

        xpad_sc[0:CONV_PAD, :] = xpad_sc[lc:lc + CONV_PAD, :]

    dtr = dt_ref[0] + dtb_ref[...]
    dt = jnp.maximum(dtr, 0.0) + jnp.log1p(jnp.exp(-jnp.abs(dtr)))
    if valid_last < lc:
        row = lax.broadcasted_iota(jnp.int32, dt.shape, 0)
        dt = jnp.where(row < valid_last, dt, 0.0)
    a = dt * (-jnp.exp(alog_ref[...]))
    rr = lax.broadcasted_iota(jnp.int32, (lc, lc), 0)
    cc = lax.broadcasted_iota(jnp.int32, (lc, lc), 1)
    causal = rr >= cc
    tri = jnp.where(causal, 1.0, 0.0).astype(BF16)
    a_cs = sum(_dot(tri, part) for part in _split3(a))
    a_last = a_cs[lc - 1:lc, :]
    a_cs_t = a_cs.T

    expand = expand_ref[...]
    stack = jnp.concatenate([dt, dt * jnp.exp(a_last - a_cs), jnp.exp(a_cs)], axis=0)
    ex = _dot(stack.astype(BF16), expand)
    w_dt = ex[0:lc]
    w_state = ex[lc:2 * lc]
    w_off = ex[2 * lc:3 * lc]
    carry = jnp.broadcast_to(jnp.exp(a_last), (2 * SUBLANES, LANES))
    w_carry = sum(_dot(part, expand) for part in _split3(carry))[0:1]

    xs = act[:, :D_INNER].astype(F32)
    xdt = xs * w_dt
    xst = xs * w_state
    lane = lax.broadcasted_iota(jnp.int32, (lc, LANES), 1)
    lo_half = lane < SSM_HEAD_DIM

    for g in range(SSM_GROUPS):
        gs = slice(g * GROUP_WIDTH, (g + 1) * GROUP_WIDTH)
        bg = act[:, D_INNER + g * SSM_STATE:D_INNER + (g + 1) * SSM_STATE]
        cg = act[:, D_INNER + SSM_GROUPS * SSM_STATE + g * SSM_STATE:
                 D_INNER + SSM_GROUPS * SSM_STATE + (g + 1) * SSM_STATE]
        bgb = bg.astype(BF16)
        cgb = cg.astype(BF16)
        cb = _dot_nt(cgb, bgb)
        st_prev = st_sc[g]
        y_off = _dot(cgb, st_prev.astype(BF16)) * w_off[:, gs]
        st_sc[g] = w_carry[:, gs] * st_prev + _dot(bg.astype(F32).T.astype(BF16), xst[:, gs].astype(BF16))

        pairs = []
        for pr in range(SSM_HEADS_PER_GROUP // 2):
            x_pair = xdt[:, g * GROUP_WIDTH + pr * LANES:g * GROUP_WIDTH + (pr + 1) * LANES]
            y_pair = None
            for half in range(2):
                h = g * SSM_HEADS_PER_GROUP + pr * 2 + half
                seg = a_cs[:, h:h + 1] - a_cs_t[h:h + 1, :]
                decay = jnp.exp(jnp.where(causal, seg, -jnp.inf))
                mix = (cb * decay).astype(BF16)
                keep = lo_half if half == 0 else jnp.logical_not(lo_half)
                contrib = _dot(mix, jnp.where(keep, x_pair, 0.0).astype(BF16))
                y_pair = contrib if y_pair is None else y_pair + contrib
            pairs.append(y_pair)
        y_g = jnp.concatenate(pairs, axis=1) + y_off + dsk_ref[:, gs] * xs[:, gs]

        zg = z_ref[0, :, gs].astype(F32)
        yz = y_g * (zg * jax.nn.sigmoid(zg))
        y_ref[0, :, gs] = _rms(yz, nrm_ref[:, gs], SSM_NORM_EPS).astype(y_ref.dtype)

    @pl.when(c == nchunks - 1)
    def _():
        for g in range(SSM_GROUPS):
            sout_ref[0, g] = st_sc[g].T


def _ssd(xbc, z, dt, state0, p, lc, valid_last, conv_buf=None):
    b, l, _ = xbc.shape
    conv_done = conv_buf is None
    assert l % lc == 0 and CONV_WIDTH - 1 <= valid_last <= lc
    assert valid_last == lc or l == lc
    state_spec = pl.BlockSpec((1, SSM_GROUPS, GROUP_WIDTH, SSM_STATE), lambda bi, c: (bi, 0, 0, 0))
    in_specs = [
        pl.BlockSpec((1, lc, CONV_DIM), lambda bi, c: (bi, c, 0)),
        pl.BlockSpec((1, lc, D_INNER), lambda bi, c: (bi, c, 0)),
        pl.BlockSpec((1, lc, LANES), lambda bi, c: (bi, c, 0)),
        state_spec,
        pl.BlockSpec((1, LANES), lambda bi, c: (0, 0)),
        pl.BlockSpec((1, LANES), lambda bi, c: (0, 0)),
        pl.BlockSpec((1, D_INNER), lambda bi, c: (0, 0)),
        pl.BlockSpec((1, D_INNER), lambda bi, c: (0, 0)),
        pl.BlockSpec((LANES, D_INNER), lambda bi, c: (0, 0)),
    ]
    args = [xbc, z, dt, state0, p["dt_bias"], p["a_log"], p["d_skip"], p["ssm_norm"], p["expand"]]
    out_specs = [pl.BlockSpec((1, lc, D_INNER), lambda bi, c: (bi, c, 0)), state_spec]
    out_shape = [jax.ShapeDtypeStruct((b, l, D_INNER), BF16),
                 jax.ShapeDtypeStruct((b, SSM_GROUPS, GROUP_WIDTH, SSM_STATE), F32)]
    scratch = [pltpu.VMEM((SSM_GROUPS, SSM_STATE, GROUP_WIDTH), F32)]
    if not conv_done:
        conv_spec = pl.BlockSpec((1, CONV_WIDTH - 1, CONV_DIM), lambda bi, c: (bi, 0, 0))
        in_specs += [conv_spec, pl.BlockSpec((CONV_WIDTH, CONV_DIM), lambda bi, c: (0, 0)),
                     pl.BlockSpec((1, CONV_DIM), lambda bi, c: (0, 0))]
        args += [conv_buf, p["conv_w"], p["conv_b"]]
        out_specs.append(conv_spec)
        out_shape.append(jax.ShapeDtypeStruct((b, CONV_WIDTH - 1, CONV_DIM), F32))
        scratch.append(pltpu.VMEM((lc + CONV_PAD, CONV_DIM), F32))
    return pl.pallas_call(
        functools.partial(_ssd_kernel, lc=lc, valid_last=valid_last, conv_done=conv_done),
        grid=(b, l // lc),
        in_specs=in_specs,
        out_specs=out_specs,
        out_shape=out_shape,
        scratch_shapes=scratch,
        compiler_params=_cparams("arbitrary", "arbitrary"),
        name="ssd",
    )(*args)


FFN_SPLIT = 2
FFN_TF = D_FF // FFN_SPLIT


def _mix_ffn_kernel(x_ref, att_ref, y_ref, g_ref, wa_ref, ws_ref, wo_ref, nw_ref, wgu_ref, wd_ref, nf_ref,
                    *out_refs, final_norm):
    ba = _dot(att_ref[...], wa_ref[...])
    bs = _dot(y_ref[...], ws_ref[...])
    g = jax.nn.sigmoid(g_ref[...].astype(F32))
    merged = g[:, :D_MODEL] * ba + g[:, D_MODEL:] * bs
    x = x_ref[...] + _dot(merged.astype(BF16), wo_ref[...])
    hb = _rms(x, nw_ref[...], RMS_EPS).astype(BF16)
    acc = x
    for c in range(FFN_SPLIT):
        gt = _dot(hb, wgu_ref[:, c * FFN_TF:(c + 1) * FFN_TF])
        up = _dot(hb, wgu_ref[:, D_FF + c * FFN_TF:D_FF + (c + 1) * FFN_TF])
        act = (gt * jax.nn.sigmoid(gt) * up).astype(BF16)
        acc = acc + _dot(act, wd_ref[c * FFN_TF:(c + 1) * FFN_TF, :])
    out_refs[0][...] = acc
    if final_norm:
        out_refs[1][...] = _rms(acc, nf_ref[...], RMS_EPS)


def _mix_ffn(x, att, y, gates, p, norm_final, final_norm, tm):
    t = x.shape[0]
    row_spec = pl.BlockSpec((tm, D_MODEL), lambda i: (i, 0))
    vec_spec = pl.BlockSpec((1, D_MODEL), lambda i: (0, 0))
    n_out = 2 if final_norm else 1
    return pl.pallas_call(
        functools.partial(_mix_ffn_kernel, final_norm=final_norm),
        grid=(t // tm,),
        in_specs=[
            row_spec,
            pl.BlockSpec((tm, ATT_WIDTH), lambda i: (i, 0)),
            pl.BlockSpec((tm, D_INNER), lambda i: (i, 0)),
            pl.BlockSpec((tm, 2 * D_MODEL), lambda i: (i, 0)),
            _const_spec((ATT_WIDTH, D_MODEL)),
            _const_spec((D_INNER, D_MODEL)),
            _const_spec((D_MODEL, D_MODEL)),
            vec_spec,
            _const_spec((D_MODEL, 2 * D_FF)),
            _const_spec((D_FF, D_MODEL)),
            vec_spec,
        ],
        out_specs=(row_spec,) * n_out,
        out_shape=(jax.ShapeDtypeStruct((t, D_MODEL), F32),) * n_out,
        compiler_params=_cparams("arbitrary"),
        name="mix_ffn",
    )(x, att, y, gates, p["wa"], p["ws"], p["wo"], p["norm_ffn"], p["wgu"], p["wd"], norm_final)


def _layer_params(l, norm_mix, w_in, lambda_q1, lambda_k1, lambda_q2, lambda_k2, attn_subln, conv_w, conv_b,
                  dt_bias, a_log, d_skip, ssm_norm, w_branch_att, w_branch_ssd, w_out, norm_ffn, w_gate_up,
                  w_down):
    w = w_in[l]
    n_qkv = Q_DIM + Q_DIM + ATT_WIDTH
    n_main = n_qkv + D_INNER + CONV_DIM
    w_dt = jnp.pad(w[:, n_main:n_main + SSM_HEADS], ((0, 0), (0, LANES - SSM_HEADS)))
    w_qkv = w[:, :n_qkv].reshape(D_MODEL, 3, Q_DIM).transpose(1, 0, 2)
    w_tiles = jnp.concatenate([w[:, n_qkv:n_main], w[:, n_main + SSM_HEADS:]], axis=1)
    w_tiles = w_tiles.reshape(D_MODEL, _N_COL_TILES, PROJ_TN).transpose(1, 0, 2)
    head_of_channel = jnp.arange(D_INNER) // SSM_HEAD_DIM
    return {
        "norm_mix": norm_mix[l][None], "w_qkv": w_qkv, "w_tiles": w_tiles, "w_dt": w_dt,
        "lamp": jnp.stack([lambda_q1[l], lambda_k1[l], lambda_q2[l], lambda_k2[l]]),
        "subln": attn_subln[l][None],
        "conv_w": conv_w[l], "conv_b": conv_b[l][None],
        "dt_bias": jnp.pad(dt_bias[l], (0, LANES - SSM_HEADS))[None],
        "a_log": jnp.pad(a_log[l], (0, LANES - SSM_HEADS))[None],
        "d_skip": jnp.repeat(d_skip[l], SSM_HEAD_DIM)[None],
        "ssm_norm": ssm_norm[l][None],
        "expand": (jnp.arange(LANES)[:, None] == head_of_channel[None, :]).astype(BF16),
        "wa": w_branch_att[l].astype(BF16), "ws": w_branch_ssd[l].astype(BF16), "wo": w_out[l].astype(BF16),
        "norm_ffn": norm_ffn[l][None], "wgu": w_gate_up[l].astype(BF16), "wd": w_down[l].astype(BF16),
    }


def _pad_time(x, b, n, lc):
    return jnp.pad(x.reshape(b, n, x.shape[-1]), ((0, 0), (0, lc - n), (0, 0)))


def kernel(x_prompt, x_sample, cache_k, cache_v, state_conv, state_ssm, norm_mix, w_in, lambda_q1, lambda_k1,
           lambda_q2, lambda_k2, attn_subln, conv_w, conv_b, dt_bias, a_log, d_skip, ssm_norm, w_branch_att,
           w_branch_ssd, w_out, norm_ffn, w_gate_up, w_down, norm_final):
    bp, lp, _ = x_prompt.shape
    bs, ls, _ = x_sample.shape
    assert bp == 1
    past = cache_k.shape[2]
    tp, ts = bp * lp, bs * ls
    xp = x_prompt.reshape(tp, D_MODEL)
    xs = x_sample.reshape(ts, D_MODEL)
    cache_k = jnp.transpose(cache_k, (0, 1, 3, 4, 5, 2)).reshape(DEPTH, bs, Q_DIM, past)
    cache_v = cache_v.reshape(DEPTH, bs, past * ATT_HEADS, ATT_V_DIM)
    state_ssm = state_ssm.reshape(DEPTH, bs, SSM_GROUPS, GROUP_WIDTH, SSM_STATE)
    conv0 = jnp.zeros((CONV_PAD, CONV_DIM), F32)
    ssm0 = jnp.zeros((bp, SSM_GROUPS, GROUP_WIDTH, SSM_STATE), F32)
    nf = norm_final[None]

    tm_p = 1024 if tp % 1024 == 0 else 128
    tm_row = 512 if tp % 512 == 0 else 128
    tq = ATTN_TQ if lp % ATTN_TQ == 0 else CHUNK
    tk = tq // 2

    w_in = w_in.astype(BF16)
    conv_p, ssm_p, conv_s, ssm_s = [], [], [], []
    kv_p = kv_s = None
    yp = ys = None
    for l in range(DEPTH):
        p = _layer_params(l, norm_mix, w_in, lambda_q1, lambda_k1, lambda_q2, lambda_k2, attn_subln, conv_w,
                          conv_b, dt_bias, a_log, d_skip, ssm_norm, w_branch_att, w_branch_ssd, w_out,
                          norm_ffn, w_gate_up, w_down)
        lam_init = 0.8 - 0.6 * math.exp(-0.3 * l)
        last = l == DEPTH - 1

        q, k_all, kb, v_all, vb = _qkv_proj(xp, p["norm_mix"], p["w_qkv"], tm_p, stacks=kv_p, layer=l)
        kv_p = (k_all, v_all)
        z, act, gates, dt, tail = _in_proj(xp, p["norm_mix"], p["w_tiles"], p["w_dt"], tm_p,
                                           conv=(p["conv_w"], p["conv_b"], conv0))
        att = _attn_prompt(q, kb, vb, p["lamp"], p["subln"], lam_init, tq, tk, ATTN_UNROLL)
        y, s_new = _ssd(act.reshape(bp, lp, CONV_DIM), z.reshape(bp, lp, D_INNER), dt.reshape(bp, lp, LANES),
                        ssm0, p, SSD_CHUNK, SSD_CHUNK)
        res = _mix_ffn(xp, att, y.reshape(tp, D_INNER), gates, p, nf, last, tm_row)
        xp = res[0]
        if last:
            yp = res[1]
        conv_p.append(tail[-1, CONV_PAD - (CONV_WIDTH - 1):][None])
        ssm_p.append(s_new.reshape(bp, SSM_GROUPS, SSM_HEADS_PER_GROUP, SSM_HEAD_DIM, SSM_STATE))

        q, k_all, kb, v_all, vb = _qkv_proj(xs, p["norm_mix"], p["w_qkv"], ts, stacks=kv_s, layer=l)
        kv_s = (k_all, v_all)
        z, xbc, gates, dt = _in_proj(xs, p["norm_mix"], p["w_tiles"], p["w_dt"], ts)
        att = _attn_sample(q.reshape(bs, ls, Q_DIM), cache_k, cache_v, kb.reshape(bs, ls, Q_DIM),
                           vb.reshape(bs, ls, ATT_WIDTH), p["lamp"], p["subln"], lam_init, l)
        y, s_new, c_new = _ssd(_pad_time(xbc, bs, ls, SSD_PAD_ROWS), _pad_time(z, bs, ls, SSD_PAD_ROWS),
                               _pad_time(dt, bs, ls, SSD_PAD_ROWS), state_ssm[l], p, SSD_PAD_ROWS, ls,
                               conv_buf=state_conv[l])
        res = _mix_ffn(xs, att.reshape(ts, ATT_WIDTH), y[:, :ls].reshape(ts, D_INNER), gates, p, nf, last, ts)
        xs = res[0]
        if last:
            ys = res[1]
        conv_s.append(c_new)
        ssm_s.append(s_new.reshape(bs, SSM_GROUPS, SSM_HEADS_PER_GROUP, SSM_HEAD_DIM, SSM_STATE))

    return (yp.reshape(bp, lp, D_MODEL), ys.reshape(bs, ls, D_MODEL),
            kv_p[0].reshape(DEPTH, bp, lp, ATT_HEADS, 2, ATT_HEAD_DIM),
            kv_p[1].reshape(DEPTH, bp, lp, ATT_HEADS, ATT_V_DIM),
            jnp.stack(conv_p), jnp.stack(ssm_p),
            kv_s[0].reshape(DEPTH, bs, ls, ATT_HEADS, 2, ATT_HEAD_DIM),
            kv_s[1].reshape(DEPTH, bs, ls, ATT_HEADS, ATT_V_DIM),
            jnp.stack(conv_s), jnp.stack(ssm_s))
```

```python
import functools
import math

import jax
import jax.numpy as jnp
from jax import lax
from jax.experimental import pallas as pl
from jax.experimental.pallas import tpu as pltpu

F32 = jnp.float32
BF16 = jnp.bfloat16
LOG2E = math.log2(math.e)

D_MODEL = 1024
DEPTH = 2
CHUNK = 64
ATT_HEADS = 8
ATT_HEAD_DIM = 64
ATT_V_DIM = 128
ATT_WIDTH = 1024
Q_DIM = 1024
ATT_SUBLN_EPS = 1e-5
D_INNER = 2048
SSM_HEAD_DIM = 64
SSM_HEADS = 32
SSM_GROUPS = 8
SSM_HEADS_PER_GROUP = 4
SSM_STATE = 128
CONV_WIDTH = 4
CONV_DIM = 4096
SSM_NORM_EPS = 1e-5
D_FF = 2816
RMS_EPS = 1e-6
GROUP_WIDTH = SSM_HEADS_PER_GROUP * SSM_HEAD_DIM

LANES = 128
SUBLANES = 8
VMEM_LIMIT_BYTES = 56 * 1024 * 1024

PROJ_TN = 512
SSD_CHUNK = 256
SSD_PAD_ROWS = 128
CONV_PAD = 8
CONV_ROWS = 32
CONV_COLS = 512
SAMPLE_TS = 1024
ATTN_TQ = 1024
ATTN_UNROLL = 2


def _cparams(*sem):
    return pltpu.CompilerParams(dimension_semantics=sem, vmem_limit_bytes=VMEM_LIMIT_BYTES)


def _const_spec(shape):
    return pl.BlockSpec(shape, lambda *_: (0,) * len(shape), pipeline_mode=pl.Buffered(1))


def _rms(x, w, eps):
    return x * lax.rsqrt(jnp.mean(x * x, axis=-1, keepdims=True) + eps) * w


def _split3(x):
    hi = x.astype(BF16)
    r1 = x - hi.astype(F32)
    mid = r1.astype(BF16)
    lo = (r1 - mid.astype(F32)).astype(BF16)
    return hi, mid, lo


def _dot(a, b):
    return jnp.dot(a, b, preferred_element_type=F32)


def _dot_nt(a, b):
    return lax.dot_general(a, b, (((1,), (1,)), ((), ())), preferred_element_type=F32)


_NZ = D_INNER // PROJ_TN
_NX = CONV_DIM // PROJ_TN
_NG = 2 * D_MODEL // PROJ_TN
_OFF_X = _NZ
_OFF_G = _OFF_X + _NX
_N_COL_TILES = _OFF_G + _NG
_W_TILE0 = (Q_DIM + Q_DIM + ATT_WIDTH) // PROJ_TN


def _qkv_kernel(*refs):
    x_ref, nw_ref, w_ref = refs[:3]
    q_ref, k_ref, kb_ref, v_ref, vb_ref, h_sc = refs[-6:]
    j = pl.program_id(1)

    @pl.when(j == 0)
    def _():
        h_sc[...] = _rms(x_ref[...], nw_ref[...], RMS_EPS).astype(BF16)
        q_ref[...] = (_dot(h_sc[...], w_ref[...]) * (LOG2E * ATT_HEAD_DIM ** -0.5)).astype(BF16)

    @pl.when(j == 1)
    def _():
        res = _dot(h_sc[...], w_ref[...])
        k_ref[...] = res
        kb_ref[...] = res.astype(BF16)

    @pl.when(j == 2)
    def _():
        res = _dot(h_sc[...], w_ref[...])
        v_ref[...] = res
        vb_ref[...] = res.astype(BF16)


def _qkv_proj(x, norm_w, w_all, tm, stacks=None, layer=0):
    t = x.shape[0]
    row = pl.BlockSpec((tm, Q_DIM), lambda i, j: (i, 0))
    stack = pl.BlockSpec((None, tm, Q_DIM), lambda i, j: (layer, i, 0))
    in_specs = [pl.BlockSpec((tm, D_MODEL), lambda i, j: (i, 0)), pl.BlockSpec((1, D_MODEL), lambda i, j: (0, 0)),
                pl.BlockSpec((None, D_MODEL, Q_DIM), lambda i, j: (layer, 0, j))]
    args = [x, norm_w, w_all]
    aliases = {}
    if stacks is not None:
        in_specs += [pl.BlockSpec(memory_space=pl.ANY)] * 2
        aliases = {len(args): 1, len(args) + 1: 3}
        args += list(stacks)
    return pl.pallas_call(
        _qkv_kernel,
        grid=(t // tm, 3),
        in_specs=in_specs,
        out_specs=[row, stack, row, stack, row],
        out_shape=[jax.ShapeDtypeStruct((t, Q_DIM), BF16),
                   jax.ShapeDtypeStruct((DEPTH, t, Q_DIM), F32),
                   jax.ShapeDtypeStruct((t, Q_DIM), BF16),
                   jax.ShapeDtypeStruct((DEPTH, t, ATT_WIDTH), F32),
                   jax.ShapeDtypeStruct((t, ATT_WIDTH), BF16)],
        scratch_shapes=[pltpu.VMEM((tm, D_MODEL), BF16)],
        input_output_aliases=aliases,
        compiler_params=_cparams("arbitrary", "arbitrary"),
        name="qkv_proj",
    )(*args)


def _in_proj_kernel(*refs, tm, fuse_conv):
    x_ref, nw_ref, w_ref, wg_ref, wdt_ref = refs[:5]
    pos = 5
    if fuse_conv:
        cw_ref, cb_ref, hist0_ref = refs[pos:pos + 3]
        pos += 3
    z_ref, xbc_ref, g_ref, dt_ref = refs[pos:pos + 4]
    pos += 4
    if fuse_conv:
        tail_ref = refs[pos]
        pos += 1
    h_sc = refs[pos]
    i = pl.program_id(0)
    j = pl.program_id(1)

    @pl.when(j == 0)
    def _():
        hb = _rms(x_ref[...], nw_ref[...], RMS_EPS).astype(BF16)
        h_sc[...] = hb
        dt_ref[...] = _dot(hb, wdt_ref[...])

    def tile(weight_ref=w_ref):
        return _dot(h_sc[...], weight_ref[...])

    @pl.when(j < _OFF_X)
    def _():
        z_ref[...] = tile().astype(z_ref.dtype)

    if not fuse_conv:
        @pl.when((j >= _OFF_X) & (j < _OFF_G))
        def _():
            xbc_ref[...] = tile()

        @pl.when(j >= _OFF_G)
        def _():
            g_ref[...] = tile(wg_ref).astype(g_ref.dtype)
        return

    hist_sc, raw_sc = refs[pos + 1], refs[pos + 2]

    @pl.when((i == 0) & (j == 0))
    def _():
        for c in range(_NX):
            hist_sc[c] = hist0_ref[:, c * PROJ_TN:(c + 1) * PROJ_TN]

    def activate(jc, slot):
        for c0 in range(0, PROJ_TN, CONV_COLS):
            cs = slice(c0, c0 + CONV_COLS)
            for r0 in range(0, tm, CONV_ROWS):
                before = hist_sc[jc, :, cs] if r0 == 0 else raw_sc[slot, r0 - CONV_PAD:r0, cs]
                xp = jnp.concatenate([before, raw_sc[slot, r0:r0 + CONV_ROWS, cs]], axis=0)
                conv = cw_ref[0:1, cs] * xp
                for tap in range(1, CONV_WIDTH):
                    conv = cw_ref[tap:tap + 1, cs] * xp + pltpu.roll(conv, 1, 0)
                conv = conv[CONV_PAD:] + cb_ref[:, cs]
                xbc_ref[r0:r0 + CONV_ROWS, cs] = (conv * jax.nn.sigmoid(conv)).astype(xbc_ref.dtype)
        last_rows = raw_sc[slot, tm - CONV_PAD:tm, :]
        hist_sc[jc] = last_rows
        tail_ref[...] = last_rows

    @pl.when(j == _OFF_X)
    def _():
        raw_sc[0] = tile()

    for slot in range(2):
        @pl.when((j > _OFF_X) & (j < _OFF_G) & (lax.rem(j - _OFF_X, 2) == slot))
        def _():
            activate(j - _OFF_X - 1, 1 - slot)
            raw_sc[slot] = tile()

    @pl.when(j == _OFF_G)
    def _():
        activate(_NX - 1, (_NX - 1) % 2)
        g_ref[...] = tile(wg_ref).astype(g_ref.dtype)

    @pl.when(j > _OFF_G)
    def _():
        g_ref[...] = tile(wg_ref).astype(g_ref.dtype)


def _in_proj(x, norm_w, w_all, w_gates, w_dt, tm, layer, conv=None):
    t = x.shape[0]
    fuse_conv = conv is not None

    def out_spec(off, n):
        return pl.BlockSpec((tm, PROJ_TN), lambda i, j: (i, jnp.clip(j - off, 0, n - 1)))

    xbc_lag = 1 if fuse_conv else 0

    def xcol(i, j):
        return (0, jnp.clip(j - _OFF_X - xbc_lag, 0, _NX - 1))

    in_specs = [
        pl.BlockSpec((tm, D_MODEL), lambda i, j: (i, 0)),
        pl.BlockSpec((1, D_MODEL), lambda i, j: (0, 0)),
        pl.BlockSpec((None, D_MODEL, PROJ_TN),
                     lambda i, j: (layer, 0, _W_TILE0 + jnp.clip(j, 0, _OFF_G - 1))),
        pl.BlockSpec((None, D_MODEL, PROJ_TN), lambda i, j: (layer, 0, jnp.clip(j - _OFF_G, 0, _NG - 1))),
        pl.BlockSpec((D_MODEL, LANES), lambda i, j: (0, 0)),
    ]
    args = [x, norm_w, w_all, w_gates, w_dt]
    if fuse_conv:
        in_specs += [pl.BlockSpec((CONV_WIDTH, PROJ_TN), xcol), pl.BlockSpec((1, PROJ_TN), xcol),
                     pl.BlockSpec((CONV_PAD, CONV_DIM), lambda i, j: (0, 0))]
        args += list(conv)
    out_shape = [
        jax.ShapeDtypeStruct((t, D_INNER), BF16),
        jax.ShapeDtypeStruct((t, CONV_DIM), BF16 if fuse_conv else F32),
        jax.ShapeDtypeStruct((t, 2 * D_MODEL), BF16),
        jax.ShapeDtypeStruct((t, LANES), F32),
    ]
    out_specs = [
        out_spec(0, _NZ), out_spec(_OFF_X + xbc_lag, _NX), out_spec(_OFF_G, _NG),
        pl.BlockSpec((tm, LANES), lambda i, j: (i, 0)),
    ]
    scratch = [pltpu.VMEM((tm, D_MODEL), BF16)]
    if fuse_conv:
        out_shape.append(jax.ShapeDtypeStruct((t // tm, CONV_PAD, CONV_DIM), F32))
        out_specs.append(pl.BlockSpec((None, CONV_PAD, PROJ_TN), lambda i, j: (i,) + xcol(i, j)))
        scratch += [pltpu.VMEM((_NX, CONV_PAD, PROJ_TN), F32), pltpu.VMEM((2, tm, PROJ_TN), F32)]
    return pl.pallas_call(
        functools.partial(_in_proj_kernel, tm=tm, fuse_conv=fuse_conv),
        grid=(t // tm, _N_COL_TILES),
        in_specs=in_specs,
        out_specs=out_specs,
        out_shape=out_shape,
        scratch_shapes=scratch,
        compiler_params=_cparams("arbitrary", "arbitrary"),
        name="in_proj",
    )(*args)


def _stack_q(q):
    lane = lax.broadcasted_iota(jnp.int32, q.shape, 1)
    zero = jnp.zeros_like(q)
    return jnp.concatenate([jnp.where(lane < ATT_HEAD_DIM, q, zero),
                            jnp.where(lane >= ATT_HEAD_DIM, q, zero)], axis=0)


def _diff_epilogue(o, n, lamp_ref, subln_ref, lam_init):
    lp = lamp_ref[...]
    lam = (jnp.exp(jnp.sum(lp[0:1] * lp[1:2], axis=-1, keepdims=True))
           - jnp.exp(jnp.sum(lp[2:3] * lp[3:4], axis=-1, keepdims=True)) + lam_init)
    a = o[:n] - lam * o[n:]
    return _rms(a, subln_ref[...], ATT_SUBLN_EPS) * (1.0 - lam_init)


def _attn_prompt_kernel(lamp_ref, subln_ref, q_ref, k_ref, v_ref, o_ref, m_sc, l_sc, acc_sc, s_sc, *,
                        tq, tk, nq, unroll, lam_init):
    qi = pl.program_id(1)
    half = tq // 2
    qs = jnp.concatenate([_stack_q(q_ref[0:half, :]), _stack_q(q_ref[half:tq, :])], axis=0)
    m_sc[...] = jnp.full(m_sc.shape, -jnp.inf, F32)
    l_sc[...] = jnp.zeros(l_sc.shape, F32)
    acc_sc[...] = jnp.zeros(acc_sc.shape, F32)
    n_lane_tiles = tk // LANES
    full_blocks = qi * (tq // tk)
    PLAIN, MASKED, MASKED_LATE = "plain", "masked", "masked_late"

    def scores(kblk, kind):
        start = pl.multiple_of(kblk * tk, tk)
        rows = qs[tq:] if kind == MASKED_LATE else qs
        s = _dot_nt(rows, k_ref[pl.ds(start, tk), :])
        if kind != PLAIN:
            r = lax.broadcasted_iota(jnp.int32, s.shape, 0)
            c = lax.broadcasted_iota(jnp.int32, s.shape, 1)
            if kind == MASKED_LATE:
                q_row = half + jnp.where(r >= half, r - half, r)
            else:
                q_row = jnp.where(r >= tq, half, 0) + lax.rem(r, half)
            k_row = c + (kblk - full_blocks) * tk
            s = jnp.where(k_row // CHUNK <= q_row // CHUNK, s, -jnp.inf)
        return s

    def absorb(kblk, s, state):
        if s.shape[0] != 2 * tq:
            late = absorb_rows(kblk, s, tuple(a[tq:] for a in state))
            return tuple(jnp.concatenate([a[:tq], b], axis=0) for a, b in zip(state, late))
        return absorb_rows(kblk, s, state)

    def absorb_rows(kblk, s, state):
        m_prev, l_prev, acc_prev = state
        vb = v_ref[pl.ds(pl.multiple_of(kblk * tk, tk), tk), :]
        m_new = jnp.maximum(m_prev, jnp.max(s, axis=-1, keepdims=True))
        alpha = jnp.exp2(m_prev - m_new)
        p = jnp.exp2(s - jnp.concatenate([m_new] * n_lane_tiles, axis=1))
        pv = _dot(p.astype(BF16), jnp.concatenate([vb, jnp.ones((tk, LANES), BF16)], axis=1))
        return m_new, alpha * l_prev + pv[:, ATT_V_DIM:], alpha * acc_prev + pv[:, :ATT_V_DIM]

    def pipeline(first, produce):
        state = (m_sc[...], l_sc[...], acc_sc[...])
        s = s_sc[...]
        for i, kind in enumerate(produce):
            s_next = None if kind is None else scores(first + i + 1, kind)
            state = absorb(first + i, s, state)
            s = s_next
        if s is not None:
            s_sc[...] = s
        m_sc[...], l_sc[...], acc_sc[...] = state

    diag = [MASKED, MASKED_LATE]

    @pl.when(qi == 0)
    def _():
        s_sc[...] = scores(0, MASKED)
        pipeline(0, diag[1:] + [None])

    @pl.when(qi > 0)
    def _():
        s_sc[...] = scores(0, PLAIN)

    def body(t, carry):
        pipeline(unroll * t, [PLAIN] * unroll)
        return carry

    plain_steps = jnp.maximum(full_blocks - 1, 0)
    lax.fori_loop(0, plain_steps // unroll, body, 0)
    for r in sorted({(len(diag) * q - 1) % unroll for q in range(1, nq)}):
        @pl.when((qi > 0) & (plain_steps % unroll == r))
        def _():
            pipeline(full_blocks - 1 - r, [PLAIN] * r + diag + [None])

    o = acc_sc[...] * (1.0 / l_sc[...])
    for g in range(2):
        o_ref[g * half:(g + 1) * half, :] = _diff_epilogue(
            o[g * tq:(g + 1) * tq], half, lamp_ref, subln_ref, lam_init).astype(o_ref.dtype)


def _attn_prompt(q, kb, vb, lamp, subln, lam_init, tq, tk, unroll):
    t = q.shape[0]
    assert t % tq == 0 and tq == 2 * tk and tk % LANES == 0 and tk % CHUNK == 0
    kern = functools.partial(_attn_prompt_kernel, tq=tq, tk=tk, nq=t // tq, unroll=unroll, lam_init=lam_init)
    return pl.pallas_call(
        kern,
        grid=(ATT_HEADS, t // tq),
        in_specs=[
            pl.BlockSpec((4, ATT_HEAD_DIM), lambda h, i: (0, 0)),
            pl.BlockSpec((1, ATT_V_DIM), lambda h, i: (0, 0)),
            pl.BlockSpec((tq, LANES), lambda h, i: (i, h)),
            pl.BlockSpec((t, LANES), lambda h, i: (0, h)),
            pl.BlockSpec((t, LANES), lambda h, i: (0, h)),
        ],
        out_specs=pl.BlockSpec((tq, LANES), lambda h, i: (i, h)),
        out_shape=jax.ShapeDtypeStruct((t, ATT_WIDTH), BF16),
        scratch_shapes=[pltpu.VMEM((2 * tq, LANES), F32), pltpu.VMEM((2 * tq, LANES), F32),
                        pltpu.VMEM((2 * tq, ATT_V_DIM), F32), pltpu.VMEM((2 * tq, tk), F32)],
        compiler_params=_cparams("arbitrary", "arbitrary"),
        name="attn_prompt",
    )(lamp, subln, q, kb, vb)


def _attn_sample_kernel(lamp_ref, subln_ref, q_ref, kt_ref, vc_ref, kn_ref, vn_ref, o_ref, m_sc, l_sc, acc_sc, *,
                        n, ts, lam_init):
    piece = pl.program_id(1)

    @pl.when(piece == 0)
    def _():
        m_sc[...] = jnp.full(m_sc.shape, -jnp.inf, F32)
        l_sc[...] = jnp.zeros(l_sc.shape, F32)
        acc_sc[...] = jnp.zeros(acc_sc.shape, F32)

    rows = 2 * n

    def head_cols(h):
        return slice(h * LANES, (h + 1) * LANES)

    def update(scores_of, values_of):
        s = jnp.concatenate([scores_of(_stack_q(q_ref[0, :, head_cols(h)]), h) for h in range(ATT_HEADS)], axis=0)
        width = s.shape[1]
        m_prev = m_sc[...]
        m_new = jnp.maximum(m_prev, jnp.max(s, axis=-1, keepdims=True))
        alpha = jnp.exp2(m_prev - m_new)
        m_wide = m_new[:, :width] if width <= LANES else jnp.concatenate([m_new] * (width // LANES), axis=1)
        p = jnp.exp2(s - m_wide)
        l_sc[...] = alpha * l_sc[...] + jnp.sum(p, axis=-1, keepdims=True)
        pb = p.astype(BF16)
        pv = jnp.concatenate([_dot(pb[h * rows:(h + 1) * rows], values_of(h)) for h in range(ATT_HEADS)], axis=0)
        acc_sc[...] = alpha * acc_sc[...] + pv
        m_sc[...] = m_new

    update(lambda qs, h: _dot(qs, kt_ref[0, 0, head_cols(h), :].astype(BF16)),
           lambda h: vc_ref[0, 0, pl.ds(h, ts, stride=ATT_HEADS), :].astype(BF16))

    @pl.when(piece == pl.num_programs(1) - 1)
    def _():
        update(lambda qs, h: _dot_nt(qs, kn_ref[0, :, head_cols(h)]), lambda h: vn_ref[0, :, head_cols(h)])
        o = acc_sc[...] * (1.0 / l_sc[...])
        for h in range(ATT_HEADS):
            o_ref[0, :, head_cols(h)] = _diff_epilogue(
                o[h * rows:(h + 1) * rows], n, lamp_ref, subln_ref, lam_init).astype(o_ref.dtype)


def _attn_sample(q, cache_kt, cache_v, kb, vb, lamp, subln, lam_init, layer):
    b, n, _ = q.shape
    past = cache_kt.shape[3]
    ts = min(past, SAMPLE_TS)
    assert past % CHUNK == 0 and n <= CHUNK and past % ts == 0
    kern = functools.partial(_attn_sample_kernel, n=n, ts=ts, lam_init=lam_init)
    row_spec = pl.BlockSpec((1, n, Q_DIM), lambda bi, s: (bi, 0, 0))
    return pl.pallas_call(
        kern,
        grid=(b, past // ts),
        in_specs=[
            pl.BlockSpec((4, ATT_HEAD_DIM), lambda bi, s: (0, 0)),
            pl.BlockSpec((1, ATT_V_DIM), lambda bi, s: (0, 0)),
            row_spec,
            pl.BlockSpec((1, 1, Q_DIM, ts), lambda bi, s: (layer, bi, 0, s)),
            pl.BlockSpec((1, 1, ts * ATT_HEADS, ATT_V_DIM), lambda bi, s: (layer, bi, s, 0)),
            row_spec,
            row_spec,
        ],
        out_specs=row_spec,
        out_shape=jax.ShapeDtypeStruct((b, n, ATT_WIDTH), BF16),
        scratch_shapes=[pltpu.VMEM((ATT_HEADS * 2 * n, LANES), F32)] * 3,
        compiler_params=_cparams("arbitrary", "arbitrary"),
        name="attn_sample",
    )(lamp, subln, q, cache_kt, cache_v, kb, vb)


def _ssd_kernel(*refs, lc, valid_last, conv_done):
    if conv_done:
        (xbc_ref, z_ref, dt_ref, st0_ref, dtb_ref, alog_ref, dsk_ref, nrm_ref, expand_ref,
         y_ref, sout_ref, st_sc) = refs
    else:
        (xbc_ref, z_ref, dt_ref, st0_ref, dtb_ref, alog_ref, dsk_ref, nrm_ref, expand_ref,
         cbuf_ref, cw_ref, cb_ref, y_ref, sout_ref, cout_ref, st_sc, xpad_sc) = refs
    c = pl.program_id(1)
    nchunks = pl.num_programs(1)

    @pl.when(c == 0)
    def _():
        for g in range(SSM_GROUPS):
            st_sc[g] = st0_ref[0, g].T

    if conv_done:
        act = xbc_ref[0]
    else:
        @pl.when(c == 0)
        def _():
            xpad_sc[0:CONV_PAD, :] = jnp.zeros((CONV_PAD, CONV_DIM), F32)
            xpad_sc[CONV_PAD - (CONV_WIDTH - 1):CONV_PAD, :] = cbuf_ref[0]

        xpad_sc[CONV_PAD:CONV_PAD + lc, :] = xbc_ref[0]
        xp = xpad_sc[...]
        conv = cw_ref[0:1, :] * xp
        for j in range(1, CONV_WIDTH):
            conv = cw_ref[j:j + 1, :] * xp + pltpu.roll(conv, 1, 0)
        conv = conv[CONV_PAD:CONV_PAD + lc] + cb_ref[...]
        act = conv * jax.nn.sigmoid(conv)

        @pl.when(c == nchunks - 1)
        def _():
            end = CONV_PAD + valid_last
            cout_ref[0] = xpad_sc[end - (CONV_WIDTH - 1):end, :]

        xpad_sc[0:CONV_PAD, :] = xpad_sc[lc:lc + CONV_PAD, :]

    dtr = dt_ref[0] + dtb_ref[...]
    dt = jnp.maximum(dtr, 0.0) + jnp.log1p(jnp.exp(-jnp.abs(dtr)))
    if valid_last < lc:
        row = lax.broadcasted_iota(jnp.int32, dt.shape, 0)
        dt = jnp.where(row < valid_last, dt, 0.0)
    a = dt * (-jnp.exp(alog_ref[...]))
    rr = lax.broadcasted_iota(jnp.int32, (lc, lc), 0)
    cc = lax.broadcasted_iota(jnp.int32, (lc, lc), 1)
    causal = rr >= cc
    tri = jnp.where(causal, 1.0, 0.0).astype(BF16)
    a_cs = sum(_dot(tri, part) for part in _split3(a))
    a_last = a_cs[lc - 1:lc, :]
    a_cs_t = a_cs.T

    expand = expand_ref[...]
    stack = jnp.concatenate([dt, dt * jnp.exp(a_last - a_cs), jnp.exp(a_cs)], axis=0)
    ex = _dot(stack.astype(BF16), expand)
    w_dt = ex[0:lc]
    w_state = ex[lc:2 * lc]
    w_off = ex[2 * lc:3 * lc]
    carry = jnp.broadcast_to(jnp.exp(a_last), (2 * SUBLANES, LANES))
    w_carry = sum(_dot(part, expand) for part in _split3(carry))[0:1]

    xs = act[:, :D_INNER].astype(F32)
    xdt = xs * w_dt
    xst = xs * w_state
    lane = lax.broadcasted_iota(jnp.int32, (lc, LANES), 1)
    lo_half = lane < SSM_HEAD_DIM

    for g in range(SSM_GROUPS):
        gs = slice(g * GROUP_WIDTH, (g + 1) * GROUP_WIDTH)
        bg = act[:, D_INNER + g * SSM_STATE:D_INNER + (g + 1) * SSM_STATE]
        cg = act[:, D_INNER + SSM_GROUPS * SSM_STATE + g * SSM_STATE:
                 D_INNER + SSM_GROUPS * SSM_STATE + (g + 1) * SSM_STATE]
        bgb = bg.astype(BF16)
        cgb = cg.astype(BF16)
        cb = _dot_nt(cgb, bgb)
        st_prev = st_sc[g]
        y_off = _dot(cgb, st_prev.astype(BF16)) * w_off[:, gs]
        st_sc[g] = w_carry[:, gs] * st_prev + _dot(bg.astype(F32).T.astype(BF16), xst[:, gs].astype(BF16))

        pairs = []
        for pr in range(SSM_HEADS_PER_GROUP // 2):
            x_pair = xdt[:, g * GROUP_WIDTH + pr * LANES:g * GROUP_WIDTH + (pr + 1) * LANES]
            y_pair = None
            for half in range(2):
                h = g * SSM_HEADS_PER_GROUP + pr * 2 + half
                seg = a_cs[:, h:h + 1] - a_cs_t[h:h + 1, :]
                decay = jnp.exp(jnp.where(causal, seg, -jnp.inf))
                mix = (cb * decay).astype(BF16)
                keep = lo_half if half == 0 else jnp.logical_not(lo_half)
                contrib = _dot(mix, jnp.where(keep, x_pair, 0.0).astype(BF16))
                y_pair = contrib if y_pair is None else y_pair + contrib
            pairs.append(y_pair)
        y_g = jnp.concatenate(pairs, axis=1) + y_off + dsk_ref[:, gs] * xs[:, gs]

        zg = z_ref[0, :, gs].astype(F32)
        yz = y_g * (zg * jax.nn.sigmoid(zg))
        y_ref[0, :, gs] = _rms(yz, nrm_ref[:, gs], SSM_NORM_EPS).astype(y_ref.dtype)

    @pl.when(c == nchunks - 1)
    def _():
        for g in range(SSM_GROUPS):
            sout_ref[0, g] = st_sc[g].T


def _ssd(xbc, z, dt, state0, p, lc, valid_last, conv_buf=None):
    b, l, _ = xbc.shape
    conv_done = conv_buf is None
    assert l % lc == 0 and CONV_WIDTH - 1 <= valid_last <= lc
    assert valid_last == lc or l == lc
    state_spec = pl.BlockSpec((1, SSM_GROUPS, GROUP_WIDTH, SSM_STATE), lambda bi, c: (bi, 0, 0, 0))
    in_specs = [
        pl.BlockSpec((1, lc, CONV_DIM), lambda bi, c: (bi, c, 0)),
        pl.BlockSpec((1, lc, D_INNER), lambda bi, c: (bi, c, 0)),
        pl.BlockSpec((1, lc, LANES), lambda bi, c: (bi, c, 0)),
        state_spec,
        pl.BlockSpec((1, LANES), lambda bi, c: (0, 0)),
        pl.BlockSpec((1, LANES), lambda bi, c: (0, 0)),
        pl.BlockSpec((1, D_INNER), lambda bi, c: (0, 0)),
        pl.BlockSpec((1, D_INNER), lambda bi, c: (0, 0)),
        pl.BlockSpec((LANES, D_INNER), lambda bi, c: (0, 0)),
    ]
    args = [xbc, z, dt, state0, p["dt_bias"], p["a_log"], p["d_skip"], p["ssm_norm"], p["expand"]]
    out_specs = [pl.BlockSpec((1, lc, D_INNER), lambda bi, c: (bi, c, 0)), state_spec]
    out_shape = [jax.ShapeDtypeStruct((b, l, D_INNER), BF16),
                 jax.ShapeDtypeStruct((b, SSM_GROUPS, GROUP_WIDTH, SSM_STATE), F32)]
    scratch = [pltpu.VMEM((SSM_GROUPS, SSM_STATE, GROUP_WIDTH), F32)]
    if not conv_done:
        conv_spec = pl.BlockSpec((1, CONV_WIDTH - 1, CONV_DIM), lambda bi, c: (bi, 0, 0))
        in_specs += [conv_spec, pl.BlockSpec((CONV_WIDTH, CONV_DIM), lambda bi, c: (0, 0)),
                     pl.BlockSpec((1, CONV_DIM), lambda bi, c: (0, 0))]
        args += [conv_buf, p["conv_w"], p["conv_b"]]
        out_specs.append(conv_spec)
        out_shape.append(jax.ShapeDtypeStruct((b, CONV_WIDTH - 1, CONV_DIM), F32))
        scratch.append(pltpu.VMEM((lc + CONV_PAD, CONV_DIM), F32))
    return pl.pallas_call(
        functools.partial(_ssd_kernel, lc=lc, valid_last=valid_last, conv_done=conv_done),
        grid=(b, l // lc),
        in_specs=in_specs,
        out_specs=out_specs,
        out_shape=out_shape,
        scratch_shapes=scratch,
        compiler_params=_cparams("arbitrary", "arbitrary"),
        name="ssd",
    )(*args)


FFN_SPLIT = 2
FFN_TF = D_FF // FFN_SPLIT


def _mix_ffn_kernel(x_ref, att_ref, y_ref, g_ref, wa_ref, ws_ref, wo_ref, nw_ref, wgu_ref, wd_ref, nf_ref,
                    *out_refs, final_norm):
    ba = _dot(att_ref[...], wa_ref[...])
    bs = _dot(y_ref[...], ws_ref[...])
    g = jax.nn.sigmoid(g_ref[...].astype(F32))
    merged = g[:, :D_MODEL] * ba + g[:, D_MODEL:] * bs
    x = x_ref[...] + _dot(merged.astype(BF16), wo_ref[...])
    hb = _rms(x, nw_ref[...], RMS_EPS).astype(BF16)
    acc = x
    for c in range(FFN_SPLIT):
        gt = _dot(hb, wgu_ref[:, c * FFN_TF:(c + 1) * FFN_TF])
        up = _dot(hb, wgu_ref[:, D_FF + c * FFN_TF:D_FF + (c + 1) * FFN_TF])
        act = (gt * jax.nn.sigmoid(gt) * up).astype(BF16)
        acc = acc + _dot(act, wd_ref[c * FFN_TF:(c + 1) * FFN_TF, :])
    out_refs[0][...] = acc
    if final_norm:
        out_refs[1][...] = _rms(acc, nf_ref[...], RMS_EPS)


def _mix_ffn(x, att, y, gates, p, norm_final, final_norm, tm):
    t = x.shape[0]
    row_spec = pl.BlockSpec((tm, D_MODEL), lambda i: (i, 0))
    vec_spec = pl.BlockSpec((1, D_MODEL), lambda i: (0, 0))
    n_out = 2 if final_norm else 1
    return pl.pallas_call(
        functools.partial(_mix_ffn_kernel, final_norm=final_norm),
        grid=(t // tm,),
        in_specs=[
            row_spec,
            pl.BlockSpec((tm, ATT_WIDTH), lambda i: (i, 0)),
            pl.BlockSpec((tm, D_INNER), lambda i: (i, 0)),
            pl.BlockSpec((tm, 2 * D_MODEL), lambda i: (i, 0)),
            _const_spec((ATT_WIDTH, D_MODEL)),
            _const_spec((D_INNER, D_MODEL)),
            _const_spec((D_MODEL, D_MODEL)),
            vec_spec,
            _const_spec((D_MODEL, 2 * D_FF)),
            _const_spec((D_FF, D_MODEL)),
            vec_spec,
        ],
        out_specs=(row_spec,) * n_out,
        out_shape=(jax.ShapeDtypeStruct((t, D_MODEL), F32),) * n_out,
        compiler_params=_cparams("arbitrary"),
        name="mix_ffn",
    )(x, att, y, gates, p["wa"], p["ws"], p["wo"], p["norm_ffn"], p["wgu"], p["wd"], norm_final)


def _layer_params(l, norm_mix, w_in, lambda_q1, lambda_k1, lambda_q2, lambda_k2, attn_subln, conv_w, conv_b,
                  dt_bias, a_log, d_skip, ssm_norm, w_branch_att, w_branch_ssd, w_out, norm_ffn, w_gate_up,
                  w_down):
    n_main = Q_DIM + Q_DIM + ATT_WIDTH + D_INNER + CONV_DIM
    w_dt = jnp.pad(w_in[l, :, n_main:n_main + SSM_HEADS], ((0, 0), (0, LANES - SSM_HEADS)))
    head_of_channel = jnp.arange(D_INNER) // SSM_HEAD_DIM
    return {
        "norm_mix": norm_mix[l][None], "w_dt": w_dt,
        "lamp": jnp.stack([lambda_q1[l], lambda_k1[l], lambda_q2[l], lambda_k2[l]]),
        "subln": attn_subln[l][None],
        "conv_w": conv_w[l], "conv_b": conv_b[l][None],
        "dt_bias": jnp.pad(dt_bias[l], (0, LANES - SSM_HEADS))[None],
        "a_log": jnp.pad(a_log[l], (0, LANES - SSM_HEADS))[None],
        "d_skip": jnp.repeat(d_skip[l], SSM_HEAD_DIM)[None],
        "ssm_norm": ssm_norm[l][None],
        "expand": (jnp.arange(LANES)[:, None] == head_of_channel[None, :]).astype(BF16),
        "wa": w_branch_att[l].astype(BF16), "ws": w_branch_ssd[l].astype(BF16), "wo": w_out[l].astype(BF16),
        "norm_ffn": norm_ffn[l][None], "wgu": w_gate_up[l].astype(BF16), "wd": w_down[l].astype(BF16),
    }


def _pad_time(x, b, n, lc):
    return jnp.pad(x.reshape(b, n, x.shape[-1]), ((0, 0), (0, lc - n), (0, 0)))


def kernel(x_prompt, x_sample, cache_k, cache_v, state_conv, state_ssm, norm_mix, w_in, lambda_q1, lambda_k1,
           lambda_q2, lambda_k2, attn_subln, conv_w, conv_b, dt_bias, a_log, d_skip, ssm_norm, w_branch_att,
           w_branch_ssd, w_out, norm_ffn, w_gate_up, w_down, norm_final):
    bp, lp, _ = x_prompt.shape
    bs, ls, _ = x_sample.shape
    assert bp == 1
    past = cache_k.shape[2]
    tp, ts = bp * lp, bs * ls
    xp = x_prompt.reshape(tp, D_MODEL)
    xs = x_sample.reshape(ts, D_MODEL)
    cache_k = jnp.transpose(cache_k, (0, 1, 3, 4, 5, 2)).reshape(DEPTH, bs, Q_DIM, past)
    cache_v = cache_v.reshape(DEPTH, bs, past * ATT_HEADS, ATT_V_DIM)
    state_ssm = state_ssm.reshape(DEPTH, bs, SSM_GROUPS, GROUP_WIDTH, SSM_STATE)
    conv0 = jnp.zeros((CONV_PAD, CONV_DIM), F32)
    ssm0 = jnp.zeros((bp, SSM_GROUPS, GROUP_WIDTH, SSM_STATE), F32)
    nf = norm_final[None]

    tm_p = 1024 if tp % 1024 == 0 else 128
    tm_row = 512 if tp % 512 == 0 else 128
    tq = ATTN_TQ if lp % ATTN_TQ == 0 else CHUNK
    tk = tq // 2

    w_in = w_in.astype(BF16)
    w_gates = w_in[:, :, w_in.shape[2] - 2 * D_MODEL:]
    conv_p, ssm_p, conv_s, ssm_s = [], [], [], []
    kv_p = kv_s = None
    yp = ys = None
    for l in range(DEPTH):
        p = _layer_params(l, norm_mix, w_in, lambda_q1, lambda_k1, lambda_q2, lambda_k2, attn_subln, conv_w,
                          conv_b, dt_bias, a_log, d_skip, ssm_norm, w_branch_att, w_branch_ssd, w_out,
                          norm_ffn, w_gate_up, w_down)
        lam_init = 0.8 - 0.6 * math.exp(-0.3 * l)
        last = l == DEPTH - 1

        q, k_all, kb, v_all, vb = _qkv_proj(xp, p["norm_mix"], w_in, tm_p, stacks=kv_p, layer=l)
        kv_p = (k_all, v_all)
        z, act, gates, dt, tail = _in_proj(xp, p["norm_mix"], w_in, w_gates, p["w_dt"], tm_p, l,
                                           conv=(p["conv_w"], p["conv_b"], conv0))
        att = _attn_prompt(q, kb, vb, p["lamp"], p["subln"], lam_init, tq, tk, ATTN_UNROLL)
        y, s_new = _ssd(act.reshape(bp, lp, CONV_DIM), z.reshape(bp, lp, D_INNER), dt.reshape(bp, lp, LANES),
                        ssm0, p, SSD_CHUNK, SSD_CHUNK)
        res = _mix_ffn(xp, att, y.reshape(tp, D_INNER), gates, p, nf, last, tm_row)
        xp = res[0]
        if last:
            yp = res[1]
        conv_p.append(tail[-1, CONV_PAD - (CONV_WIDTH - 1):][None])
        ssm_p.append(s_new.reshape(bp, SSM_GROUPS, SSM_HEADS_PER_GROUP, SSM_HEAD_DIM, SSM_STATE))

        q, k_all, kb, v_all, vb = _qkv_proj(xs, p["norm_mix"], w_in, ts, stacks=kv_s, layer=l)
        kv_s = (k_all, v_all)
        z, xbc, gates, dt = _in_proj(xs, p["norm_mix"], w_in, w_gates, p["w_dt"], ts, l)
        att = _attn_sample(q.reshape(bs, ls, Q_DIM), cache_k, cache_v, kb.reshape(bs, ls, Q_DIM),
                           vb.reshape(bs, ls, ATT_WIDTH), p["lamp"], p["subln"], lam_init, l)
        y, s_new, c_new = _ssd(_pad_time(xbc, bs, ls, SSD_PAD_ROWS), _pad_time(z, bs, ls, SSD_PAD_ROWS),
                               _pad_time(dt, bs, ls, SSD_PAD_ROWS), state_ssm[l], p, SSD_PAD_ROWS, ls,
                               conv_buf=state_conv[l])
        res = _mix_ffn(xs, att.reshape(ts, ATT_WIDTH), y[:, :ls].reshape(ts, D_INNER), gates, p, nf, last, ts)
        xs = res[0]
        if last:
            ys = res[1]
        conv_s.append(c_new)
        ssm_s.append(s_new.reshape(bs, SSM_GROUPS, SSM_HEADS_PER_GROUP, SSM_HEAD_DIM, SSM_STATE))

    return (yp.reshape(bp, lp, D_MODEL), ys.reshape(bs, ls, D_MODEL),
            kv_p[0].reshape(DEPTH, bp, lp, ATT_HEADS, 2, ATT_HEAD_DIM),
            kv_p[1].reshape(DEPTH, bp, lp, ATT_HEADS, ATT_V_DIM),
            jnp.stack(conv_p), jnp.stack(ssm_p),
            kv_s[0].reshape(DEPTH, bs, ls, ATT_HEADS, 2, ATT_HEAD_DIM),
            kv_s[1].reshape(DEPTH, bs, ls, ATT_HEADS, ATT_V_DIM),
            jnp.stack(conv_s), jnp.stack(ssm_s))
```
